```python
import jax, jax.numpy as jnp
from jax import lax
import numpy as np

D_MODEL = 2048
BATCH = 1
SEQ = 16384
DEPTH = 2

GRID_W = 64
EPS = 1e-6
ATT_HEADS = 8
ATT_KV_HEADS = 2
ATT_HEAD_DIM = D_MODEL // 16
ROPE_THETA = 10000.0
Q_BLOCK = 128
ML_HEADS = 4
ML_HEAD_DIM = D_MODEL // 8
ML_CHUNK = 128
CONV_W = 5
POOL_WINDOWS = (2, 4, 8, 16)
POOL_GROUPS = 4
POOL_DIM = D_MODEL // POOL_GROUPS
PEER_HEADS = 8
PEER_NKEYS = 128
PEER_EXPERTS = PEER_NKEYS * PEER_NKEYS
PEER_TOPK = 16
PEER_DKEY = 256
PEER_BLOCK = 128

N_EVEN = (DEPTH + 1) // 2
N_ODD = DEPTH // 2

ATT_Q = ATT_HEADS * ATT_HEAD_DIM
ATT_KV = ATT_KV_HEADS * ATT_HEAD_DIM
ML_W = ML_HEADS * ML_HEAD_DIM
N_GATE_COLS = 4 * ML_HEADS
MIX_W = ATT_Q + ML_W
IN_SPLITS = [ATT_Q, ATT_Q + ATT_KV, ATT_Q + 2 * ATT_KV, ATT_Q + 2 * ATT_KV + 2 * ML_W,
             ATT_Q + 2 * ATT_KV + 3 * ML_W, ATT_Q + 2 * ATT_KV + 4 * ML_W]
IN_WIDTH = ATT_Q + 2 * ATT_KV + 4 * ML_W + N_GATE_COLS

kernel_name = 'hybrid_gqa_mlstm_pool_peer_encoder'


def rmsnorm(x, g):
    xf = x.astype(jnp.float32)
    y = xf * lax.rsqrt(jnp.mean(xf * xf, axis=-1, keepdims=True) + EPS)
    return (y * g.astype(jnp.float32)).astype(x.dtype)


def axial_rope_tables(seq):
    rows = seq // GRID_W
    r, c = jnp.meshgrid(jnp.arange(rows), jnp.arange(GRID_W), indexing='ij')
    pos = jnp.stack([r.reshape(-1), c.reshape(-1)], axis=-1).astype(jnp.float32)
    nf = ATT_HEAD_DIM // 4
    inv = ROPE_THETA ** (-jnp.arange(nf, dtype=jnp.float32) / nf)
    ang = pos[:, :, None] * inv
    return jnp.cos(ang), jnp.sin(ang)


def apply_rope(x, cos, sin):
    b, h, s, d = x.shape
    xr = x.reshape(b, h, s, 2, 2, d // 4)
    x1, x2 = xr[..., 0, :], xr[..., 1, :]
    c, sn = cos.astype(x.dtype), sin.astype(x.dtype)
    out = jnp.stack([x1 * c - x2 * sn, x2 * c + x1 * sn], axis=-2)
    return out.reshape(b, h, s, d)


def gqa_attention(q, k, v):
    b, hq, s, d = q.shape
    g = hq // ATT_KV_HEADS
    nb = s // Q_BLOCK
    qb = q.reshape(b, ATT_KV_HEADS, g, nb, Q_BLOCK, d).transpose(3, 0, 1, 2, 4, 5)
    scale = d ** -0.5

    def block(qi):
        sc = jnp.einsum('bkgqd,bksd->bkgqs', qi, k).astype(jnp.float32) * scale
        p = jax.nn.softmax(sc, axis=-1).astype(v.dtype)
        return jnp.einsum('bkgqs,bksd->bkgqd', p, v)

    o = lax.map(block, qb)
    return o.transpose(1, 0, 4, 2, 3, 5).reshape(b, s, hq * d)


def centred_dwconv(x, w):
    s = x.shape[1]
    p = CONV_W // 2
    xp = jnp.pad(x, ((0, 0), (p, p), (0, 0)))
    return sum(xp[:, j:j + s] * w[j] for j in range(CONV_W))


def mlstm_chunkwise(q, k, v, i_pre, logf):
    b, h, s, d = q.shape
    L = ML_CHUNK
    nc = s // L
    q, k, v = (t.reshape(b, h, nc, L, d) for t in (q, k, v))
    i_pre = i_pre.reshape(b, h, nc, L)
    logf = logf.reshape(b, h, nc, L)
    bcum = jnp.cumsum(logf, axis=-1)
    g = bcum[..., -1]
    a = g[..., None] - bcum + i_pre
    a_max = jnp.max(a, axis=-1)
    w = jnp.exp(a - a_max[..., None])
    C_loc = jnp.einsum('bhcl,bhcld,bhcle->bhcde', w, k, v)
    n_loc = jnp.einsum('bhcl,bhcld->bhcd', w, k)

    def step(carry, xs):
        C, n, m = carry
        g_c, am_c, Cl, nl = xs
        m_new = jnp.maximum(g_c + m, am_c)
        dec = jnp.exp(g_c + m - m_new)
        add = jnp.exp(am_c - m_new)
        C_new = dec[..., None, None] * C + add[..., None, None] * Cl
        n_new = dec[..., None] * n + add[..., None] * nl
        return (C_new, n_new, m_new), (C, n, m)

    init = (jnp.zeros((b, h, d, d), jnp.float32), jnp.zeros((b, h, d), jnp.float32),
            jnp.zeros((b, h), jnp.float32))
    xs = (jnp.moveaxis(g, 2, 0), jnp.moveaxis(a_max, 2, 0),
          jnp.moveaxis(C_loc, 2, 0), jnp.moveaxis(n_loc, 2, 0))
    _, (C_prev, n_prev, m_prev) = lax.scan(step, init, xs)
    C_prev = jnp.moveaxis(C_prev, 0, 2)
    n_prev = jnp.moveaxis(n_prev, 0, 2)
    m_prev = jnp.moveaxis(m_prev, 0, 2)

    lower = jnp.tril(jnp.ones((L, L), bool))
    Dlog = bcum[..., :, None] - bcum[..., None, :] + i_pre[..., None, :]
    Dlog = jnp.where(lower, Dlog, -jnp.inf)
    inter_log = bcum + m_prev[..., None]
    m_j = jnp.maximum(inter_log, jnp.max(Dlog, axis=-1))
    Sqk = jnp.einsum('bhcjd,bhcrd->bhcjr', q, k) * jnp.exp(Dlog - m_j[..., None])
    inter_w = jnp.exp(inter_log - m_j)
    num = (inter_w[..., None] * jnp.einsum('bhcjd,bhcde->bhcje', q, C_prev)
           + jnp.einsum('bhcjr,bhcre->bhcje', Sqk, v))
    den = inter_w * jnp.einsum('bhcjd,bhcd->bhcj', q, n_prev) + jnp.sum(Sqk, axis=-1)
    out = num / jnp.maximum(jnp.abs(den), jnp.exp(-m_j))[..., None]
    return out.reshape(b, h, s, d)


def mlstm_bidir(q, k, v, i_f, f_f, i_b, f_b):
    h_fwd = mlstm_chunkwise(q, k, v, i_f, jax.nn.log_sigmoid(f_f))
    rev = lambda t: jnp.flip(t, axis=2)
    h_bwd = rev(mlstm_chunkwise(rev(q), rev(k), rev(v), rev(i_b), jax.nn.log_sigmoid(rev(f_b))))
    return h_fwd + h_bwd


def attn_mlstm_mixer(xn, w_in, b_gate, conv_w, q_gain, k_gain, ml_gain, w_out):
    b, s, _ = xn.shape
    f32 = jnp.float32
    proj = xn @ w_in
    aq, ak, av, mqk, mv, mo, gates = jnp.split(proj, IN_SPLITS, axis=-1)
    heads = lambda t, nh: t.reshape(b, s, nh, -1).transpose(0, 2, 1, 3)

    cos, sin = axial_rope_tables(s)
    q = apply_rope(rmsnorm(heads(aq, ATT_HEADS), q_gain), cos, sin)
    k = apply_rope(rmsnorm(heads(ak, ATT_KV_HEADS), k_gain), cos, sin)
    v = heads(av, ATT_KV_HEADS)
    y_att = gqa_attention(q, k, v)

    mqk = jax.nn.silu(centred_dwconv(mqk, conv_w))
    mq, mk = jnp.split(mqk, 2, axis=-1)
    mq = heads(mq, ML_HEADS).astype(f32)
    mk = heads(mk, ML_HEADS).astype(f32) * (ML_HEAD_DIM ** -0.5)
    mvh = heads(mv, ML_HEADS).astype(f32)
    gates = (gates + b_gate).astype(f32).transpose(0, 2, 1)
    i_f, f_f, i_b, f_b = jnp.split(gates, 4, axis=1)
    hm = mlstm_bidir(mq, mk, mvh, i_f, f_f, i_b, f_b)
    hm = rmsnorm(hm, ml_gain.reshape(ML_HEADS, 1, ML_HEAD_DIM))
    hm = hm.transpose(0, 2, 1, 3).reshape(b, s, ML_W)
    y_ml = (hm * jax.nn.sigmoid(mo.astype(f32))).astype(xn.dtype)

    return jnp.concatenate([y_att, y_ml], axis=-1) @ w_out


def multiscale_pool_mixer(xn, pool_w, layer_scale):
    b, s, d = xn.shape
    xf = xn.astype(jnp.float32)
    cs = jnp.concatenate([jnp.zeros((b, 1, d), jnp.float32), jnp.cumsum(xf, axis=1)], axis=1)
    cs = cs.reshape(b, s + 1, POOL_GROUPS, POOL_DIM)
    t = jnp.arange(s)[:, None]
    half = jnp.array([w // 2 for w in POOL_WINDOWS], jnp.int32)[None, :]
    lo = jnp.clip(t - half, 0, s)
    hi = jnp.clip(t + half, 0, s)
    grp = jnp.arange(POOL_GROUPS)[None, :]
    win_sum = cs[:, hi, grp] - cs[:, lo, grp]
    count = (hi - lo).astype(jnp.float32)[..., None]
    pooled = win_sum / count - xf.reshape(b, s, POOL_GROUPS, POOL_DIM)
    y = jnp.einsum('bsgc,gcd->bsgd', pooled.astype(xn.dtype), pool_w).reshape(b, s, d)
    return y * layer_scale


def peer_ffn(xn, w_query, sub_keys, u, v):
    b, s, d = xn.shape
    K = PEER_TOPK
    q = (xn @ w_query).reshape(b, s, PEER_HEADS, 2, PEER_DKEY // 2)
    sc = jnp.einsum('bshpc,hpnc->bshpn', q, sub_keys).astype(jnp.float32)
    s_top, i_top = lax.top_k(sc, K)
    cand = (s_top[..., 0, :, None] + s_top[..., 1, None, :]).reshape(b, s, PEER_HEADS, K * K)
    cand_idx = (i_top[..., 0, :, None] * PEER_NKEYS + i_top[..., 1, None, :]).reshape(b, s, PEER_HEADS, K * K)
    best, pos = lax.top_k(cand, K)
    expert = jnp.take_along_axis(cand_idx, pos, axis=-1)
    gate = jax.nn.softmax(best, axis=-1).astype(xn.dtype)
    nb = (b * s) // PEER_BLOCK
    xs = (xn.reshape(nb, PEER_BLOCK, d),
          expert.reshape(nb, PEER_BLOCK, PEER_HEADS * K),
          gate.reshape(nb, PEER_BLOCK, PEER_HEADS * K))

    def block(args):
        xt, e, g = args
        act = jax.nn.gelu(jnp.einsum('tkd,td->tk', u[e], xt), approximate=False)
        return jnp.einsum('tk,tkd->td', g * act, v[e])

    return lax.map(block, xs).reshape(b, s, d)


def setup_inputs(seed: int = 0) -> dict:
    key = jax.random.key(seed)
    ks = jax.random.split(key, 20)
    f32 = jnp.float32
    nrm = lambda k, shape, scale: jax.random.normal(k, shape, f32) * scale
    gain = lambda k, shape: 1.0 + 0.02 * jax.random.normal(k, shape, f32)
    f_bias = jnp.linspace(3.0, 6.0, ML_HEADS, dtype=f32)
    zeros_h = jnp.zeros((ML_HEADS,), f32)
    gate_base = jnp.concatenate([zeros_h, f_bias, zeros_h, f_bias])
    return {
        'x': nrm(ks[0], (BATCH, SEQ, D_MODEL), 1.0),
        'norm_mix_g': gain(ks[1], (DEPTH, D_MODEL)),
        'norm_ffn_g': gain(ks[2], (DEPTH, D_MODEL)),
        'w_in': nrm(ks[3], (N_EVEN, D_MODEL, IN_WIDTH), D_MODEL ** -0.5),
        'b_gate': gate_base[None, :] + nrm(ks[4], (N_EVEN, N_GATE_COLS), 0.1),
        'conv_w': nrm(ks[5], (N_EVEN, CONV_W, 2 * ML_W), CONV_W ** -0.5),
        'q_gain': gain(ks[6], (N_EVEN, ATT_HEAD_DIM)),
        'k_gain': gain(ks[7], (N_EVEN, ATT_HEAD_DIM)),
        'ml_gain': gain(ks[8], (N_EVEN, ML_W)),
        'w_out': nrm(ks[9], (N_EVEN, MIX_W, D_MODEL), MIX_W ** -0.5),
        'pool_w': nrm(ks[10], (N_ODD, POOL_GROUPS, POOL_DIM, POOL_DIM), POOL_DIM ** -0.5),
        'layer_scale': gain(ks[11], (N_ODD, D_MODEL)),
        'peer_wq': nrm(ks[12], (DEPTH, D_MODEL, PEER_HEADS * PEER_DKEY), D_MODEL ** -0.5),
        'peer_keys': nrm(ks[13], (DEPTH, PEER_HEADS, 2, PEER_NKEYS, PEER_DKEY // 2), (PEER_DKEY // 2) ** -0.5),
        'peer_u': nrm(ks[14], (DEPTH, PEER_EXPERTS, D_MODEL), D_MODEL ** -0.5),
        'peer_v': nrm(ks[15], (DEPTH, PEER_EXPERTS, D_MODEL), PEER_HEADS ** -0.5),
        'norm_f': gain(ks[16], (D_MODEL,)),
    }


def reference(x, norm_mix_g, norm_ffn_g, w_in, b_gate, conv_w, q_gain, k_gain, ml_gain,
              w_out, pool_w, layer_scale, peer_wq, peer_keys, peer_u, peer_v, norm_f):
    h = x
    for layer in range(DEPTH):
        hn = rmsnorm(h, norm_mix_g[layer])
        if layer % 2 == 0:
            e = layer // 2
            h = h + attn_mlstm_mixer(hn, w_in[e], b_gate[e], conv_w[e], q_gain[e], k_gain[e],
                                     ml_gain[e], w_out[e])
        else:
            o = layer // 2
            h = h + multiscale_pool_mixer(hn, pool_w[o], layer_scale[o])
        h = h + peer_ffn(rmsnorm(h, norm_ffn_g[layer]), peer_wq[layer], peer_keys[layer],
                         peer_u[layer], peer_v[layer])
    return rmsnorm(h, norm_f)
```

```python
import functools

import jax
import jax.numpy as jnp
from jax import lax
from jax.experimental import pallas as pl
from jax.experimental.pallas import tpu as pltpu

F32 = jnp.float32
BF16 = jnp.bfloat16
EPS = 1e-6

GRID_W = 64
ROPE_THETA = 10000.0
ATT_HEADS = 8
ATT_KV_HEADS = 2
ATT_HEAD_DIM = 128
ML_HEADS = 4
ML_HEAD_DIM = 256
ML_CHUNK = 128
CONV_W = 5
POOL_WINDOWS = (2, 4, 8, 16)
PEER_HEADS = 8
PEER_NKEYS = 128
PEER_TOPK = 16
PEER_BLOCK = 128

ATT_Q = ATT_HEADS * ATT_HEAD_DIM
ATT_KV = ATT_KV_HEADS * ATT_HEAD_DIM
ML_W = ML_HEADS * ML_HEAD_DIM
MAIN_W = ATT_Q + 2 * ATT_KV + 4 * ML_W
LANE = 128
SUBLANE = 8
VMEM_LIMIT = 48 * 1024 * 1024


def _params(*sem):
    return pltpu.CompilerParams(dimension_semantics=sem, vmem_limit_bytes=VMEM_LIMIT)


def _rms(x, g):
    return x * lax.rsqrt(jnp.mean(x * x, axis=-1, keepdims=True) + EPS) * g


def _norm_mm_body(x_ref, g_ref, w_ref, wg_ref, o_ref, og_ref, xn_ref):
    @pl.when(pl.program_id(1) == 0)
    def _():
        xn = _rms(x_ref[...], g_ref[...]).astype(BF16)
        xn_ref[...] = xn
        og_ref[...] = jnp.dot(xn, wg_ref[...], preferred_element_type=F32)

    o_ref[...] = jnp.dot(xn_ref[...], w_ref[...], preferred_element_type=F32)


def _norm_matmul(x, g, w, wg, *, tm=512, tn=512):
    s, d = x.shape
    n = w.shape[1]
    return pl.pallas_call(
        _norm_mm_body,
        grid=(s // tm, n // tn),
        in_specs=[
            pl.BlockSpec((tm, d), lambda i, j: (i, 0)),
            pl.BlockSpec((1, d), lambda i, j: (0, 0)),
            pl.BlockSpec((d, tn), lambda i, j: (0, j)),
            pl.BlockSpec((d, LANE), lambda i, j: (0, 0)),
        ],
        out_specs=[
            pl.BlockSpec((tm, tn), lambda i, j: (i, j)),
            pl.BlockSpec((tm, LANE), lambda i, j: (i, 0)),
        ],
        out_shape=[jax.ShapeDtypeStruct((s, n), F32), jax.ShapeDtypeStruct((s, LANE), F32)],
        scratch_shapes=[pltpu.VMEM((tm, d), BF16)],
        compiler_params=_params("parallel", "arbitrary"),
        name="norm_matmul",
    )(x, g.reshape(1, d), w, wg)


def _res_mm_body(a_ref, w_ref, r_ref, o_ref):
    o_ref[...] = r_ref[...] + jnp.dot(a_ref[...].astype(BF16), w_ref[...], preferred_element_type=F32)


def _res_matmul(a, w, res, *, tm=512, tn=512):
    s, k = a.shape
    n = w.shape[1]
    return pl.pallas_call(
        _res_mm_body,
        grid=(s // tm, n // tn),
        in_specs=[
            pl.BlockSpec((tm, k), lambda i, j: (i, 0)),
            pl.BlockSpec((k, tn), lambda i, j: (0, j)),
            pl.BlockSpec((tm, tn), lambda i, j: (i, j)),
        ],
        out_specs=pl.BlockSpec((tm, tn), lambda i, j: (i, j)),
        out_shape=jax.ShapeDtypeStruct((s, n), F32),
        compiler_params=_params("parallel", "arbitrary"),
        name="res_matmul",
    )(a, w, res)


def _rope_tables(s):
    rows = s // GRID_W
    r, c = jnp.meshgrid(jnp.arange(rows), jnp.arange(GRID_W), indexing="ij")
    pos = jnp.stack([r.reshape(-1), c.reshape(-1)], axis=-1).astype(F32)
    nf = ATT_HEAD_DIM // 4
    inv = ROPE_THETA ** (-jnp.arange(nf, dtype=F32) / nf)
    ang = pos[:, :, None] * inv
    cos, sin = jnp.cos(ang), jnp.sin(ang)
    ctab = jnp.stack([cos, cos], axis=2).reshape(s, ATT_HEAD_DIM)
    stab = jnp.stack([-sin, sin], axis=2).reshape(s, ATT_HEAD_DIM)
    return ctab, stab


def _rope(y, ctab, stab, first_half):
    nf = ATT_HEAD_DIM // 4
    swapped = jnp.where(first_half, pltpu.roll(y, ATT_HEAD_DIM - nf, 1), pltpu.roll(y, nf, 1))
    return y * ctab + swapped * stab


def _attn_prep_body(aq_ref, ak_ref, av_ref, c_ref, s_ref, qg_ref, kg_ref, q_ref, k_ref, v_ref):
    d = ATT_HEAD_DIM
    ctab, stab = c_ref[...], s_ref[...]
    lane = lax.broadcasted_iota(jnp.int32, ctab.shape, 1)
    first_half = (lane % (d // 2)) < (d // 4)
    scale = d ** -0.5
    for h in range(ATT_HEADS):
        y = _rms(aq_ref[:, h * d:(h + 1) * d], qg_ref[...])
        q_ref[h] = (_rope(y, ctab, stab, first_half) * scale).astype(BF16)
    for h in range(ATT_KV_HEADS):
        y = _rms(ak_ref[:, h * d:(h + 1) * d], kg_ref[...])
        k_ref[h] = _rope(y, ctab, stab, first_half).astype(BF16)
        v_ref[h] = av_ref[:, h * d:(h + 1) * d].astype(BF16)


def _attn_prep(proj, q_gain, k_gain, *, tm=512):
    s = proj.shape[0]
    d = ATT_HEAD_DIM
    ctab, stab = _rope_tables(s)
    kv_blk = ATT_Q // ATT_KV
    return pl.pallas_call(
        _attn_prep_body,
        grid=(s // tm,),
        in_specs=[
            pl.BlockSpec((tm, ATT_Q), lambda i: (i, 0)),
            pl.BlockSpec((tm, ATT_KV), lambda i: (i, kv_blk)),
            pl.BlockSpec((tm, ATT_KV), lambda i: (i, kv_blk + 1)),
            pl.BlockSpec((tm, d), lambda i: (i, 0)),
            pl.BlockSpec((tm, d), lambda i: (i, 0)),
            pl.BlockSpec((1, d), lambda i: (0, 0)),
            pl.BlockSpec((1, d), lambda i: (0, 0)),
        ],
        out_specs=[
            pl.BlockSpec((ATT_HEADS, tm, d), lambda i: (0, i, 0)),
            pl.BlockSpec((ATT_KV_HEADS, tm, d), lambda i: (0, i, 0)),
            pl.BlockSpec((ATT_KV_HEADS, tm, d), lambda i: (0, i, 0)),
        ],
        out_shape=[
            jax.ShapeDtypeStruct((ATT_HEADS, s, d), BF16),
            jax.ShapeDtypeStruct((ATT_KV_HEADS, s, d), BF16),
            jax.ShapeDtypeStruct((ATT_KV_HEADS, s, d), BF16),
        ],
        compiler_params=_params("parallel"),
        name="attn_prep",
    )(proj, proj, proj, ctab, stab, q_gain.reshape(1, d), k_gain.reshape(1, d))


def _attn_body(q_ref, k_ref, v_ref, o_ref, m_ref, l_ref, acc_ref, *, tq, grp):
    j = pl.program_id(2)
    d = ATT_HEAD_DIM

    @pl.when(j == 0)
    def _():
        m_ref[...] = jnp.full(m_ref.shape, -jnp.inf, F32)
        l_ref[...] = jnp.zeros(l_ref.shape, F32)
        acc_ref[...] = jnp.zeros(acc_ref.shape, F32)

    q = q_ref[...].reshape(grp * tq, d)
    sc = lax.dot_general(q, k_ref[0], (((1,), (1,)), ((), ())), preferred_element_type=F32)
    m_prev = m_ref[...]
    m_new = jnp.maximum(m_prev, jnp.max(sc, axis=-1, keepdims=True))
    alpha = jnp.exp(m_prev - m_new)
    p = jnp.exp(sc - m_new)
    l_ref[...] = alpha * l_ref[...] + jnp.sum(p, axis=-1, keepdims=True)
    acc_ref[...] = alpha * acc_ref[...] + jnp.dot(p.astype(BF16), v_ref[0], preferred_element_type=F32)
    m_ref[...] = m_new

    @pl.when(j == pl.num_programs(2) - 1)
    def _():
        out = acc_ref[...] / l_ref[...]
        for h in range(grp):
            o_ref[:, h * d:(h + 1) * d] = out[h * tq:(h + 1) * tq].astype(o_ref.dtype)


def _attention(q, k, v, *, tq=256, tk=512):
    _, s, d = q.shape
    grp = ATT_HEADS // ATT_KV_HEADS
    return pl.pallas_call(
        functools.partial(_attn_body, tq=tq, grp=grp),
        grid=(ATT_KV_HEADS, s // tq, s // tk),
        in_specs=[
            pl.BlockSpec((grp, tq, d), lambda h, i, j: (h, i, 0)),
            pl.BlockSpec((1, tk, d), lambda h, i, j: (h, j, 0)),
            pl.BlockSpec((1, tk, d), lambda h, i, j: (h, j, 0)),
        ],
        out_specs=pl.BlockSpec((tq, grp * d), lambda h, i, j: (i, h)),
        out_shape=jax.ShapeDtypeStruct((s, ATT_HEADS * d), BF16),
        scratch_shapes=[
            pltpu.VMEM((grp * tq, 1), F32),
            pltpu.VMEM((grp * tq, 1), F32),
            pltpu.VMEM((grp * tq, d), F32),
        ],
        compiler_params=_params("parallel", "parallel", "arbitrary"),
        name="attention",
    )(q, k, v)


def _pool_body(prev_ref, x_ref, next_ref, g_ref, w_ref, ls_ref, o_ref, xc_ref, *, tm, s_total):
    i = pl.program_id(0)
    halo = SUBLANE
    g = g_ref[...]
    xm = x_ref[...]
    xc_ref[halo:halo + tm, :] = _rms(xm, g)
    xc_ref[0:halo, :] = jnp.where(i > 0, _rms(prev_ref[...], g), 0.0)
    xc_ref[halo + tm:2 * halo + tm, :] = jnp.where(i < pl.num_programs(0) - 1, _rms(next_ref[...], g), 0.0)
    row = i * tm + lax.broadcasted_iota(jnp.int32, (tm, 1), 0)
    p = xm.shape[1] // len(POOL_WINDOWS)
    outs = []
    for gi, window in enumerate(POOL_WINDOWS):
        hf = window // 2
        cols = slice(gi * p, (gi + 1) * p)
        acc = xc_ref[halo - hf:halo - hf + tm, cols]
        for o in range(-hf + 1, hf):
            acc = acc + xc_ref[halo + o:halo + o + tm, cols]
        cnt = (jnp.minimum(row + hf, s_total) - jnp.maximum(row - hf, 0)).astype(F32)
        pooled = acc / cnt - xc_ref[halo:halo + tm, cols]
        outs.append(jnp.dot(pooled.astype(BF16), w_ref[gi], preferred_element_type=F32))
    o_ref[...] = xm + jnp.concatenate(outs, axis=-1) * ls_ref[...]


def _pool_mixer(h, g, pool_w, layer_scale, *, tm=256):
    s, d = h.shape
    nb = tm // SUBLANE
    ng, p, _ = pool_w.shape
    return pl.pallas_call(
        functools.partial(_pool_body, tm=tm, s_total=s),
        grid=(s // tm,),
        in_specs=[
            pl.BlockSpec((SUBLANE, d), lambda i: (jnp.maximum(i * nb - 1, 0), 0)),
            pl.BlockSpec((tm, d), lambda i: (i, 0)),
            pl.BlockSpec((SUBLANE, d), lambda i: (jnp.minimum((i + 1) * nb, s // SUBLANE - 1), 0)),
            pl.BlockSpec((1, d), lambda i: (0, 0)),
            pl.BlockSpec((ng, p, p), lambda i: (0, 0, 0)),
            pl.BlockSpec((1, d), lambda i: (0, 0)),
        ],
        out_specs=pl.BlockSpec((tm, d), lambda i: (i, 0)),
        out_shape=jax.ShapeDtypeStruct((s, d), F32),
        scratch_shapes=[pltpu.VMEM((tm + 2 * SUBLANE, d), F32)],
        compiler_params=_params("parallel"),
        name="pool_mixer",
    )(h, h, h, g.reshape(1, d), pool_w.astype(BF16), layer_scale.reshape(1, d))


def _norm_body(x_ref, g_ref, o_ref):
    o_ref[...] = _rms(x_ref[...], g_ref[...]).astype(o_ref.dtype)


def _rmsnorm(x, g, dtype=F32, *, tm=512):
    s, d = x.shape
    return pl.pallas_call(
        _norm_body,
        grid=(s // tm,),
        in_specs=[pl.BlockSpec((tm, d), lambda i: (i, 0)), pl.BlockSpec((1, d), lambda i: (0, 0))],
        out_specs=pl.BlockSpec((tm, d), lambda i: (i, 0)),
        out_shape=jax.ShapeDtypeStruct((s, d), dtype),
        compiler_params=_params("parallel"),
        name="rmsnorm",
    )(x, g.reshape(1, d))


def _dwconv_jnp(x, w):
    s = x.shape[0]
    p = CONV_W // 2
    xp = jnp.pad(x, ((p, p), (0, 0)))
    return sum(xp[j:j + s] * w[j] for j in range(CONV_W))


def _mlstm_dir_jnp(q, k, v, i_pre, logf):
    h, s, d = q.shape
    L = ML_CHUNK
    nc = s // L
    q, k, v = (t.reshape(h, nc, L, d) for t in (q, k, v))
    i_pre = i_pre.reshape(h, nc, L)
    logf = logf.reshape(h, nc, L)
    bcum = jnp.cumsum(logf, axis=-1)
    g = bcum[..., -1]
    a = g[..., None] - bcum + i_pre
    a_max = jnp.max(a, axis=-1)
    w = jnp.exp(a - a_max[..., None])
    c_loc = jnp.einsum("hcl,hcld,hcle->hcde", w, k, v)
    n_loc = jnp.einsum("hcl,hcld->hcd", w, k)

    def step(carry, xs):
        c, n, m = carry
        g_c, am_c, cl, nl = xs
        m_new = jnp.maximum(g_c + m, am_c)
        dec = jnp.exp(g_c + m - m_new)
        add = jnp.exp(am_c - m_new)
        return (dec[..., None, None] * c + add[..., None, None] * cl,
                dec[..., None] * n + add[..., None] * nl, m_new), (c, n, m)

    init = (jnp.zeros((h, d, d), F32), jnp.zeros((h, d), F32), jnp.zeros((h,), F32))
    xs = (jnp.moveaxis(g, 1, 0), jnp.moveaxis(a_max, 1, 0), jnp.moveaxis(c_loc, 1, 0), jnp.moveaxis(n_loc, 1, 0))
    _, (c_prev, n_prev, m_prev) = lax.scan(step, init, xs)
    c_prev = jnp.moveaxis(c_prev, 0, 1)
    n_prev = jnp.moveaxis(n_prev, 0, 1)
    m_prev = jnp.moveaxis(m_prev, 0, 1)
    lower = jnp.tril(jnp.ones((L, L), bool))
    dlog = bcum[..., :, None] - bcum[..., None, :] + i_pre[..., None, :]
    dlog = jnp.where(lower, dlog, -jnp.inf)
    inter_log = bcum + m_prev[..., None]
    m_j = jnp.maximum(inter_log, jnp.max(dlog, axis=-1))
    sqk = jnp.einsum("hcjd,hcrd->hcjr", q, k) * jnp.exp(dlog - m_j[..., None])
    inter_w = jnp.exp(inter_log - m_j)
    num = inter_w[..., None] * jnp.einsum("hcjd,hcde->hcje", q, c_prev) + jnp.einsum("hcjr,hcre->hcje", sqk, v)
    den = inter_w * jnp.einsum("hcjd,hcd->hcj", q, n_prev) + jnp.sum(sqk, axis=-1)
    out = num / jnp.maximum(jnp.abs(den), jnp.exp(-m_j))[..., None]
    return out.reshape(h, s, d)


def _mlstm_jnp(proj, gates, b_gate, conv_w, ml_gain):
    s = proj.shape[0]
    o = ATT_Q + 2 * ATT_KV
    mqk = jax.nn.silu(_dwconv_jnp(proj[:, o:o + 2 * ML_W], conv_w))
    mv = proj[:, o + 2 * ML_W:o + 3 * ML_W]
    mo = proj[:, o + 3 * ML_W:o + 4 * ML_W]
    heads = lambda t: t.reshape(s, ML_HEADS, ML_HEAD_DIM).transpose(1, 0, 2)
    mq = heads(mqk[:, :ML_W])
    mk = heads(mqk[:, ML_W:]) * (ML_HEAD_DIM ** -0.5)
    mvh = heads(mv)
    gt = (gates[:, :4 * ML_HEADS] + b_gate).T
    i_f, f_f, i_b, f_b = jnp.split(gt, 4, axis=0)
    rev = lambda t: jnp.flip(t, axis=1)
    h_fwd = _mlstm_dir_jnp(mq, mk, mvh, i_f, jax.nn.log_sigmoid(f_f))
    h_bwd = rev(_mlstm_dir_jnp(rev(mq), rev(mk), rev(mvh), rev(i_b), jax.nn.log_sigmoid(rev(f_b))))
    hm = h_fwd + h_bwd
    hm = hm * lax.rsqrt(jnp.mean(hm * hm, axis=-1, keepdims=True) + EPS) * ml_gain.reshape(ML_HEADS, 1, ML_HEAD_DIM)
    hm = hm.transpose(1, 0, 2).reshape(s, ML_W)
    return hm * jax.nn.sigmoid(mo)


def _peer_jnp(xn, w_query, sub_keys, u, v):
    s, d = xn.shape
    kk = PEER_TOPK
    q = (xn @ w_query).reshape(s, PEER_HEADS, 2, -1)
    sc = jnp.einsum("shpc,hpnc->shpn", q, sub_keys)
    s_top, i_top = lax.top_k(sc, kk)
    cand = (s_top[..., 0, :, None] + s_top[..., 1, None, :]).reshape(s, PEER_HEADS, kk * kk)
    cand_idx = (i_top[..., 0, :, None] * PEER_NKEYS + i_top[..., 1, None, :]).reshape(s, PEER_HEADS, kk * kk)
    best, pos = lax.top_k(cand, kk)
    expert = jnp.take_along_axis(cand_idx, pos, axis=-1)
    gate = jax.nn.softmax(best, axis=-1)
    nb = s // PEER_BLOCK
    xs = (xn.reshape(nb, PEER_BLOCK, d), expert.reshape(nb, PEER_BLOCK, -1), gate.reshape(nb, PEER_BLOCK, -1))

    def block(args):
        xt, e, g = args
        act = jax.nn.gelu(jnp.einsum("tkd,td->tk", u[e], xt), approximate=False)
        return jnp.einsum("tk,tkd->td", g * act, v[e])

    return lax.map(block, xs).reshape(s, d)


def kernel(x, norm_mix_g, norm_ffn_g, w_in, b_gate, conv_w, q_gain, k_gain, ml_gain, w_out, pool_w,
           layer_scale, peer_wq, peer_keys, peer_u, peer_v, norm_f):
    b, s, d = x.shape
    h = x.reshape(b * s, d)

    w_main = w_in[0, :, :MAIN_W].astype(BF16)
    w_gate = jnp.pad(w_in[0, :, MAIN_W:], ((0, 0), (0, LANE - (w_in.shape[2] - MAIN_W)))).astype(BF16)
    proj, gates = _norm_matmul(h, norm_mix_g[0], w_main, w_gate)
    q, k, v = _attn_prep(proj, q_gain[0], k_gain[0])
    y_att = _attention(q, k, v)
    y_ml = _mlstm_jnp(proj, gates, b_gate[0], conv_w[0], ml_gain[0])
    y = jnp.concatenate([y_att, y_ml.astype(BF16)], axis=-1)
    h = _res_matmul(y, w_out[0].astype(BF16), h)
    h = h + _peer_jnp(_rmsnorm(h, norm_ffn_g[0]), peer_wq[0], peer_keys[0], peer_u[0], peer_v[0])

    h = _pool_mixer(h, norm_mix_g[1], pool_w[0], layer_scale[0])
    h = h + _peer_jnp(_rmsnorm(h, norm_ffn_g[1]), peer_wq[1], peer_keys[1], peer_u[1], peer_v[1])

    return _rmsnorm(h, norm_f).reshape(b, s, d)
```

```python
import functools

import jax
import jax.numpy as jnp
from jax import lax
from jax.experimental import pallas as pl
from jax.experimental.pallas import tpu as pltpu

F32 = jnp.float32
BF16 = jnp.bfloat16
EPS = 1e-6

GRID_W = 64
ROPE_THETA = 10000.0
ATT_HEADS = 8
ATT_KV_HEADS = 2
ATT_HEAD_DIM = 128
ML_HEADS = 4
ML_HEAD_DIM = 256
ML_CHUNK = 128
CONV_W = 5
POOL_WINDOWS = (2, 4, 8, 16)
PEER_HEADS = 8
PEER_NKEYS = 128
PEER_TOPK = 16
PEER_BLOCK = 128

ATT_Q = ATT_HEADS * ATT_HEAD_DIM
ATT_KV = ATT_KV_HEADS * ATT_HEAD_DIM
ML_W = ML_HEADS * ML_HEAD_DIM
MAIN_W = ATT_Q + 2 * ATT_KV + 4 * ML_W
LANE = 128
SUBLANE = 8
VMEM_LIMIT = 48 * 1024 * 1024


def _params(*sem):
    return pltpu.CompilerParams(dimension_semantics=sem, vmem_limit_bytes=VMEM_LIMIT)


def _rms(x, g):
    return x * lax.rsqrt(jnp.mean(x * x, axis=-1, keepdims=True) + EPS) * g


def _norm_mm_body(x_ref, g_ref, w_ref, wg_ref, o_ref, og_ref, xn_ref):
    @pl.when(pl.program_id(1) == 0)
    def _():
        xn = _rms(x_ref[...], g_ref[...]).astype(BF16)
        xn_ref[...] = xn
        og_ref[...] = jnp.dot(xn, wg_ref[...], preferred_element_type=F32)

    o_ref[...] = jnp.dot(xn_ref[...], w_ref[...], preferred_element_type=F32)


def _norm_matmul(x, g, w, wg, *, tm=512, tn=512):
    s, d = x.shape
    n = w.shape[1]
    return pl.pallas_call(
        _norm_mm_body,
        grid=(s // tm, n // tn),
        in_specs=[
            pl.BlockSpec((tm, d), lambda i, j: (i, 0)),
            pl.BlockSpec((1, d), lambda i, j: (0, 0)),
            pl.BlockSpec((d, tn), lambda i, j: (0, j)),
            pl.BlockSpec((d, LANE), lambda i, j: (0, 0)),
        ],
        out_specs=[
            pl.BlockSpec((tm, tn), lambda i, j: (i, j)),
            pl.BlockSpec((tm, LANE), lambda i, j: (i, 0)),
        ],
        out_shape=[jax.ShapeDtypeStruct((s, n), F32), jax.ShapeDtypeStruct((s, LANE), F32)],
        scratch_shapes=[pltpu.VMEM((tm, d), BF16)],
        compiler_params=_params("parallel", "arbitrary"),
        name="norm_matmul",
    )(x, g.reshape(1, d), w, wg)


def _res_mm_body(a_ref, w_ref, r_ref, o_ref):
    o_ref[...] = r_ref[...] + jnp.dot(a_ref[...].astype(BF16), w_ref[...], preferred_element_type=F32)


def _res_matmul(a, w, res, *, tm=512, tn=512):
    s, k = a.shape
    n = w.shape[1]
    return pl.pallas_call(
        _res_mm_body,
        grid=(s // tm, n // tn),
        in_specs=[
            pl.BlockSpec((tm, k), lambda i, j: (i, 0)),
            pl.BlockSpec((k, tn), lambda i, j: (0, j)),
            pl.BlockSpec((tm, tn), lambda i, j: (i, j)),
        ],
        out_specs=pl.BlockSpec((tm, tn), lambda i, j: (i, j)),
        out_shape=jax.ShapeDtypeStruct((s, n), F32),
        compiler_params=_params("parallel", "arbitrary"),
        name="res_matmul",
    )(a, w, res)


def _rope_tables(s):
    rows = s // GRID_W
    r, c = jnp.meshgrid(jnp.arange(rows), jnp.arange(GRID_W), indexing="ij")
    pos = jnp.stack([r.reshape(-1), c.reshape(-1)], axis=-1).astype(F32)
    nf = ATT_HEAD_DIM // 4
    inv = ROPE_THETA ** (-jnp.arange(nf, dtype=F32) / nf)
    ang = pos[:, :, None] * inv
    cos, sin = jnp.cos(ang), jnp.sin(ang)
    ctab = jnp.stack([cos, cos], axis=2).reshape(s, ATT_HEAD_DIM)
    stab = jnp.stack([-sin, sin], axis=2).reshape(s, ATT_HEAD_DIM)
    return ctab, stab


def _rope(y, ctab, stab, first_half):
    nf = ATT_HEAD_DIM // 4
    swapped = jnp.where(first_half, pltpu.roll(y, ATT_HEAD_DIM - nf, 1), pltpu.roll(y, nf, 1))
    return y * ctab + swapped * stab


def _attn_prep_body(aq_ref, ak_ref, av_ref, c_ref, s_ref, qg_ref, kg_ref, q_ref, k_ref, v_ref):
    d = ATT_HEAD_DIM
    ctab, stab = c_ref[...], s_ref[...]
    lane = lax.broadcasted_iota(jnp.int32, ctab.shape, 1)
    first_half = (lane % (d // 2)) < (d // 4)
    scale = d ** -0.5
    for h in range(ATT_HEADS):
        y = _rms(aq_ref[:, h * d:(h + 1) * d], qg_ref[...])
        q_ref[h] = (_rope(y, ctab, stab, first_half) * scale).astype(BF16)
    for h in range(ATT_KV_HEADS):
        y = _rms(ak_ref[:, h * d:(h + 1) * d], kg_ref[...])
        k_ref[h] = _rope(y, ctab, stab, first_half).astype(BF16)
        v_ref[h] = av_ref[:, h * d:(h + 1) * d].astype(BF16)


def _attn_prep(proj, q_gain, k_gain, *, tm=512):
    s = proj.shape[0]
    d = ATT_HEAD_DIM
    ctab, stab = _rope_tables(s)
    kv_blk = ATT_Q // ATT_KV
    return pl.pallas_call(
        _attn_prep_body,
        grid=(s // tm,),
        in_specs=[
            pl.BlockSpec((tm, ATT_Q), lambda i: (i, 0)),
            pl.BlockSpec((tm, ATT_KV), lambda i: (i, kv_blk)),
            pl.BlockSpec((tm, ATT_KV), lambda i: (i, kv_blk + 1)),
            pl.BlockSpec((tm, d), lambda i: (i, 0)),
            pl.BlockSpec((tm, d), lambda i: (i, 0)),
            pl.BlockSpec((1, d), lambda i: (0, 0)),
            pl.BlockSpec((1, d), lambda i: (0, 0)),
        ],
        out_specs=[
            pl.BlockSpec((ATT_HEADS, tm, d), lambda i: (0, i, 0)),
            pl.BlockSpec((ATT_KV_HEADS, tm, d), lambda i: (0, i, 0)),
            pl.BlockSpec((ATT_KV_HEADS, tm, d), lambda i: (0, i, 0)),
        ],
        out_shape=[
            jax.ShapeDtypeStruct((ATT_HEADS, s, d), BF16),
            jax.ShapeDtypeStruct((ATT_KV_HEADS, s, d), BF16),
            jax.ShapeDtypeStruct((ATT_KV_HEADS, s, d), BF16),
        ],
        compiler_params=_params("parallel"),
        name="attn_prep",
    )(proj, proj, proj, ctab, stab, q_gain.reshape(1, d), k_gain.reshape(1, d))


def _attn_body(q_ref, k_ref, v_ref, o_ref, m_ref, l_ref, acc_ref, *, tq, grp):
    j = pl.program_id(2)
    d = ATT_HEAD_DIM

    @pl.when(j == 0)
    def _():
        m_ref[...] = jnp.full(m_ref.shape, -jnp.inf, F32)
        l_ref[...] = jnp.zeros(l_ref.shape, F32)
        acc_ref[...] = jnp.zeros(acc_ref.shape, F32)

    q = q_ref[...].reshape(grp * tq, d)
    sc = lax.dot_general(q, k_ref[0], (((1,), (1,)), ((), ())), preferred_element_type=F32)
    m_prev = m_ref[...]
    m_new = jnp.maximum(m_prev, jnp.max(sc, axis=-1, keepdims=True))
    alpha = jnp.exp(m_prev - m_new)
    p = jnp.exp(sc - m_new)
    l_ref[...] = alpha * l_ref[...] + jnp.sum(p, axis=-1, keepdims=True)
    acc_ref[...] = alpha * acc_ref[...] + jnp.dot(p.astype(BF16), v_ref[0], preferred_element_type=F32)
    m_ref[...] = m_new

    @pl.when(j == pl.num_programs(2) - 1)
    def _():
        out = acc_ref[...] / l_ref[...]
        for h in range(grp):
            o_ref[:, h * d:(h + 1) * d] = out[h * tq:(h + 1) * tq].astype(o_ref.dtype)


def _attention(q, k, v, *, tq=256, tk=512):
    _, s, d = q.shape
    grp = ATT_HEADS // ATT_KV_HEADS
    return pl.pallas_call(
        functools.partial(_attn_body, tq=tq, grp=grp),
        grid=(ATT_KV_HEADS, s // tq, s // tk),
        in_specs=[
            pl.BlockSpec((grp, tq, d), lambda h, i, j: (h, i, 0)),
            pl.BlockSpec((1, tk, d), lambda h, i, j: (h, j, 0)),
            pl.BlockSpec((1, tk, d), lambda h, i, j: (h, j, 0)),
        ],
        out_specs=pl.BlockSpec((tq, grp * d), lambda h, i, j: (i, h)),
        out_shape=jax.ShapeDtypeStruct((s, ATT_HEADS * d), BF16),
        scratch_shapes=[
            pltpu.VMEM((grp * tq, 1), F32),
            pltpu.VMEM((grp * tq, 1), F32),
            pltpu.VMEM((grp * tq, d), F32),
        ],
        compiler_params=_params("parallel", "parallel", "arbitrary"),
        name="attention",
    )(q, k, v)


def _pool_body(prev_ref, x_ref, next_ref, g_ref, w_ref, ls_ref, o_ref, xc_ref, *, tm, s_total):
    i = pl.program_id(0)
    halo = SUBLANE
    g = g_ref[...]
    xm = x_ref[...]
    xc_ref[halo:halo + tm, :] = _rms(xm, g)
    xc_ref[0:halo, :] = jnp.where(i > 0, _rms(prev_ref[...], g), 0.0)
    xc_ref[halo + tm:2 * halo + tm, :] = jnp.where(i < pl.num_programs(0) - 1, _rms(next_ref[...], g), 0.0)
    row = i * tm + lax.broadcasted_iota(jnp.int32, (tm, 1), 0)
    p = xm.shape[1] // len(POOL_WINDOWS)
    outs = []
    for gi, window in enumerate(POOL_WINDOWS):
        hf = window // 2
        cols = slice(gi * p, (gi + 1) * p)
        acc = xc_ref[halo - hf:halo - hf + tm, cols]
        for o in range(-hf + 1, hf):
            acc = acc + xc_ref[halo + o:halo + o + tm, cols]
        cnt = (jnp.minimum(row + hf, s_total) - jnp.maximum(row - hf, 0)).astype(F32)
        pooled = acc / cnt - xc_ref[halo:halo + tm, cols]
        outs.append(jnp.dot(pooled.astype(BF16), w_ref[gi], preferred_element_type=F32))
    o_ref[...] = xm + jnp.concatenate(outs, axis=-1) * ls_ref[...]


def _pool_mixer(h, g, pool_w, layer_scale, *, tm=256):
    s, d = h.shape
    nb = tm // SUBLANE
    ng, p, _ = pool_w.shape
    return pl.pallas_call(
        functools.partial(_pool_body, tm=tm, s_total=s),
        grid=(s // tm,),
        in_specs=[
            pl.BlockSpec((SUBLANE, d), lambda i: (jnp.maximum(i * nb - 1, 0), 0)),
            pl.BlockSpec((tm, d), lambda i: (i, 0)),
            pl.BlockSpec((SUBLANE, d), lambda i: (jnp.minimum((i + 1) * nb, s // SUBLANE - 1), 0)),
            pl.BlockSpec((1, d), lambda i: (0, 0)),
            pl.BlockSpec((ng, p, p), lambda i: (0, 0, 0)),
            pl.BlockSpec((1, d), lambda i: (0, 0)),
        ],
        out_specs=pl.BlockSpec((tm, d), lambda i: (i, 0)),
        out_shape=jax.ShapeDtypeStruct((s, d), F32),
        scratch_shapes=[pltpu.VMEM((tm + 2 * SUBLANE, d), F32)],
        compiler_params=_params("parallel"),
        name="pool_mixer",
    )(h, h, h, g.reshape(1, d), pool_w.astype(BF16), layer_scale.reshape(1, d))


def _norm_body(x_ref, g_ref, o_ref):
    o_ref[...] = _rms(x_ref[...], g_ref[...]).astype(o_ref.dtype)


def _rmsnorm(x, g, dtype=F32, *, tm=512):
    s, d = x.shape
    return pl.pallas_call(
        _norm_body,
        grid=(s // tm,),
        in_specs=[pl.BlockSpec((tm, d), lambda i: (i, 0)), pl.BlockSpec((1, d), lambda i: (0, 0))],
        out_specs=pl.BlockSpec((tm, d), lambda i: (i, 0)),
        out_shape=jax.ShapeDtypeStruct((s, d), dtype),
        compiler_params=_params("parallel"),
        name="rmsnorm",
    )(x, g.reshape(1, d))


NEG_INF = float("-inf")


def _top_values(x, count):
    rows = []
    for _ in range(count):
        m = jnp.max(x, axis=0, keepdims=True)
        rows.append(m)
        x = jnp.where(x == m, NEG_INF, x)
    return rows


def _stack_rows(rows, t, n=None):
    n = len(rows) if n is None else n
    rid = lax.broadcasted_iota(jnp.int32, (n, t), 0)
    out = jnp.full((n, t), NEG_INF, F32)
    for r, row in enumerate(rows):
        out = jnp.where(rid == r, row, out)
    return out


def _route_head(s1, s2):
    t = s1.shape[1]
    kk = PEER_TOPK
    v1 = _top_values(s1, kk + 1)
    v2 = _top_values(s2, kk + 1)
    v2_all = _stack_rows(v2[:kk], t)
    pieces = []
    for a in range(kk // 2):
        nb = kk // (a + 1)
        rows = kk if nb > SUBLANE else SUBLANE
        rid = lax.broadcasted_iota(jnp.int32, (rows, t), 0)
        pieces.append(jnp.where(rid < nb, v1[a] + v2_all[:rows], NEG_INF))
    tail = [v1[a] + v2[0] for a in range(kk // 2, kk + 1)] + [v1[0] + v2[kk]]
    pieces.append(_stack_rows(tail, t, kk))
    cand = jnp.concatenate(pieces, axis=0)
    best = _top_values(cand, kk + 1)
    tau = 0.5 * (best[kk - 1] + best[kk])
    z = jnp.ones_like(tau)
    for r in range(1, kk):
        z = z + jnp.exp(best[r] - best[0])
    thr = tau - s1
    e1 = jnp.exp(s1 - v1[0])
    e2 = jnp.exp(s2 - v2[0]) / z
    return thr, e1, e2


def _peer_route_body(h_ref, g_ref, wq_ref, keys_ref, xnt_ref, thr_ref, s2_ref, e1_ref, e2_ref):
    nk = PEER_NKEYS
    xnt = _rms(h_ref[...], g_ref[...]).T.astype(BF16)
    xnt_ref[...] = xnt
    qt = jnp.dot(wq_ref[...], xnt, preferred_element_type=F32)
    for hd in range(PEER_HEADS):
        q1 = qt[(2 * hd) * nk:(2 * hd + 1) * nk].astype(BF16)
        q2 = qt[(2 * hd + 1) * nk:(2 * hd + 2) * nk].astype(BF16)
        s1 = jnp.dot(keys_ref[2 * hd], q1, preferred_element_type=F32)
        s2 = jnp.dot(keys_ref[2 * hd + 1], q2, preferred_element_type=F32)
        thr, e1, e2 = _route_head(s1, s2)
        thr_ref[hd] = thr
        s2_ref[hd] = s2
        e1_ref[hd] = e1
        e2_ref[hd] = e2


def _peer_route(h, g, wq_t, keys, *, tt=256):
    s, d = h.shape
    nq = wq_t.shape[0]
    hk = (PEER_HEADS, PEER_NKEYS, s)
    gate_spec = pl.BlockSpec((PEER_HEADS, PEER_NKEYS, tt), lambda i: (0, 0, i))
    return pl.pallas_call(
        _peer_route_body,
        grid=(s // tt,),
        in_specs=[
            pl.BlockSpec((tt, d), lambda i: (i, 0)),
            pl.BlockSpec((1, d), lambda i: (0, 0)),
            pl.BlockSpec((nq, d), lambda i: (0, 0)),
            pl.BlockSpec(keys.shape, lambda i: (0, 0, 0)),
        ],
        out_specs=[pl.BlockSpec((d, tt), lambda i: (0, i)), gate_spec, gate_spec, gate_spec, gate_spec],
        out_shape=[jax.ShapeDtypeStruct((d, s), BF16)] + [jax.ShapeDtypeStruct(hk, F32)] * 4,
        compiler_params=_params("parallel"),
        name="peer_route",
    )(h, g.reshape(1, d), wq_t, keys)


def _gelu(x):
    return 0.5 * x * (1.0 + lax.erf(x * (2.0 ** -0.5)))


def _peer_dense_body(xnt_ref, u_ref, vt_ref, thr_ref, e1_ref, s2_ref, e2_ref, o_ref, a_ref, p_ref, *, tt, eb):
    e = pl.program_id(1)
    nk = PEER_NKEYS

    @pl.when(e == 0)
    def _():
        o_ref[...] = jnp.zeros(o_ref.shape, F32)

    a_ref[...] = jnp.dot(u_ref[...], xnt_ref[...], preferred_element_type=F32)

    def lane_chunk(c, carry):
        lanes = pl.ds(pl.multiple_of(c * LANE, LANE), LANE)
        for k in range(eb // nk):
            gate = jnp.zeros((nk, LANE), F32)
            for hd in range(PEER_HEADS):
                thr = thr_ref[hd, 0, k:k + 1, lanes]
                e1 = e1_ref[hd, 0, k:k + 1, lanes]
                gate = gate + jnp.where(s2_ref[hd, :, lanes] >= thr, e2_ref[hd, :, lanes] * e1, 0.0)
            act = _gelu(a_ref[k * nk:(k + 1) * nk, lanes])
            p_ref[k * nk:(k + 1) * nk, lanes] = (gate * act).astype(BF16)
        return carry

    lax.fori_loop(0, tt // LANE, lane_chunk, 0)
    o_ref[...] += jnp.dot(vt_ref[...], p_ref[...], preferred_element_type=F32)


def _peer_dense(xnt, u, vt, thr, e1, s2, e2, *, tt=512, eb=1024):
    d, s = xnt.shape
    ne = u.shape[0]
    nk = PEER_NKEYS
    rows = eb // nk
    split = lambda a: a.reshape(PEER_HEADS, ne // eb, rows, s)
    row_spec = pl.BlockSpec((PEER_HEADS, 1, rows, tt), lambda t, e: (0, e, 0, t))
    full_spec = pl.BlockSpec((PEER_HEADS, nk, tt), lambda t, e: (0, 0, t))
    return pl.pallas_call(
        functools.partial(_peer_dense_body, tt=tt, eb=eb),
        grid=(s // tt, ne // eb),
        in_specs=[
            pl.BlockSpec((d, tt), lambda t, e: (0, t)),
            pl.BlockSpec((eb, d), lambda t, e: (e, 0)),
            pl.BlockSpec((d, eb), lambda t, e: (0, e)),
            row_spec, row_spec, full_spec, full_spec,
        ],
        out_specs=pl.BlockSpec((d, tt), lambda t, e: (0, t)),
        out_shape=jax.ShapeDtypeStruct((d, s), F32),
        scratch_shapes=[pltpu.VMEM((eb, tt), F32), pltpu.VMEM((eb, tt), BF16)],
        compiler_params=_params("parallel", "arbitrary"),
        name="peer_dense",
    )(xnt, u, vt, split(thr), split(e1), s2, e2)


def _add_t_body(h_ref, pt_ref, g_ref, o_ref, *, final_norm):
    y = h_ref[...] + pt_ref[...].T
    o_ref[...] = _rms(y, g_ref[...]) if final_norm else y


def _add_transposed(h, pt, g=None, *, tm=512):
    s, d = h.shape
    gain = jnp.ones((1, d), F32) if g is None else g.reshape(1, d)
    return pl.pallas_call(
        functools.partial(_add_t_body, final_norm=g is not None),
        grid=(s // tm,),
        in_specs=[
            pl.BlockSpec((tm, d), lambda i: (i, 0)),
            pl.BlockSpec((d, tm), lambda i: (0, i)),
            pl.BlockSpec((1, d), lambda i: (0, 0)),
        ],
        out_specs=pl.BlockSpec((tm, d), lambda i: (i, 0)),
        out_shape=jax.ShapeDtypeStruct((s, d), F32),
        compiler_params=_params("parallel"),
        name="add_transposed",
    )(h, pt, gain)


def _peer_ffn(h, g, w_query, sub_keys, u, v, final_g=None):
    nk = PEER_NKEYS
    keys = sub_keys.reshape(2 * PEER_HEADS, nk, -1).astype(BF16)
    xnt, thr, s2, e1, e2 = _peer_route(h, g, w_query.T.astype(BF16), keys)
    pt = _peer_dense(xnt, u.astype(BF16), v.T.astype(BF16), thr, e1, s2, e2)
    return _add_transposed(h, pt, final_g)


def _dwconv_jnp(x, w):
    s = x.shape[0]
    p = CONV_W // 2
    xp = jnp.pad(x, ((p, p), (0, 0)))
    return sum(xp[j:j + s] * w[j] for j in range(CONV_W))


def _mlstm_dir_jnp(q, k, v, i_pre, logf):
    h, s, d = q.shape
    L = ML_CHUNK
    nc = s // L
    q, k, v = (t.reshape(h, nc, L, d) for t in (q, k, v))
    i_pre = i_pre.reshape(h, nc, L)
    logf = logf.reshape(h, nc, L)
    bcum = jnp.cumsum(logf, axis=-1)
    g = bcum[..., -1]
    a = g[..., None] - bcum + i_pre
    a_max = jnp.max(a, axis=-1)
    w = jnp.exp(a - a_max[..., None])
    c_loc = jnp.einsum("hcl,hcld,hcle->hcde", w, k, v)
    n_loc = jnp.einsum("hcl,hcld->hcd", w, k)

    def step(carry, xs):
        c, n, m = carry
        g_c, am_c, cl, nl = xs
        m_new = jnp.maximum(g_c + m, am_c)
        dec = jnp.exp(g_c + m - m_new)
        add = jnp.exp(am_c - m_new)
        return (dec[..., None, None] * c + add[..., None, None] * cl,
                dec[..., None] * n + add[..., None] * nl, m_new), (c, n, m)

    init = (jnp.zeros((h, d, d), F32), jnp.zeros((h, d), F32), jnp.zeros((h,), F32))
    xs = (jnp.moveaxis(g, 1, 0), jnp.moveaxis(a_max, 1, 0), jnp.moveaxis(c_loc, 1, 0), jnp.moveaxis(n_loc, 1, 0))
    _, (c_prev, n_prev, m_prev) = lax.scan(step, init, xs)
    c_prev = jnp.moveaxis(c_prev, 0, 1)
    n_prev = jnp.moveaxis(n_prev, 0, 1)
    m_prev = jnp.moveaxis(m_prev, 0, 1)
    lower = jnp.tril(jnp.ones((L, L), bool))
    dlog = bcum[..., :, None] - bcum[..., None, :] + i_pre[..., None, :]
    dlog = jnp.where(lower, dlog, -jnp.inf)
    inter_log = bcum + m_prev[..., None]
    m_j = jnp.maximum(inter_log, jnp.max(dlog, axis=-1))
    sqk = jnp.einsum("hcjd,hcrd->hcjr", q, k) * jnp.exp(dlog - m_j[..., None])
    inter_w = jnp.exp(inter_log - m_j)
    num = inter_w[..., None] * jnp.einsum("hcjd,hcde->hcje", q, c_prev) + jnp.einsum("hcjr,hcre->hcje", sqk, v)
    den = inter_w * jnp.einsum("hcjd,hcd->hcj", q, n_prev) + jnp.sum(sqk, axis=-1)
    out = num / jnp.maximum(jnp.abs(den), jnp.exp(-m_j))[..., None]
    return out.reshape(h, s, d)


def _mlstm_jnp(proj, gates, b_gate, conv_w, ml_gain):
    s = proj.shape[0]
    o = ATT_Q + 2 * ATT_KV
    mqk = jax.nn.silu(_dwconv_jnp(proj[:, o:o + 2 * ML_W], conv_w))
    mv = proj[:, o + 2 * ML_W:o + 3 * ML_W]
    mo = proj[:, o + 3 * ML_W:o + 4 * ML_W]
    heads = lambda t: t.reshape(s, ML_HEADS, ML_HEAD_DIM).transpose(1, 0, 2)
    mq = heads(mqk[:, :ML_W])
    mk = heads(mqk[:, ML_W:]) * (ML_HEAD_DIM ** -0.5)
    mvh = heads(mv)
    gt = (gates[:, :4 * ML_HEADS] + b_gate).T
    i_f, f_f, i_b, f_b = jnp.split(gt, 4, axis=0)
    rev = lambda t: jnp.flip(t, axis=1)
    h_fwd = _mlstm_dir_jnp(mq, mk, mvh, i_f, jax.nn.log_sigmoid(f_f))
    h_bwd = rev(_mlstm_dir_jnp(rev(mq), rev(mk), rev(mvh), rev(i_b), jax.nn.log_sigmoid(rev(f_b))))
    hm = h_fwd + h_bwd
    hm = hm * lax.rsqrt(jnp.mean(hm * hm, axis=-1, keepdims=True) + EPS) * ml_gain.reshape(ML_HEADS, 1, ML_HEAD_DIM)
    hm = hm.transpose(1, 0, 2).reshape(s, ML_W)
    return hm * jax.nn.sigmoid(mo)


def _peer_jnp(xn, w_query, sub_keys, u, v):
    s, d = xn.shape
    kk = PEER_TOPK
    q = (xn @ w_query).reshape(s, PEER_HEADS, 2, -1)
    sc = jnp.einsum("shpc,hpnc->shpn", q, sub_keys)
    s_top, i_top = lax.top_k(sc, kk)
    cand = (s_top[..., 0, :, None] + s_top[..., 1, None, :]).reshape(s, PEER_HEADS, kk * kk)
    cand_idx = (i_top[..., 0, :, None] * PEER_NKEYS + i_top[..., 1, None, :]).reshape(s, PEER_HEADS, kk * kk)
    best, pos = lax.top_k(cand, kk)
    expert = jnp.take_along_axis(cand_idx, pos, axis=-1)
    gate = jax.nn.softmax(best, axis=-1)
    nb = s // PEER_BLOCK
    xs = (xn.reshape(nb, PEER_BLOCK, d), expert.reshape(nb, PEER_BLOCK, -1), gate.reshape(nb, PEER_BLOCK, -1))

    def block(args):
        xt, e, g = args
        act = jax.nn.gelu(jnp.einsum("tkd,td->tk", u[e], xt), approximate=False)
        return jnp.einsum("tk,tkd->td", g * act, v[e])

    return lax.map(block, xs).reshape(s, d)


def kernel(x, norm_mix_g, norm_ffn_g, w_in, b_gate, conv_w, q_gain, k_gain, ml_gain, w_out, pool_w,
           layer_scale, peer_wq, peer_keys, peer_u, peer_v, norm_f):
    b, s, d = x.shape
    h = x.reshape(b * s, d)

    w_main = w_in[0, :, :MAIN_W].astype(BF16)
    w_gate = jnp.pad(w_in[0, :, MAIN_W:], ((0, 0), (0, LANE - (w_in.shape[2] - MAIN_W)))).astype(BF16)
    proj, gates = _norm_matmul(h, norm_mix_g[0], w_main, w_gate)
    q, k, v = _attn_prep(proj, q_gain[0], k_gain[0])
    y_att = _attention(q, k, v)
    y_ml = _mlstm_jnp(proj, gates, b_gate[0], conv_w[0], ml_gain[0])
    y = jnp.concatenate([y_att, y_ml.astype(BF16)], axis=-1)
    h = _res_matmul(y, w_out[0].astype(BF16), h)
    h = _peer_ffn(h, norm_ffn_g[0], peer_wq[0], peer_keys[0], peer_u[0], peer_v[0])

    h = _pool_mixer(h, norm_mix_g[1], pool_w[0], layer_scale[0])
    h = _peer_ffn(h, norm_ffn_g[1], peer_wq[1], peer_keys[1], peer_u[1], peer_v[1], final_g=norm_f)
    return h.reshape(b, s, d)
```

```python
import functools

import jax
import jax.numpy as jnp
from jax import lax
from jax.experimental import pallas as pl
from jax.experimental.pallas import tpu as pltpu

F32 = jnp.float32
BF16 = jnp.bfloat16
EPS = 1e-6
LOG2E = 1.4426950408889634

GRID_W = 64
ROPE_THETA = 10000.0
ATT_HEADS = 8
ATT_KV_HEADS = 2
ATT_HEAD_DIM = 128
ML_HEADS = 4
ML_HEAD_DIM = 256
ML_CHUNK = 128
CONV_W = 5
POOL_WINDOWS = (2, 4, 8, 16)
PEER_HEADS = 8
PEER_NKEYS = 128
PEER_TOPK = 16
PEER_BLOCK = 128

ATT_Q = ATT_HEADS * ATT_HEAD_DIM
ATT_KV = ATT_KV_HEADS * ATT_HEAD_DIM
ML_W = ML_HEADS * ML_HEAD_DIM
MAIN_W = ATT_Q + 2 * ATT_KV + 4 * ML_W
assert ATT_Q == ML_W
COL_AQ, COL_MQ, COL_MK, COL_MV, COL_MO = 0, 1, 2, 3, 4
COL_AK = (ATT_Q + 4 * ML_W) // ATT_KV
COL_AV = COL_AK + 1
LANE = 128
SUBLANE = 8
VMEM_LIMIT = 48 * 1024 * 1024


def _params(*sem):
    return pltpu.CompilerParams(dimension_semantics=sem, vmem_limit_bytes=VMEM_LIMIT)


def _rms(x, g):
    return x * lax.rsqrt(jnp.mean(x * x, axis=-1, keepdims=True) + EPS) * g


def _norm_mm_body(x_ref, g_ref, w_ref, wg_ref, o_ref, og_ref, xn_ref):
    @pl.when(pl.program_id(1) == 0)
    def _():
        xn = _rms(x_ref[...], g_ref[...]).astype(BF16)
        xn_ref[...] = xn
        og_ref[...] = jnp.dot(xn, wg_ref[...], preferred_element_type=F32)

    o_ref[...] = jnp.dot(xn_ref[...], w_ref[...], preferred_element_type=F32)


def _norm_matmul(x, g, w, wg, *, tm=512, tn=512):
    s, d = x.shape
    n = w.shape[1]
    return pl.pallas_call(
        _norm_mm_body,
        grid=(s // tm, n // tn),
        in_specs=[
            pl.BlockSpec((tm, d), lambda i, j: (i, 0)),
            pl.BlockSpec((1, d), lambda i, j: (0, 0)),
            pl.BlockSpec((d, tn), lambda i, j: (0, j)),
            pl.BlockSpec((d, LANE), lambda i, j: (0, 0)),
        ],
        out_specs=[
            pl.BlockSpec((tm, tn), lambda i, j: (i, j)),
            pl.BlockSpec((tm, LANE), lambda i, j: (i, 0)),
        ],
        out_shape=[jax.ShapeDtypeStruct((s, n), F32), jax.ShapeDtypeStruct((s, LANE), F32)],
        scratch_shapes=[pltpu.VMEM((tm, d), BF16)],
        compiler_params=_params("parallel", "arbitrary"),
        name="norm_matmul",
    )(x, g.reshape(1, d), w, wg)


def _rope_tables(s):
    rows = s // GRID_W
    r, c = jnp.meshgrid(jnp.arange(rows), jnp.arange(GRID_W), indexing="ij")
    pos = jnp.stack([r.reshape(-1), c.reshape(-1)], axis=-1).astype(F32)
    nf = ATT_HEAD_DIM // 4
    inv = ROPE_THETA ** (-jnp.arange(nf, dtype=F32) / nf)
    ang = pos[:, :, None] * inv
    cos, sin = jnp.cos(ang), jnp.sin(ang)
    ctab = jnp.stack([cos, cos], axis=2).reshape(s, ATT_HEAD_DIM)
    stab = jnp.stack([-sin, sin], axis=2).reshape(s, ATT_HEAD_DIM)
    return ctab, stab


def _rope(y, ctab, stab, first_half):
    nf = ATT_HEAD_DIM // 4
    swapped = jnp.where(first_half, pltpu.roll(y, ATT_HEAD_DIM - nf, 1), pltpu.roll(y, nf, 1))
    return y * ctab + swapped * stab


def _attn_prep_body(aq_ref, ak_ref, av_ref, c_ref, s_ref, qg_ref, kg_ref, qt_ref, k_ref, vt_ref):
    d = ATT_HEAD_DIM
    ctab, stab = c_ref[...], s_ref[...]
    lane = lax.broadcasted_iota(jnp.int32, ctab.shape, 1)
    first_half = (lane % (d // 2)) < (d // 4)
    scale = d ** -0.5 * LOG2E
    for h in range(ATT_HEADS):
        y = _rms(aq_ref[:, h * d:(h + 1) * d], qg_ref[...])
        qt_ref[h] = (_rope(y, ctab, stab, first_half) * scale).T.astype(BF16)
    for h in range(ATT_KV_HEADS):
        y = _rms(ak_ref[:, h * d:(h + 1) * d], kg_ref[...])
        k_ref[h] = _rope(y, ctab, stab, first_half).astype(BF16)
        vt_ref[h] = av_ref[:, h * d:(h + 1) * d].T.astype(BF16)


def _attn_prep(proj, q_gain, k_gain, *, tm=512):
    s = proj.shape[0]
    d = ATT_HEAD_DIM
    ctab, stab = _rope_tables(s)
    return pl.pallas_call(
        _attn_prep_body,
        grid=(s // tm,),
        in_specs=[
            pl.BlockSpec((tm, ATT_Q), lambda i: (i, COL_AQ)),
            pl.BlockSpec((tm, ATT_KV), lambda i: (i, COL_AK)),
            pl.BlockSpec((tm, ATT_KV), lambda i: (i, COL_AV)),
            pl.BlockSpec((tm, d), lambda i: (i, 0)),
            pl.BlockSpec((tm, d), lambda i: (i, 0)),
            pl.BlockSpec((1, d), lambda i: (0, 0)),
            pl.BlockSpec((1, d), lambda i: (0, 0)),
        ],
        out_specs=[
            pl.BlockSpec((ATT_HEADS, d, tm), lambda i: (0, 0, i)),
            pl.BlockSpec((ATT_KV_HEADS, tm, d), lambda i: (0, i, 0)),
            pl.BlockSpec((ATT_KV_HEADS, d, tm), lambda i: (0, 0, i)),
        ],
        out_shape=[
            jax.ShapeDtypeStruct((ATT_HEADS, d, s), BF16),
            jax.ShapeDtypeStruct((ATT_KV_HEADS, s, d), BF16),
            jax.ShapeDtypeStruct((ATT_KV_HEADS, d, s), BF16),
        ],
        compiler_params=_params("parallel"),
        name="attn_prep",
    )(proj, proj, proj, ctab, stab, q_gain.reshape(1, d), k_gain.reshape(1, d))


def _attn_body(qt_ref, k_ref, vt_ref, o_ref, q_all, m_ref, l_ref, acc_ref, *, tq, grp):
    j = pl.program_id(2)
    d = ATT_HEAD_DIM

    @pl.when(j == 0)
    def _():
        for h in range(grp):
            q_all[:, h * tq:(h + 1) * tq] = qt_ref[h]
        m_ref[...] = jnp.full(m_ref.shape, -jnp.inf, F32)
        l_ref[...] = jnp.zeros(l_ref.shape, F32)
        acc_ref[...] = jnp.zeros(acc_ref.shape, F32)

    st = jnp.dot(k_ref[0], q_all[...], preferred_element_type=F32)
    m_prev = m_ref[...]
    m_new = jnp.maximum(m_prev, jnp.max(st, axis=0, keepdims=True))
    alpha = jnp.exp2(m_prev - m_new)
    pt = jnp.exp2(st - m_new)
    l_ref[...] = alpha * l_ref[...] + jnp.sum(pt, axis=0, keepdims=True)
    acc_ref[...] = alpha * acc_ref[...] + jnp.dot(vt_ref[0], pt.astype(BF16), preferred_element_type=F32)
    m_ref[...] = m_new

    @pl.when(j == pl.num_programs(2) - 1)
    def _():
        out_t = acc_ref[...] / l_ref[...]
        for h in range(grp):
            o_ref[:, h * d:(h + 1) * d] = out_t[:, h * tq:(h + 1) * tq].T.astype(o_ref.dtype)


def _attention(qt, k, vt, *, tq=512, tk=512):
    _, s, d = k.shape
    grp = ATT_HEADS // ATT_KV_HEADS
    return pl.pallas_call(
        functools.partial(_attn_body, tq=tq, grp=grp),
        grid=(ATT_KV_HEADS, s // tq, s // tk),
        in_specs=[
            pl.BlockSpec((grp, d, tq), lambda h, i, j: (h, 0, i)),
            pl.BlockSpec((1, tk, d), lambda h, i, j: (h, j, 0)),
            pl.BlockSpec((1, d, tk), lambda h, i, j: (h, 0, j)),
        ],
        out_specs=pl.BlockSpec((tq, grp * d), lambda h, i, j: (i, h)),
        out_shape=jax.ShapeDtypeStruct((s, ATT_HEADS * d), BF16),
        scratch_shapes=[
            pltpu.VMEM((d, grp * tq), BF16),
            pltpu.VMEM((1, grp * tq), F32),
            pltpu.VMEM((1, grp * tq), F32),
            pltpu.VMEM((d, grp * tq), F32),
        ],
        compiler_params=_params("parallel", "parallel", "arbitrary"),
        name="attention",
    )(qt, k, vt)


def _pool_body(prev_ref, x_ref, next_ref, g_ref, w_ref, ls_ref, o_ref, xc_ref, *, tm, s_total):
    i = pl.program_id(0)
    halo = SUBLANE
    g = g_ref[...]
    xm = x_ref[...]
    xc_ref[halo:halo + tm, :] = _rms(xm, g)
    xc_ref[0:halo, :] = jnp.where(i > 0, _rms(prev_ref[...], g), 0.0)
    xc_ref[halo + tm:2 * halo + tm, :] = jnp.where(i < pl.num_programs(0) - 1, _rms(next_ref[...], g), 0.0)
    row = i * tm + lax.broadcasted_iota(jnp.int32, (tm, 1), 0)
    p = xm.shape[1] // len(POOL_WINDOWS)
    outs = []
    for gi, window in enumerate(POOL_WINDOWS):
        hf = window // 2
        cols = slice(gi * p, (gi + 1) * p)
        acc = xc_ref[halo - hf:halo - hf + tm, cols]
        for o in range(-hf + 1, hf):
            acc = acc + xc_ref[halo + o:halo + o + tm, cols]
        cnt = (jnp.minimum(row + hf, s_total) - jnp.maximum(row - hf, 0)).astype(F32)
        pooled = acc / cnt - xc_ref[halo:halo + tm, cols]
        outs.append(jnp.dot(pooled.astype(BF16), w_ref[gi], preferred_element_type=F32))
    o_ref[...] = xm + jnp.concatenate(outs, axis=-1) * ls_ref[...]


def _pool_mixer(h, g, pool_w, layer_scale, *, tm=256):
    s, d = h.shape
    nb = tm // SUBLANE
    ng, p, _ = pool_w.shape
    return pl.pallas_call(
        functools.partial(_pool_body, tm=tm, s_total=s),
        grid=(s // tm,),
        in_specs=[
            pl.BlockSpec((SUBLANE, d), lambda i: (jnp.maximum(i * nb - 1, 0), 0)),
            pl.BlockSpec((tm, d), lambda i: (i, 0)),
            pl.BlockSpec((SUBLANE, d), lambda i: (jnp.minimum((i + 1) * nb, s // SUBLANE - 1), 0)),
            pl.BlockSpec((1, d), lambda i: (0, 0)),
            pl.BlockSpec((ng, p, p), lambda i: (0, 0, 0)),
            pl.BlockSpec((1, d), lambda i: (0, 0)),
        ],
        out_specs=pl.BlockSpec((tm, d), lambda i: (i, 0)),
        out_shape=jax.ShapeDtypeStruct((s, d), F32),
        scratch_shapes=[pltpu.VMEM((tm + 2 * SUBLANE, d), F32)],
        compiler_params=_params("parallel"),
        name="pool_mixer",
    )(h, h, h, g.reshape(1, d), pool_w.astype(BF16), layer_scale.reshape(1, d))


NEG_INF = float("-inf")


def _top_values(x, count):
    rows = []
    for _ in range(count):
        m = jnp.max(x, axis=0, keepdims=True)
        rows.append(m)
        x = jnp.where(x == m, NEG_INF, x)
    return rows


def _stack_rows(rows, t, n=None):
    n = len(rows) if n is None else n
    rid = lax.broadcasted_iota(jnp.int32, (n, t), 0)
    out = jnp.full((n, t), NEG_INF, F32)
    for r, row in enumerate(rows):
        out = jnp.where(rid == r, row, out)
    return out


def _route_head(s1, s2):
    t = s1.shape[1]
    kk = PEER_TOPK
    v1 = _top_values(s1, kk + 1)
    v2 = _top_values(s2, kk + 1)
    v2_all = _stack_rows(v2[:kk], t)
    pieces = []
    for a in range(kk // 2):
        nb = kk // (a + 1)
        rows = kk if nb > SUBLANE else SUBLANE
        rid = lax.broadcasted_iota(jnp.int32, (rows, t), 0)
        pieces.append(jnp.where(rid < nb, v1[a] + v2_all[:rows], NEG_INF))
    tail = [v1[a] + v2[0] for a in range(kk // 2, kk + 1)] + [v1[0] + v2[kk]]
    pieces.append(_stack_rows(tail, t, kk))
    cand = jnp.concatenate(pieces, axis=0)
    best = _top_values(cand, kk + 1)
    tau = 0.5 * (best[kk - 1] + best[kk])
    z = jnp.ones_like(tau)
    for r in range(1, kk):
        z = z + jnp.exp(best[r] - best[0])
    thr = tau - s1
    e1 = jnp.exp(s1 - v1[0])
    e2 = jnp.exp(s2 - v2[0]) / z
    return thr, e1, e2


def _peer_route_body(h_ref, g_ref, wq_ref, keys_ref, xnt_ref, thr_ref, s2_ref, e1_ref, e2_ref):
    nk = PEER_NKEYS
    xnt = _rms(h_ref[...], g_ref[...]).T.astype(BF16)
    xnt_ref[...] = xnt
    qt = jnp.dot(wq_ref[...], xnt, preferred_element_type=F32)
    for hd in range(PEER_HEADS):
        q1 = qt[(2 * hd) * nk:(2 * hd + 1) * nk].astype(BF16)
        q2 = qt[(2 * hd + 1) * nk:(2 * hd + 2) * nk].astype(BF16)
        s1 = jnp.dot(keys_ref[2 * hd], q1, preferred_element_type=F32)
        s2 = jnp.dot(keys_ref[2 * hd + 1], q2, preferred_element_type=F32)
        thr, e1, e2 = _route_head(s1, s2)
        thr_ref[hd] = thr
        s2_ref[hd] = s2
        e1_ref[hd] = e1
        e2_ref[hd] = e2


def _peer_route(h, g, wq_t, keys, *, tt=256):
    s, d = h.shape
    nq = wq_t.shape[0]
    hk = (PEER_HEADS, PEER_NKEYS, s)
    gate_spec = pl.BlockSpec((PEER_HEADS, PEER_NKEYS, tt), lambda i: (0, 0, i))
    return pl.pallas_call(
        _peer_route_body,
        grid=(s // tt,),
        in_specs=[
            pl.BlockSpec((tt, d), lambda i: (i, 0)),
            pl.BlockSpec((1, d), lambda i: (0, 0)),
            pl.BlockSpec((nq, d), lambda i: (0, 0)),
            pl.BlockSpec(keys.shape, lambda i: (0, 0, 0)),
        ],
        out_specs=[pl.BlockSpec((d, tt), lambda i: (0, i)), gate_spec, gate_spec, gate_spec, gate_spec],
        out_shape=[jax.ShapeDtypeStruct((d, s), BF16)] + [jax.ShapeDtypeStruct(hk, F32)] * 4,
        compiler_params=_params("parallel"),
        name="peer_route",
    )(h, g.reshape(1, d), wq_t, keys)


def _gelu(x):
    return 0.5 * x * (1.0 + lax.erf(x * (2.0 ** -0.5)))


def _peer_dense_body(xnt_ref, u_ref, vt_ref, thr_ref, e1_ref, s2_ref, e2_ref, o_ref, a_ref, p_ref, *, tt, eb):
    e = pl.program_id(1)
    nk = PEER_NKEYS

    @pl.when(e == 0)
    def _():
        o_ref[...] = jnp.zeros(o_ref.shape, F32)

    a_ref[...] = jnp.dot(u_ref[...], xnt_ref[...], preferred_element_type=F32)

    def lane_chunk(c, carry):
        lanes = pl.ds(pl.multiple_of(c * LANE, LANE), LANE)
        for k in range(eb // nk):
            gate = jnp.zeros((nk, LANE), F32)
            for hd in range(PEER_HEADS):
                thr = thr_ref[hd, 0, k:k + 1, lanes]
                e1 = e1_ref[hd, 0, k:k + 1, lanes]
                gate = gate + jnp.where(s2_ref[hd, :, lanes] >= thr, e2_ref[hd, :, lanes] * e1, 0.0)
            act = _gelu(a_ref[k * nk:(k + 1) * nk, lanes])
            p_ref[k * nk:(k + 1) * nk, lanes] = (gate * act).astype(BF16)
        return carry

    lax.fori_loop(0, tt // LANE, lane_chunk, 0)
    o_ref[...] += jnp.dot(vt_ref[...], p_ref[...], preferred_element_type=F32)


def _peer_dense(xnt, u, vt, thr, e1, s2, e2, *, tt=512, eb=1024):
    d, s = xnt.shape
    ne = u.shape[0]
    nk = PEER_NKEYS
    rows = eb // nk
    split = lambda a: a.reshape(PEER_HEADS, ne // eb, rows, s)
    row_spec = pl.BlockSpec((PEER_HEADS, 1, rows, tt), lambda t, e: (0, e, 0, t))
    full_spec = pl.BlockSpec((PEER_HEADS, nk, tt), lambda t, e: (0, 0, t))
    return pl.pallas_call(
        functools.partial(_peer_dense_body, tt=tt, eb=eb),
        grid=(s // tt, ne // eb),
        in_specs=[
            pl.BlockSpec((d, tt), lambda t, e: (0, t)),
            pl.BlockSpec((eb, d), lambda t, e: (e, 0)),
            pl.BlockSpec((d, eb), lambda t, e: (0, e)),
            row_spec, row_spec, full_spec, full_spec,
        ],
        out_specs=pl.BlockSpec((d, tt), lambda t, e: (0, t)),
        out_shape=jax.ShapeDtypeStruct((d, s), F32),
        scratch_shapes=[pltpu.VMEM((eb, tt), F32), pltpu.VMEM((eb, tt), BF16)],
        compiler_params=_params("parallel", "arbitrary"),
        name="peer_dense",
    )(xnt, u, vt, split(thr), split(e1), s2, e2)


def _add_t_body(h_ref, pt_ref, g_ref, o_ref, *, final_norm):
    y = h_ref[...] + pt_ref[...].T
    o_ref[...] = _rms(y, g_ref[...]) if final_norm else y


def _add_transposed(h, pt, g=None, *, tm=512):
    s, d = h.shape
    gain = jnp.ones((1, d), F32) if g is None else g.reshape(1, d)
    return pl.pallas_call(
        functools.partial(_add_t_body, final_norm=g is not None),
        grid=(s // tm,),
        in_specs=[
            pl.BlockSpec((tm, d), lambda i: (i, 0)),
            pl.BlockSpec((d, tm), lambda i: (0, i)),
            pl.BlockSpec((1, d), lambda i: (0, 0)),
        ],
        out_specs=pl.BlockSpec((tm, d), lambda i: (i, 0)),
        out_shape=jax.ShapeDtypeStruct((s, d), F32),
        compiler_params=_params("parallel"),
        name="add_transposed",
    )(h, pt, gain)


def _peer_ffn(h, g, w_query, sub_keys, u, v, final_g=None):
    nk = PEER_NKEYS
    keys = sub_keys.reshape(2 * PEER_HEADS, nk, -1).astype(BF16)
    xnt, thr, s2, e1, e2 = _peer_route(h, g, w_query.T.astype(BF16), keys)
    pt = _peer_dense(xnt, u.astype(BF16), v.T.astype(BF16), thr, e1, s2, e2)
    return _add_transposed(h, pt, final_g)


def _mlstm_prep_body(qp_ref, q_ref, qn_ref, kp_ref, k_ref, kn_ref, v_ref, cwq_ref, cwk_ref,
                     qo_ref, kto_ref, vo_ref, xc_ref, *, tm):
    i = pl.program_id(0)
    last = pl.num_programs(0) - 1
    halo = SUBLANE
    pad = CONV_W // 2

    def conv_silu(prev_ref, x_ref, next_ref, w_ref):
        xc_ref[halo:halo + tm, :] = x_ref[...]
        xc_ref[0:halo, :] = jnp.where(i > 0, prev_ref[...], 0.0)
        xc_ref[halo + tm:2 * halo + tm, :] = jnp.where(i < last, next_ref[...], 0.0)
        acc = xc_ref[halo - pad:halo - pad + tm, :] * w_ref[0:1, :]
        for j in range(1, CONV_W):
            acc = acc + xc_ref[halo - pad + j:halo - pad + j + tm, :] * w_ref[j:j + 1, :]
        return acc * jax.nn.sigmoid(acc)

    qo_ref[...] = conv_silu(qp_ref, q_ref, qn_ref, cwq_ref).astype(BF16)
    kc = conv_silu(kp_ref, k_ref, kn_ref, cwk_ref) * (ML_HEAD_DIM ** -0.5)
    kto_ref[...] = kc.T.astype(BF16)
    vo_ref[...] = v_ref[...].astype(BF16)


def _mlstm_prep(proj, conv_w, *, tm=256):
    s = proj.shape[0]
    w = ML_W
    nb = tm // SUBLANE
    prev = lambda c: pl.BlockSpec((SUBLANE, w), lambda i: (jnp.maximum(i * nb - 1, 0), c))
    main = lambda c: pl.BlockSpec((tm, w), lambda i: (i, c))
    nxt = lambda c: pl.BlockSpec((SUBLANE, w), lambda i: (jnp.minimum((i + 1) * nb, s // SUBLANE - 1), c))
    cw = pl.BlockSpec((CONV_W, w), lambda i: (0, 0))
    return pl.pallas_call(
        functools.partial(_mlstm_prep_body, tm=tm),
        grid=(s // tm,),
        in_specs=[prev(COL_MQ), main(COL_MQ), nxt(COL_MQ), prev(COL_MK), main(COL_MK), nxt(COL_MK),
                  main(COL_MV), cw, cw],
        out_specs=[
            pl.BlockSpec((tm, w), lambda i: (i, 0)),
            pl.BlockSpec((w, tm), lambda i: (0, i)),
            pl.BlockSpec((tm, w), lambda i: (i, 0)),
        ],
        out_shape=[
            jax.ShapeDtypeStruct((s, w), BF16),
            jax.ShapeDtypeStruct((w, s), BF16),
            jax.ShapeDtypeStruct((s, w), BF16),
        ],
        scratch_shapes=[pltpu.VMEM((tm + 2 * SUBLANE, w), F32)],
        compiler_params=_params("parallel"),
        name="mlstm_prep",
    )(proj, proj, proj, proj, proj, proj, proj, conv_w[:, :w], conv_w[:, w:])


def _log_sigmoid(x):
    return jnp.minimum(x, 0.0) - jnp.log1p(jnp.exp(-jnp.abs(x)))


def _mlstm_direction(q_ref, kt_ref, v_ref, g_ref, bias_ref, o_ref, s_ref, m_ref, *, backward):
    L = ML_CHUNK
    d = ML_HEAD_DIM
    row = lax.broadcasted_iota(jnp.int32, (L, L), 0)
    col = lax.broadcasted_iota(jnp.int32, (L, L), 1)
    keep = (col >= row) if backward else (col <= row)
    tri = keep.astype(BF16)
    ones = jnp.ones((L, LANE), BF16)

    gates = g_ref[...] + bias_ref[...]
    logf = _log_sigmoid(gates)
    hi = logf.astype(BF16)
    lo = (logf - hi.astype(F32)).astype(BF16)
    bcol = jnp.dot(tri, hi, preferred_element_type=F32) + jnp.dot(tri, lo, preferred_element_type=F32)
    gates_t = gates.T
    b_t = bcol.T
    base = 2 * ML_HEADS if backward else 0
    for h in range(ML_HEADS):
        idx = base // 2 + h
        ci, cf = base + h, base + ML_HEADS + h
        i_row = gates_t[ci:ci + 1, :]
        b_row = b_t[cf:cf + 1, :]
        b_col = bcol[:, cf:cf + 1]
        g_tot = b_row[:, 0:1] if backward else b_row[:, L - 1:L]
        a_row = g_tot - b_row + i_row
        a_max = jnp.max(a_row, axis=1, keepdims=True)
        w_row = jnp.exp(a_row - a_max)

        q = q_ref[:, h * d:(h + 1) * d]
        kt = kt_ref[h * d:(h + 1) * d, :]
        v_aug = jnp.concatenate([v_ref[:, h * d:(h + 1) * d], ones], axis=1)
        s_prev = s_ref[idx]
        m_prev = m_ref[idx][:, 0:1]

        dlog = jnp.where(keep, b_col + (i_row - b_row), NEG_INF)
        inter_log = b_col + m_prev
        m_j = jnp.maximum(inter_log, jnp.max(dlog, axis=1, keepdims=True))
        sqk = jnp.dot(q, kt, preferred_element_type=F32) * jnp.exp(dlog - m_j)
        inter_w = jnp.exp(inter_log - m_j)
        tot = (inter_w * jnp.dot(q, s_prev.astype(BF16), preferred_element_type=F32)
               + jnp.dot(sqk.astype(BF16), v_aug, preferred_element_type=F32))
        den = tot[:, d:]
        inv = 1.0 / jnp.maximum(jnp.abs(den), jnp.exp(-m_j))
        o_ref[:, h * d:(h + 1) * d] = tot[:, :d] * jnp.concatenate([inv] * (d // LANE), axis=1)

        ktw = (kt.astype(F32) * w_row).astype(BF16)
        upd = jnp.dot(ktw, v_aug, preferred_element_type=F32)
        m_new = jnp.maximum(g_tot + m_prev, a_max)
        s_ref[idx] = jnp.exp(g_tot + m_prev - m_new) * s_prev + jnp.exp(a_max - m_new) * upd
        m_ref[idx] = jnp.broadcast_to(m_new, (1, LANE))


def _mlstm_body(qf_ref, ktf_ref, vf_ref, gf_ref, qb_ref, ktb_ref, vb_ref, gb_ref, bias_ref,
                hf_ref, hb_ref, s_ref, m_ref):
    @pl.when(pl.program_id(0) == 0)
    def _():
        s_ref[...] = jnp.zeros(s_ref.shape, F32)
        m_ref[...] = jnp.zeros(m_ref.shape, F32)

    _mlstm_direction(qf_ref, ktf_ref, vf_ref, gf_ref, bias_ref, hf_ref, s_ref, m_ref, backward=False)
    _mlstm_direction(qb_ref, ktb_ref, vb_ref, gb_ref, bias_ref, hb_ref, s_ref, m_ref, backward=True)


def _mlstm_scan(mq, mkt, mv, gates, b_gate):
    s, w = mq.shape
    L = ML_CHUNK
    nc = s // L
    d = ML_HEAD_DIM
    bias = jnp.pad(b_gate, (0, LANE - b_gate.shape[0])).reshape(1, LANE)
    fwd = lambda c: c
    bwd = lambda c: nc - 1 - c
    rows = lambda ix, n: pl.BlockSpec((L, n), lambda c: (ix(c), 0))
    cols = lambda ix: pl.BlockSpec((w, L), lambda c: (0, ix(c)))
    return pl.pallas_call(
        _mlstm_body,
        grid=(nc,),
        in_specs=[rows(fwd, w), cols(fwd), rows(fwd, w), rows(fwd, LANE),
                  rows(bwd, w), cols(bwd), rows(bwd, w), rows(bwd, LANE),
                  pl.BlockSpec((1, LANE), lambda c: (0, 0))],
        out_specs=[rows(fwd, w), rows(bwd, w)],
        out_shape=[jax.ShapeDtypeStruct((s, w), F32)] * 2,
        scratch_shapes=[pltpu.VMEM((2 * ML_HEADS, d, d + LANE), F32), pltpu.VMEM((2 * ML_HEADS, 1, LANE), F32)],
        compiler_params=_params("arbitrary"),
        name="mlstm_scan",
    )(mq, mkt, mv, gates, mq, mkt, mv, gates, bias)


def _mix_out_body(ya_ref, hf_ref, hb_ref, mo_ref, mg_ref, wa_ref, wm_ref, r_ref, o_ref, yml_ref):
    @pl.when(pl.program_id(1) == 0)
    def _():
        d = ML_HEAD_DIM
        for h in range(ML_HEADS):
            sl = slice(h * d, (h + 1) * d)
            hm = _rms(hf_ref[:, sl] + hb_ref[:, sl], mg_ref[:, sl])
            yml_ref[:, sl] = (hm * jax.nn.sigmoid(mo_ref[:, sl])).astype(BF16)

    o_ref[...] = (r_ref[...] + jnp.dot(ya_ref[...], wa_ref[...], preferred_element_type=F32)
                  + jnp.dot(yml_ref[...], wm_ref[...], preferred_element_type=F32))


def _mix_out(y_att, hf, hb, proj, ml_gain, w_att, w_ml, res, *, tm=512, tn=512):
    s, n = res.shape
    wa, wm = w_att.shape[0], w_ml.shape[0]
    return pl.pallas_call(
        _mix_out_body,
        grid=(s // tm, n // tn),
        in_specs=[
            pl.BlockSpec((tm, wa), lambda i, j: (i, 0)),
            pl.BlockSpec((tm, wm), lambda i, j: (i, 0)),
            pl.BlockSpec((tm, wm), lambda i, j: (i, 0)),
            pl.BlockSpec((tm, wm), lambda i, j: (i, COL_MO)),
            pl.BlockSpec((1, wm), lambda i, j: (0, 0)),
            pl.BlockSpec((wa, tn), lambda i, j: (0, j)),
            pl.BlockSpec((wm, tn), lambda i, j: (0, j)),
            pl.BlockSpec((tm, tn), lambda i, j: (i, j)),
        ],
        out_specs=pl.BlockSpec((tm, tn), lambda i, j: (i, j)),
        out_shape=jax.ShapeDtypeStruct((s, n), F32),
        scratch_shapes=[pltpu.VMEM((tm, wm), BF16)],
        compiler_params=_params("parallel", "arbitrary"),
        name="mix_out",
    )(y_att, hf, hb, proj, ml_gain.reshape(1, wm), w_att, w_ml, res)


def kernel(x, norm_mix_g, norm_ffn_g, w_in, b_gate, conv_w, q_gain, k_gain, ml_gain, w_out, pool_w,
           layer_scale, peer_wq, peer_keys, peer_u, peer_v, norm_f):
    b, s, d = x.shape
    h = x.reshape(b * s, d)

    wi = w_in[0]
    kv_end = ATT_Q + 2 * ATT_KV
    w_main = jnp.concatenate([wi[:, :ATT_Q], wi[:, kv_end:MAIN_W], wi[:, ATT_Q:kv_end]], axis=1).astype(BF16)
    w_gate = jnp.pad(wi[:, MAIN_W:], ((0, 0), (0, LANE - (wi.shape[1] - MAIN_W)))).astype(BF16)
    proj, gates = _norm_matmul(h, norm_mix_g[0], w_main, w_gate)
    qt, k, vt = _attn_prep(proj, q_gain[0], k_gain[0])
    y_att = _attention(qt, k, vt)
    mq, mkt, mv = _mlstm_prep(proj, conv_w[0])
    hf, hb = _mlstm_scan(mq, mkt, mv, gates, b_gate[0])
    wo = w_out[0].astype(BF16)
    h = _mix_out(y_att, hf, hb, proj, ml_gain[0], wo[:ATT_Q], wo[ATT_Q:], h)
    h = _peer_ffn(h, norm_ffn_g[0], peer_wq[0], peer_keys[0], peer_u[0], peer_v[0])

    h = _pool_mixer(h, norm_mix_g[1], pool_w[0], layer_scale[0])
    h = _peer_ffn(h, norm_ffn_g[1], peer_wq[1], peer_keys[1], peer_u[1], peer_v[1], final_g=norm_f)
    return h.reshape(b, s, d)
```

```python
import functools

import jax
import jax.numpy as jnp
from jax import lax
from jax.experimental import pallas as pl
from jax.experimental.pallas import tpu as pltpu

F32 = jnp.float32
BF16 = jnp.bfloat16
EPS = 1e-6
LOG2E = 1.4426950408889634

GRID_W = 64
ROPE_THETA = 10000.0
ATT_HEADS = 8
ATT_KV_HEADS = 2
ATT_HEAD_DIM = 128
ML_HEADS = 4
ML_HEAD_DIM = 256
ML_CHUNK = 128
CONV_W = 5
POOL_WINDOWS = (2, 4, 8, 16)
PEER_HEADS = 8
PEER_NKEYS = 128
PEER_TOPK = 16
PEER_BLOCK = 128

ATT_Q = ATT_HEADS * ATT_HEAD_DIM
ATT_KV = ATT_KV_HEADS * ATT_HEAD_DIM
ML_W = ML_HEADS * ML_HEAD_DIM
MAIN_W = ATT_Q + 2 * ATT_KV + 4 * ML_W
assert ATT_Q == ML_W
COL_AQ, COL_MQ, COL_MK, COL_MV, COL_MO = 0, 1, 2, 3, 4
COL_AK = (ATT_Q + 4 * ML_W) // ATT_KV
COL_AV = COL_AK + 1
LANE = 128
SUBLANE = 8
VMEM_LIMIT = 48 * 1024 * 1024


def _params(*sem):
    return pltpu.CompilerParams(dimension_semantics=sem, vmem_limit_bytes=VMEM_LIMIT)


def _rms(x, g):
    return x * lax.rsqrt(jnp.mean(x * x, axis=-1, keepdims=True) + EPS) * g


def _norm_mm_body(x_ref, g_ref, w_ref, wg_ref, o_ref, og_ref, xn_ref):
    @pl.when(pl.program_id(1) == 0)
    def _():
        xn = _rms(x_ref[...], g_ref[...]).astype(BF16)
        xn_ref[...] = xn
        og_ref[...] = jnp.dot(xn, wg_ref[...], preferred_element_type=F32)

    o_ref[...] = jnp.dot(xn_ref[...], w_ref[...], preferred_element_type=F32)


def _norm_matmul(x, g, w, wg, *, tm=512, tn=512):
    s, d = x.shape
    n = w.shape[1]
    return pl.pallas_call(
        _norm_mm_body,
        grid=(s // tm, n // tn),
        in_specs=[
            pl.BlockSpec((tm, d), lambda i, j: (i, 0)),
            pl.BlockSpec((1, d), lambda i, j: (0, 0)),
            pl.BlockSpec((d, tn), lambda i, j: (0, j)),
            pl.BlockSpec((d, LANE), lambda i, j: (0, 0)),
        ],
        out_specs=[
            pl.BlockSpec((tm, tn), lambda i, j: (i, j)),
            pl.BlockSpec((tm, LANE), lambda i, j: (i, 0)),
        ],
        out_shape=[jax.ShapeDtypeStruct((s, n), F32), jax.ShapeDtypeStruct((s, LANE), F32)],
        scratch_shapes=[pltpu.VMEM((tm, d), BF16)],
        compiler_params=_params("parallel", "arbitrary"),
        name="norm_matmul",
    )(x, g.reshape(1, d), w, wg)


def _rope_tables(s):
    rows = s // GRID_W
    r, c = jnp.meshgrid(jnp.arange(rows), jnp.arange(GRID_W), indexing="ij")
    pos = jnp.stack([r.reshape(-1), c.reshape(-1)], axis=-1).astype(F32)
    nf = ATT_HEAD_DIM // 4
    inv = ROPE_THETA ** (-jnp.arange(nf, dtype=F32) / nf)
    ang = pos[:, :, None] * inv
    cos, sin = jnp.cos(ang), jnp.sin(ang)
    ctab = jnp.stack([cos, cos], axis=2).reshape(s, ATT_HEAD_DIM)
    stab = jnp.stack([-sin, sin], axis=2).reshape(s, ATT_HEAD_DIM)
    return ctab, stab


def _rope(y, ctab, stab, first_half):
    nf = ATT_HEAD_DIM // 4
    swapped = jnp.where(first_half, pltpu.roll(y, ATT_HEAD_DIM - nf, 1), pltpu.roll(y, nf, 1))
    return y * ctab + swapped * stab


def _attn_prep_body(aq_ref, ak_ref, av_ref, c_ref, s_ref, qg_ref, kg_ref, qt_ref, k_ref, vt_ref):
    d = ATT_HEAD_DIM
    ctab, stab = c_ref[...], s_ref[...]
    lane = lax.broadcasted_iota(jnp.int32, ctab.shape, 1)
    first_half = (lane % (d // 2)) < (d // 4)
    scale = d ** -0.5 * LOG2E
    for h in range(ATT_HEADS):
        y = _rms(aq_ref[:, h * d:(h + 1) * d], qg_ref[...])
        qt_ref[h] = (_rope(y, ctab, stab, first_half) * scale).T.astype(BF16)
    for h in range(ATT_KV_HEADS):
        y = _rms(ak_ref[:, h * d:(h + 1) * d], kg_ref[...])
        k_ref[h] = _rope(y, ctab, stab, first_half).astype(BF16)
        vt_ref[h] = av_ref[:, h * d:(h + 1) * d].T.astype(BF16)


def _attn_prep(proj, q_gain, k_gain, *, tm=512):
    s = proj.shape[0]
    d = ATT_HEAD_DIM
    ctab, stab = _rope_tables(s)
    return pl.pallas_call(
        _attn_prep_body,
        grid=(s // tm,),
        in_specs=[
            pl.BlockSpec((tm, ATT_Q), lambda i: (i, COL_AQ)),
            pl.BlockSpec((tm, ATT_KV), lambda i: (i, COL_AK)),
            pl.BlockSpec((tm, ATT_KV), lambda i: (i, COL_AV)),
            pl.BlockSpec((tm, d), lambda i: (i, 0)),
            pl.BlockSpec((tm, d), lambda i: (i, 0)),
            pl.BlockSpec((1, d), lambda i: (0, 0)),
            pl.BlockSpec((1, d), lambda i: (0, 0)),
        ],
        out_specs=[
            pl.BlockSpec((ATT_HEADS, d, tm), lambda i: (0, 0, i)),
            pl.BlockSpec((ATT_KV_HEADS, tm, d), lambda i: (0, i, 0)),
            pl.BlockSpec((ATT_KV_HEADS, d, tm), lambda i: (0, 0, i)),
        ],
        out_shape=[
            jax.ShapeDtypeStruct((ATT_HEADS, d, s), BF16),
            jax.ShapeDtypeStruct((ATT_KV_HEADS, s, d), BF16),
            jax.ShapeDtypeStruct((ATT_KV_HEADS, d, s), BF16),
        ],
        compiler_params=_params("parallel"),
        name="attn_prep",
    )(proj, proj, proj, ctab, stab, q_gain.reshape(1, d), k_gain.reshape(1, d))


def _col_reduce(x, pair, final):
    while x.shape[0] > SUBLANE:
        half = x.shape[0] // 2
        x = pair(x[:half], x[half:])
    return final(x, axis=0, keepdims=True)


def _attn_body(qt_ref, k_ref, vt_ref, o_ref, q_all, m_ref, l_ref, acc_ref, s_ref, p_ref, *, tq, grp, wide):
    j = pl.program_id(2)
    d = ATT_HEAD_DIM
    mq = grp * tq
    chunks = mq // LANE

    @pl.when(j == 0)
    def _():
        for h in range(grp):
            for t in range(0, tq, wide):
                q_all[(h * tq + t) // wide] = qt_ref[h, :, t:t + wide]
        m_ref[...] = jnp.full(m_ref.shape, -jnp.inf, F32)
        l_ref[...] = jnp.zeros(l_ref.shape, F32)
        acc_ref[...] = jnp.zeros(acc_ref.shape, F32)

    k = k_ref[0]
    vt = vt_ref[0]
    per = wide // LANE
    groups = mq // wide

    for g in range(groups):
        st = jnp.dot(k, q_all[g], preferred_element_type=F32)
        for c in range(per):
            s_ref[g * per + c] = st[:, c * LANE:(c + 1) * LANE]
    alphas = []
    for ci in range(chunks):
        sc = s_ref[ci]
        m_prev = m_ref[ci]
        m_new = jnp.maximum(m_prev, _col_reduce(sc, jnp.maximum, jnp.max))
        alpha = jnp.exp2(m_prev - m_new)
        p = jnp.exp2(sc - m_new)
        l_ref[ci] = alpha * l_ref[ci] + _col_reduce(p, jnp.add, jnp.sum)
        m_ref[ci] = m_new
        p_ref[ci] = p.astype(BF16)
        alphas.append(alpha)
    pt = jnp.concatenate([p_ref[ci] for ci in range(chunks)], axis=1)
    pv = jnp.dot(vt, pt, preferred_element_type=F32)
    for ci in range(chunks):
        acc_ref[ci] = alphas[ci] * acc_ref[ci] + pv[:, ci * LANE:(ci + 1) * LANE]

    @pl.when(j == pl.num_programs(2) - 1)
    def _():
        per_head = tq // LANE
        for c in range(chunks):
            h, tb = divmod(c, per_head)
            out_t = acc_ref[c] / l_ref[c]
            o_ref[tb * LANE:(tb + 1) * LANE, h * d:(h + 1) * d] = out_t.T.astype(o_ref.dtype)


def _attention(qt, k, vt, *, tq=512, tk=1024, wide=2 * LANE):
    _, s, d = k.shape
    grp = ATT_HEADS // ATT_KV_HEADS
    mq = grp * tq
    return pl.pallas_call(
        functools.partial(_attn_body, tq=tq, grp=grp, wide=wide),
        grid=(ATT_KV_HEADS, s // tq, s // tk),
        in_specs=[
            pl.BlockSpec((grp, d, tq), lambda h, i, j: (h, 0, i)),
            pl.BlockSpec((1, tk, d), lambda h, i, j: (h, j, 0)),
            pl.BlockSpec((1, d, tk), lambda h, i, j: (h, 0, j)),
        ],
        out_specs=pl.BlockSpec((tq, grp * d), lambda h, i, j: (i, h)),
        out_shape=jax.ShapeDtypeStruct((s, ATT_HEADS * d), BF16),
        scratch_shapes=[
            pltpu.VMEM((mq // wide, d, wide), BF16),
            pltpu.VMEM((mq // LANE, 1, LANE), F32),
            pltpu.VMEM((mq // LANE, 1, LANE), F32),
            pltpu.VMEM((mq // LANE, d, LANE), F32),
            pltpu.VMEM((mq // LANE, tk, LANE), F32),
            pltpu.VMEM((mq // LANE, tk, LANE), BF16),
        ],
        compiler_params=_params("parallel", "parallel", "arbitrary"),
        name="attention",
    )(qt, k, vt)


def _pool_body(prev_ref, x_ref, next_ref, g_ref, w_ref, ls_ref, o_ref, xc_ref, *, tm, s_total):
    i = pl.program_id(0)
    halo = SUBLANE
    g = g_ref[...]
    xm = x_ref[...]
    xc_ref[halo:halo + tm, :] = _rms(xm, g)
    xc_ref[0:halo, :] = jnp.where(i > 0, _rms(prev_ref[...], g), 0.0)
    xc_ref[halo + tm:2 * halo + tm, :] = jnp.where(i < pl.num_programs(0) - 1, _rms(next_ref[...], g), 0.0)
    row = i * tm + lax.broadcasted_iota(jnp.int32, (tm, 1), 0)
    p = xm.shape[1] // len(POOL_WINDOWS)
    outs = []
    for gi, window in enumerate(POOL_WINDOWS):
        hf = window // 2
        cols = slice(gi * p, (gi + 1) * p)
        acc = xc_ref[halo - hf:halo - hf + tm, cols]
        for o in range(-hf + 1, hf):
            acc = acc + xc_ref[halo + o:halo + o + tm, cols]
        cnt = (jnp.minimum(row + hf, s_total) - jnp.maximum(row - hf, 0)).astype(F32)
        pooled = acc / cnt - xc_ref[halo:halo + tm, cols]
        outs.append(jnp.dot(pooled.astype(BF16), w_ref[gi], preferred_element_type=F32))
    o_ref[...] = xm + jnp.concatenate(outs, axis=-1) * ls_ref[...]


def _pool_mixer(h, g, pool_w, layer_scale, *, tm=256):
    s, d = h.shape
    nb = tm // SUBLANE
    ng, p, _ = pool_w.shape
    return pl.pallas_call(
        functools.partial(_pool_body, tm=tm, s_total=s),
        grid=(s // tm,),
        in_specs=[
            pl.BlockSpec((SUBLANE, d), lambda i: (jnp.maximum(i * nb - 1, 0), 0)),
            pl.BlockSpec((tm, d), lambda i: (i, 0)),
            pl.BlockSpec((SUBLANE, d), lambda i: (jnp.minimum((i + 1) * nb, s // SUBLANE - 1), 0)),
            pl.BlockSpec((1, d), lambda i: (0, 0)),
            pl.BlockSpec((ng, p, p), lambda i: (0, 0, 0)),
            pl.BlockSpec((1, d), lambda i: (0, 0)),
        ],
        out_specs=pl.BlockSpec((tm, d), lambda i: (i, 0)),
        out_shape=jax.ShapeDtypeStruct((s, d), F32),
        scratch_shapes=[pltpu.VMEM((tm + 2 * SUBLANE, d), F32)],
        compiler_params=_params("parallel"),
        name="pool_mixer",
    )(h, h, h, g.reshape(1, d), pool_w.astype(BF16), layer_scale.reshape(1, d))


NEG_INF = float("-inf")


def _top_values(x, count):
    rows = []
    for _ in range(count):
        m = jnp.max(x, axis=0, keepdims=True)
        rows.append(m)
        x = jnp.where(x == m, NEG_INF, x)
    return rows


def _stack_rows(rows, t, n=None):
    n = len(rows) if n is None else n
    rid = lax.broadcasted_iota(jnp.int32, (n, t), 0)
    out = jnp.full((n, t), NEG_INF, F32)
    for r, row in enumerate(rows):
        out = jnp.where(rid == r, row, out)
    return out


def _route_head(s1, s2):
    t = s1.shape[1]
    kk = PEER_TOPK
    v1 = _top_values(s1, kk + 1)
    v2 = _top_values(s2, kk + 1)
    v2_all = _stack_rows(v2[:kk], t)
    pieces = []
    for a in range(kk // 2):
        nb = kk // (a + 1)
        rows = kk if nb > SUBLANE else SUBLANE
        rid = lax.broadcasted_iota(jnp.int32, (rows, t), 0)
        pieces.append(jnp.where(rid < nb, v1[a] + v2_all[:rows], NEG_INF))
    tail = [v1[a] + v2[0] for a in range(kk // 2, kk + 1)] + [v1[0] + v2[kk]]
    pieces.append(_stack_rows(tail, t, kk))
    cand = jnp.concatenate(pieces, axis=0)
    best = _top_values(cand, kk + 1)
    tau = 0.5 * (best[kk - 1] + best[kk])
    z = jnp.ones_like(tau)
    for r in range(1, kk):
        z = z + jnp.exp(best[r] - best[0])
    thr = tau - s1
    e1 = jnp.exp(s1 - v1[0])
    e2 = jnp.exp(s2 - v2[0]) / z
    return thr, e1, e2


def _peer_route_body(h_ref, g_ref, wq_ref, keys_ref, xnt_ref, thr_ref, s2_ref, e1_ref, e2_ref):
    nk = PEER_NKEYS
    xnt = _rms(h_ref[...], g_ref[...]).T.astype(BF16)
    xnt_ref[...] = xnt
    qt = jnp.dot(wq_ref[...], xnt, preferred_element_type=F32)
    for hd in range(PEER_HEADS):
        q1 = qt[(2 * hd) * nk:(2 * hd + 1) * nk].astype(BF16)
        q2 = qt[(2 * hd + 1) * nk:(2 * hd + 2) * nk].astype(BF16)
        s1 = jnp.dot(keys_ref[2 * hd], q1, preferred_element_type=F32)
        s2 = jnp.dot(keys_ref[2 * hd + 1], q2, preferred_element_type=F32)
        thr, e1, e2 = _route_head(s1, s2)
        thr_ref[hd] = thr
        e1_ref[hd] = e1
        for c in range(s2.shape[1] // LANE):
            s2_ref[hd, c] = s2[:, c * LANE:(c + 1) * LANE]
            e2_ref[hd, c] = e2[:, c * LANE:(c + 1) * LANE]


def _peer_route(h, g, wq_t, keys, *, tt=256):
    s, d = h.shape
    nq = wq_t.shape[0]
    hk = jax.ShapeDtypeStruct((PEER_HEADS, PEER_NKEYS, s), F32)
    hck = jax.ShapeDtypeStruct((PEER_HEADS, s // LANE, PEER_NKEYS, LANE), F32)
    gate_spec = pl.BlockSpec((PEER_HEADS, PEER_NKEYS, tt), lambda i: (0, 0, i))
    chunk_spec = pl.BlockSpec((PEER_HEADS, tt // LANE, PEER_NKEYS, LANE), lambda i: (0, i, 0, 0))
    return pl.pallas_call(
        _peer_route_body,
        grid=(s // tt,),
        in_specs=[
            pl.BlockSpec((tt, d), lambda i: (i, 0)),
            pl.BlockSpec((1, d), lambda i: (0, 0)),
            pl.BlockSpec((nq, d), lambda i: (0, 0)),
            pl.BlockSpec(keys.shape, lambda i: (0, 0, 0)),
        ],
        out_specs=[pl.BlockSpec((d, tt), lambda i: (0, i)), gate_spec, chunk_spec, gate_spec, chunk_spec],
        out_shape=[jax.ShapeDtypeStruct((d, s), BF16), hk, hck, hk, hck],
        compiler_params=_params("parallel"),
        name="peer_route",
    )(h, g.reshape(1, d), wq_t, keys)


def _gelu(x):
    return 0.5 * x * (1.0 + lax.erf(x * (2.0 ** -0.5)))


def _peer_dense_body(xnt_ref, u_ref, vt_ref, thr_ref, e1_ref, s2_ref, e2_ref, o_ref, a_ref, p_ref, *, tt, eb, eg):
    e = pl.program_id(1)
    nk = PEER_NKEYS
    rb = 4 * SUBLANE
    chunks = tt // LANE

    @pl.when(e == 0)
    def _():
        o_ref[...] = jnp.zeros(o_ref.shape, F32)

    def activations(rows):
        a = jnp.dot(u_ref[rows, :], xnt_ref[...], preferred_element_type=F32)
        for c in range(chunks):
            a_ref[c, rows, :] = a[:, c * LANE:(c + 1) * LANE]

    def gated(rows):
        ks = range(rows.start // nk, rows.stop // nk)
        for c in range(chunks):
            lanes = slice(c * LANE, (c + 1) * LANE)
            for r in range(0, nk, rb):
                gates = {k: jnp.zeros((rb, LANE), F32) for k in ks}
                for hd in range(PEER_HEADS):
                    s2 = s2_ref[hd, c, r:r + rb, :]
                    e2 = e2_ref[hd, c, r:r + rb, :]
                    for k in ks:
                        thr = thr_ref[hd, 0, k:k + 1, lanes]
                        e1 = e1_ref[hd, 0, k:k + 1, lanes]
                        gates[k] = gates[k] + jnp.where(s2 >= thr, e2 * e1, 0.0)
                for k in ks:
                    blk = slice(k * nk + r, k * nk + r + rb)
                    p_ref[c, blk, :] = (gates[k] * _gelu(a_ref[c, blk, :])).astype(BF16)

    def project(rows):
        p = jnp.concatenate([p_ref[c, rows, :] for c in range(chunks)], axis=1)
        o_ref[...] += jnp.dot(vt_ref[:, rows], p, preferred_element_type=F32)

    groups = [slice(r, r + eg) for r in range(0, eb, eg)]
    activations(groups[0])
    for g, rows in enumerate(groups):
        if g + 1 < len(groups):
            activations(groups[g + 1])
        gated(rows)
        project(rows)


def _peer_dense(xnt, u, vt, thr, e1, s2, e2, *, tt=512, eb=1024, eg=512):
    d, s = xnt.shape
    ne = u.shape[0]
    nk = PEER_NKEYS
    rows = eb // nk
    split = lambda a: a.reshape(PEER_HEADS, ne // eb, rows, s)
    row_spec = pl.BlockSpec((PEER_HEADS, 1, rows, tt), lambda t, e: (0, e, 0, t))
    full_spec = pl.BlockSpec((PEER_HEADS, tt // LANE, nk, LANE), lambda t, e: (0, t, 0, 0))
    return pl.pallas_call(
        functools.partial(_peer_dense_body, tt=tt, eb=eb, eg=eg),
        grid=(s // tt, ne // eb),
        in_specs=[
            pl.BlockSpec((d, tt), lambda t, e: (0, t)),
            pl.BlockSpec((eb, d), lambda t, e: (e, 0)),
            pl.BlockSpec((d, eb), lambda t, e: (0, e)),
            row_spec, row_spec, full_spec, full_spec,
        ],
        out_specs=pl.BlockSpec((d, tt), lambda t, e: (0, t)),
        out_shape=jax.ShapeDtypeStruct((d, s), F32),
        scratch_shapes=[pltpu.VMEM((tt // LANE, eb, LANE), F32), pltpu.VMEM((tt // LANE, eb, LANE), BF16)],
        compiler_params=_params("parallel", "arbitrary"),
        name="peer_dense",
    )(xnt, u, vt, split(thr), split(e1), s2, e2)


def _add_t_body(h_ref, pt_ref, g_ref, o_ref, *, final_norm):
    y = h_ref[...] + pt_ref[...].T
    o_ref[...] = _rms(y, g_ref[...]) if final_norm else y


def _add_transposed(h, pt, g=None, *, tm=512):
    s, d = h.shape
    gain = jnp.ones((1, d), F32) if g is None else g.reshape(1, d)
    return pl.pallas_call(
        functools.partial(_add_t_body, final_norm=g is not None),
        grid=(s // tm,),
        in_specs=[
            pl.BlockSpec((tm, d), lambda i: (i, 0)),
            pl.BlockSpec((d, tm), lambda i: (0, i)),
            pl.BlockSpec((1, d), lambda i: (0, 0)),
        ],
        out_specs=pl.BlockSpec((tm, d), lambda i: (i, 0)),
        out_shape=jax.ShapeDtypeStruct((s, d), F32),
        compiler_params=_params("parallel"),
        name="add_transposed",
    )(h, pt, gain)


def _peer_ffn(h, g, w_query, sub_keys, u, v, final_g=None):
    nk = PEER_NKEYS
    keys = sub_keys.reshape(2 * PEER_HEADS, nk, -1).astype(BF16)
    xnt, thr, s2, e1, e2 = _peer_route(h, g, w_query.T.astype(BF16), keys)
    pt = _peer_dense(xnt, u.astype(BF16), v.T.astype(BF16), thr, e1, s2, e2)
    return _add_transposed(h, pt, final_g)


def _mlstm_prep_body(qp_ref, q_ref, qn_ref, kp_ref, k_ref, kn_ref, v_ref, cwq_ref, cwk_ref,
                     qo_ref, kto_ref, vo_ref, xc_ref, *, tm):
    i = pl.program_id(0)
    last = pl.num_programs(0) - 1
    halo = SUBLANE
    pad = CONV_W // 2

    def conv_silu(prev_ref, x_ref, next_ref, w_ref):
        xc_ref[halo:halo + tm, :] = x_ref[...]
        xc_ref[0:halo, :] = jnp.where(i > 0, prev_ref[...], 0.0)
        xc_ref[halo + tm:2 * halo + tm, :] = jnp.where(i < last, next_ref[...], 0.0)
        acc = xc_ref[halo - pad:halo - pad + tm, :] * w_ref[0:1, :]
        for j in range(1, CONV_W):
            acc = acc + xc_ref[halo - pad + j:halo - pad + j + tm, :] * w_ref[j:j + 1, :]
        return acc * jax.nn.sigmoid(acc)

    qo_ref[...] = conv_silu(qp_ref, q_ref, qn_ref, cwq_ref).astype(BF16)
    kc = conv_silu(kp_ref, k_ref, kn_ref, cwk_ref) * (ML_HEAD_DIM ** -0.5)
    kto_ref[...] = kc.T.astype(BF16)
    vo_ref[...] = v_ref[...].astype(BF16)


def _mlstm_prep(proj, conv_w, *, tm=256):
    s = proj.shape[0]
    w = ML_W
    nb = tm // SUBLANE
    prev = lambda c: pl.BlockSpec((SUBLANE, w), lambda i: (jnp.maximum(i * nb - 1, 0), c))
    main = lambda c: pl.BlockSpec((tm, w), lambda i: (i, c))
    nxt = lambda c: pl.BlockSpec((SUBLANE, w), lambda i: (jnp.minimum((i + 1) * nb, s // SUBLANE - 1), c))
    cw = pl.BlockSpec((CONV_W, w), lambda i: (0, 0))
    return pl.pallas_call(
        functools.partial(_mlstm_prep_body, tm=tm),
        grid=(s // tm,),
        in_specs=[prev(COL_MQ), main(COL_MQ), nxt(COL_MQ), prev(COL_MK), main(COL_MK), nxt(COL_MK),
                  main(COL_MV), cw, cw],
        out_specs=[
            pl.BlockSpec((tm, w), lambda i: (i, 0)),
            pl.BlockSpec((w, tm), lambda i: (0, i)),
            pl.BlockSpec((tm, w), lambda i: (i, 0)),
        ],
        out_shape=[
            jax.ShapeDtypeStruct((s, w), BF16),
            jax.ShapeDtypeStruct((w, s), BF16),
            jax.ShapeDtypeStruct((s, w), BF16),
        ],
        scratch_shapes=[pltpu.VMEM((tm + 2 * SUBLANE, w), F32)],
        compiler_params=_params("parallel"),
        name="mlstm_prep",
    )(proj, proj, proj, proj, proj, proj, proj, conv_w[:, :w], conv_w[:, w:])


def _log_sigmoid(x):
    return jnp.minimum(x, 0.0) - jnp.log1p(jnp.exp(-jnp.abs(x)))


def _mlstm_direction(q_ref, kt_ref, v_ref, g_ref, bias_ref, o_ref, s_ref, m_ref, *, backward):
    L = ML_CHUNK
    d = ML_HEAD_DIM
    row = lax.broadcasted_iota(jnp.int32, (L, L), 0)
    col = lax.broadcasted_iota(jnp.int32, (L, L), 1)
    keep = (col >= row) if backward else (col <= row)
    tri = keep.astype(BF16)
    ones = jnp.ones((L, LANE), BF16)

    gates = g_ref[...] + bias_ref[...]
    logf = _log_sigmoid(gates)
    hi = logf.astype(BF16)
    lo = (logf - hi.astype(F32)).astype(BF16)
    bcol = jnp.dot(tri, hi, preferred_element_type=F32) + jnp.dot(tri, lo, preferred_element_type=F32)
    gates_t = gates.T
    b_t = bcol.T
    base = 2 * ML_HEADS if backward else 0
    for h in range(ML_HEADS):
        idx = base // 2 + h
        ci, cf = base + h, base + ML_HEADS + h
        i_row = gates_t[ci:ci + 1, :]
        b_row = b_t[cf:cf + 1, :]
        b_col = bcol[:, cf:cf + 1]
        g_tot = b_row[:, 0:1] if backward else b_row[:, L - 1:L]
        a_row = g_tot - b_row + i_row
        a_max = jnp.max(a_row, axis=1, keepdims=True)
        w_row = jnp.exp(a_row - a_max)

        q = q_ref[:, h * d:(h + 1) * d]
        kt = kt_ref[h * d:(h + 1) * d, :]
        v_aug = jnp.concatenate([v_ref[:, h * d:(h + 1) * d], ones], axis=1)
        s_prev = s_ref[idx]
        m_prev = m_ref[idx][:, 0:1]

        dlog = jnp.where(keep, b_col + (i_row - b_row), NEG_INF)
        inter_log = b_col + m_prev
        m_j = jnp.maximum(inter_log, jnp.max(dlog, axis=1, keepdims=True))
        sqk = jnp.dot(q, kt, preferred_element_type=F32) * jnp.exp(dlog - m_j)
        inter_w = jnp.exp(inter_log - m_j)
        tot = (inter_w * jnp.dot(q, s_prev.astype(BF16), preferred_element_type=F32)
               + jnp.dot(sqk.astype(BF16), v_aug, preferred_element_type=F32))
        den = tot[:, d:]
        inv = 1.0 / jnp.maximum(jnp.abs(den), jnp.exp(-m_j))
        o_ref[:, h * d:(h + 1) * d] = tot[:, :d] * jnp.concatenate([inv] * (d // LANE), axis=1)

        ktw = (kt.astype(F32) * w_row).astype(BF16)
        upd = jnp.dot(ktw, v_aug, preferred_element_type=F32)
        m_new = jnp.maximum(g_tot + m_prev, a_max)
        s_ref[idx] = jnp.exp(g_tot + m_prev - m_new) * s_prev + jnp.exp(a_max - m_new) * upd
        m_ref[idx] = jnp.broadcast_to(m_new, (1, LANE))


def _mlstm_body(qf_ref, ktf_ref, vf_ref, gf_ref, qb_ref, ktb_ref, vb_ref, gb_ref, bias_ref,
                hf_ref, hb_ref, s_ref, m_ref):
    @pl.when(pl.program_id(0) == 0)
    def _():
        s_ref[...] = jnp.zeros(s_ref.shape, F32)
        m_ref[...] = jnp.zeros(m_ref.shape, F32)

    _mlstm_direction(qf_ref, ktf_ref, vf_ref, gf_ref, bias_ref, hf_ref, s_ref, m_ref, backward=False)
    _mlstm_direction(qb_ref, ktb_ref, vb_ref, gb_ref, bias_ref, hb_ref, s_ref, m_ref, backward=True)


def _mlstm_scan(mq, mkt, mv, gates, b_gate):
    s, w = mq.shape
    L = ML_CHUNK
    nc = s // L
    d = ML_HEAD_DIM
    bias = jnp.pad(b_gate, (0, LANE - b_gate.shape[0])).reshape(1, LANE)
    fwd = lambda c: c
    bwd = lambda c: nc - 1 - c
    rows = lambda ix, n: pl.BlockSpec((L, n), lambda c: (ix(c), 0))
    cols = lambda ix: pl.BlockSpec((w, L), lambda c: (0, ix(c)))
    return pl.pallas_call(
        _mlstm_body,
        grid=(nc,),
        in_specs=[rows(fwd, w), cols(fwd), rows(fwd, w), rows(fwd, LANE),
                  rows(bwd, w), cols(bwd), rows(bwd, w), rows(bwd, LANE),
                  pl.BlockSpec((1, LANE), lambda c: (0, 0))],
        out_specs=[rows(fwd, w), rows(bwd, w)],
        out_shape=[jax.ShapeDtypeStruct((s, w), F32)] * 2,
        scratch_shapes=[pltpu.VMEM((2 * ML_HEADS, d, d + LANE), F32), pltpu.VMEM((2 * ML_HEADS, 1, LANE), F32)],
        compiler_params=_params("arbitrary"),
        name="mlstm_scan",
    )(mq, mkt, mv, gates, mq, mkt, mv, gates, bias)


def _mix_out_body(ya_ref, hf_ref, hb_ref, mo_ref, mg_ref, wa_ref, wm_ref, r_ref, o_ref, yml_ref):
    @pl.when(pl.program_id(1) == 0)
    def _():
        d = ML_HEAD_DIM
        for h in range(ML_HEADS):
            sl = slice(h * d, (h + 1) * d)
            hm = _rms(hf_ref[:, sl] + hb_ref[:, sl], mg_ref[:, sl])
            yml_ref[:, sl] = (hm * jax.nn.sigmoid(mo_ref[:, sl])).astype(BF16)

    o_ref[...] = (r_ref[...] + jnp.dot(ya_ref[...], wa_ref[...], preferred_element_type=F32)
                  + jnp.dot(yml_ref[...], wm_ref[...], preferred_element_type=F32))


def _mix_out(y_att, hf, hb, proj, ml_gain, w_att, w_ml, res, *, tm=512, tn=512):
    s, n = res.shape
    wa, wm = w_att.shape[0], w_ml.shape[0]
    return pl.pallas_call(
        _mix_out_body,
        grid=(s // tm, n // tn),
        in_specs=[
            pl.BlockSpec((tm, wa), lambda i, j: (i, 0)),
            pl.BlockSpec((tm, wm), lambda i, j: (i, 0)),
            pl.BlockSpec((tm, wm), lambda i, j: (i, 0)),
            pl.BlockSpec((tm, wm), lambda i, j: (i, COL_MO)),
            pl.BlockSpec((1, wm), lambda i, j: (0, 0)),
            pl.BlockSpec((wa, tn), lambda i, j: (0, j)),
            pl.BlockSpec((wm, tn), lambda i, j: (0, j)),
            pl.BlockSpec((tm, tn), lambda i, j: (i, j)),
        ],
        out_specs=pl.BlockSpec((tm, tn), lambda i, j: (i, j)),
        out_shape=jax.ShapeDtypeStruct((s, n), F32),
        scratch_shapes=[pltpu.VMEM((tm, wm), BF16)],
        compiler_params=_params("parallel", "arbitrary"),
        name="mix_out",
    )(y_att, hf, hb, proj, ml_gain.reshape(1, wm), w_att, w_ml, res)


def kernel(x, norm_mix_g, norm_ffn_g, w_in, b_gate, conv_w, q_gain, k_gain, ml_gain, w_out, pool_w,
           layer_scale, peer_wq, peer_keys, peer_u, peer_v, norm_f):
    b, s, d = x.shape
    h = x.reshape(b * s, d)

    wi = w_in[0]
    kv_end = ATT_Q + 2 * ATT_KV
    w_main = jnp.concatenate([wi[:, :ATT_Q], wi[:, kv_end:MAIN_W], wi[:, ATT_Q:kv_end]], axis=1).astype(BF16)
    w_gate = jnp.pad(wi[:, MAIN_W:], ((0, 0), (0, LANE - (wi.shape[1] - MAIN_W)))).astype(BF16)
    proj, gates = _norm_matmul(h, norm_mix_g[0], w_main, w_gate)
    qt, k, vt = _attn_prep(proj, q_gain[0], k_gain[0])
    y_att = _attention(qt, k, vt)
    mq, mkt, mv = _mlstm_prep(proj, conv_w[0])
    hf, hb = _mlstm_scan(mq, mkt, mv, gates, b_gate[0])
    wo = w_out[0].astype(BF16)
    h = _mix_out(y_att, hf, hb, proj, ml_gain[0], wo[:ATT_Q], wo[ATT_Q:], h)
    h = _peer_ffn(h, norm_ffn_g[0], peer_wq[0], peer_keys[0], peer_u[0], peer_v[0])

    h = _pool_mixer(h, norm_mix_g[1], pool_w[0], layer_scale[0])
    h = _peer_ffn(h, norm_ffn_g[1], peer_wq[1], peer_keys[1], peer_u[1], peer_v[1], final_g=norm_f)
    return h.reshape(b, s, d)
```

```python
import functools

import jax
import jax.numpy as jnp
from jax import lax
from jax.experimental import pallas as pl
from jax.experimental.pallas import tpu as pltpu

F32 = jnp.float32
BF16 = jnp.bfloat16
EPS = 1e-6
LOG2E = 1.4426950408889634

GRID_W = 64
ROPE_THETA = 10000.0
ATT_HEADS = 8
ATT_KV_HEADS = 2
ATT_HEAD_DIM = 128
ML_HEADS = 4
ML_HEAD_DIM = 256
ML_CHUNK = 128
CONV_W = 5
POOL_WINDOWS = (2, 4, 8, 16)
PEER_HEADS = 8
PEER_NKEYS = 128
PEER_TOPK = 16
PEER_BLOCK = 128

ATT_Q = ATT_HEADS * ATT_HEAD_DIM
ATT_KV = ATT_KV_HEADS * ATT_HEAD_DIM
ML_W = ML_HEADS * ML_HEAD_DIM
MAIN_W = ATT_Q + 2 * ATT_KV + 4 * ML_W
assert ATT_Q == ML_W
COL_AQ, COL_MQ, COL_MK, COL_MV, COL_MO = 0, 1, 2, 3, 4
COL_AK = (ATT_Q + 4 * ML_W) // ATT_KV
COL_AV = COL_AK + 1
LANE = 128
SUBLANE = 8
VMEM_LIMIT = 48 * 1024 * 1024


def _params(*sem):
    return pltpu.CompilerParams(dimension_semantics=sem, vmem_limit_bytes=VMEM_LIMIT)


def _rms(x, g):
    return x * lax.rsqrt(jnp.mean(x * x, axis=-1, keepdims=True) + EPS) * g


def _norm_mm_body(x_ref, g_ref, w_ref, wg_ref, o_ref, og_ref, xn_ref):
    @pl.when(pl.program_id(1) == 0)
    def _():
        xn = _rms(x_ref[...], g_ref[...]).astype(BF16)
        xn_ref[...] = xn
        og_ref[...] = jnp.dot(xn, wg_ref[...], preferred_element_type=F32)

    o_ref[...] = jnp.dot(xn_ref[...], w_ref[...], preferred_element_type=F32)


def _norm_matmul(x, g, w, wg, *, tm=512, tn=512):
    s, d = x.shape
    n = w.shape[1]
    return pl.pallas_call(
        _norm_mm_body,
        grid=(s // tm, n // tn),
        in_specs=[
            pl.BlockSpec((tm, d), lambda i, j: (i, 0)),
            pl.BlockSpec((1, d), lambda i, j: (0, 0)),
            pl.BlockSpec((d, tn), lambda i, j: (0, j)),
            pl.BlockSpec((d, LANE), lambda i, j: (0, 0)),
        ],
        out_specs=[
            pl.BlockSpec((tm, tn), lambda i, j: (i, j)),
            pl.BlockSpec((tm, LANE), lambda i, j: (i, 0)),
        ],
        out_shape=[jax.ShapeDtypeStruct((s, n), F32), jax.ShapeDtypeStruct((s, LANE), F32)],
        scratch_shapes=[pltpu.VMEM((tm, d), BF16)],
        compiler_params=_params("parallel", "arbitrary"),
        name="norm_matmul",
    )(x, g.reshape(1, d), w, wg)


def _rope_tables(s):
    rows = s // GRID_W
    r, c = jnp.meshgrid(jnp.arange(rows), jnp.arange(GRID_W), indexing="ij")
    pos = jnp.stack([r.reshape(-1), c.reshape(-1)], axis=-1).astype(F32)
    nf = ATT_HEAD_DIM // 4
    inv = ROPE_THETA ** (-jnp.arange(nf, dtype=F32) / nf)
    ang = pos[:, :, None] * inv
    cos, sin = jnp.cos(ang), jnp.sin(ang)
    ctab = jnp.stack([cos, cos], axis=2).reshape(s, ATT_HEAD_DIM)
    stab = jnp.stack([-sin, sin], axis=2).reshape(s, ATT_HEAD_DIM)
    return ctab, stab


def _rope(y, ctab, stab, first_half):
    nf = ATT_HEAD_DIM // 4
    swapped = jnp.where(first_half, pltpu.roll(y, ATT_HEAD_DIM - nf, 1), pltpu.roll(y, nf, 1))
    return y * ctab + swapped * stab


def _attn_prep_body(aq_ref, ak_ref, av_ref, c_ref, s_ref, qg_ref, kg_ref, qt_ref, k_ref, vt_ref):
    d = ATT_HEAD_DIM
    ctab, stab = c_ref[...], s_ref[...]
    lane = lax.broadcasted_iota(jnp.int32, ctab.shape, 1)
    first_half = (lane % (d // 2)) < (d // 4)
    scale = d ** -0.5 * LOG2E
    for h in range(ATT_HEADS):
        y = _rms(aq_ref[:, h * d:(h + 1) * d], qg_ref[...])
        qt_ref[h] = (_rope(y, ctab, stab, first_half) * scale).T.astype(BF16)
    for h in range(ATT_KV_HEADS):
        y = _rms(ak_ref[:, h * d:(h + 1) * d], kg_ref[...])
        k_ref[h] = _rope(y, ctab, stab, first_half).astype(BF16)
        vt_ref[h] = av_ref[:, h * d:(h + 1) * d].T.astype(BF16)


def _attn_prep(proj, q_gain, k_gain, *, tm=512):
    s = proj.shape[0]
    d = ATT_HEAD_DIM
    ctab, stab = _rope_tables(s)
    return pl.pallas_call(
        _attn_prep_body,
        grid=(s // tm,),
        in_specs=[
            pl.BlockSpec((tm, ATT_Q), lambda i: (i, COL_AQ)),
            pl.BlockSpec((tm, ATT_KV), lambda i: (i, COL_AK)),
            pl.BlockSpec((tm, ATT_KV), lambda i: (i, COL_AV)),
            pl.BlockSpec((tm, d), lambda i: (i, 0)),
            pl.BlockSpec((tm, d), lambda i: (i, 0)),
            pl.BlockSpec((1, d), lambda i: (0, 0)),
            pl.BlockSpec((1, d), lambda i: (0, 0)),
        ],
        out_specs=[
            pl.BlockSpec((ATT_HEADS, d, tm), lambda i: (0, 0, i)),
            pl.BlockSpec((ATT_KV_HEADS, tm, d), lambda i: (0, i, 0)),
            pl.BlockSpec((ATT_KV_HEADS, d, tm), lambda i: (0, 0, i)),
        ],
        out_shape=[
            jax.ShapeDtypeStruct((ATT_HEADS, d, s), BF16),
            jax.ShapeDtypeStruct((ATT_KV_HEADS, s, d), BF16),
            jax.ShapeDtypeStruct((ATT_KV_HEADS, d, s), BF16),
        ],
        compiler_params=_params("parallel"),
        name="attn_prep",
    )(proj, proj, proj, ctab, stab, q_gain.reshape(1, d), k_gain.reshape(1, d))


def _col_reduce(x, pair, final):
    while x.shape[0] > SUBLANE:
        half = x.shape[0] // 2
        x = pair(x[:half], x[half:])
    return final(x, axis=0, keepdims=True)


def _attn_body(qt_ref, k_ref, vt_ref, o_ref, q_all, m_ref, l_ref, acc_ref, s_ref, p_ref, *, tq, grp, wide):
    j = pl.program_id(2)
    d = ATT_HEAD_DIM
    mq = grp * tq
    chunks = mq // LANE

    @pl.when(j == 0)
    def _():
        for h in range(grp):
            for t in range(0, tq, wide):
                q_all[(h * tq + t) // wide] = qt_ref[h, :, t:t + wide]
        m_ref[...] = jnp.full(m_ref.shape, -jnp.inf, F32)
        l_ref[...] = jnp.zeros(l_ref.shape, F32)
        acc_ref[...] = jnp.zeros(acc_ref.shape, F32)

    k = k_ref[0]
    vt = vt_ref[0]
    per = wide // LANE
    groups = mq // wide

    for g in range(groups):
        st = jnp.dot(k, q_all[g], preferred_element_type=F32)
        for c in range(per):
            s_ref[g * per + c] = st[:, c * LANE:(c + 1) * LANE]
    alphas = []
    for ci in range(chunks):
        sc = s_ref[ci]
        m_prev = m_ref[ci]
        m_new = jnp.maximum(m_prev, _col_reduce(sc, jnp.maximum, jnp.max))
        alpha = jnp.exp2(m_prev - m_new)
        p = jnp.exp2(sc - m_new)
        l_ref[ci] = alpha * l_ref[ci] + _col_reduce(p, jnp.add, jnp.sum)
        m_ref[ci] = m_new
        p_ref[ci] = p.astype(BF16)
        alphas.append(alpha)
    pt = jnp.concatenate([p_ref[ci] for ci in range(chunks)], axis=1)
    pv = jnp.dot(vt, pt, preferred_element_type=F32)
    for ci in range(chunks):
        acc_ref[ci] = alphas[ci] * acc_ref[ci] + pv[:, ci * LANE:(ci + 1) * LANE]

    @pl.when(j == pl.num_programs(2) - 1)
    def _():
        per_head = tq // LANE
        for c in range(chunks):
            h, tb = divmod(c, per_head)
            out_t = acc_ref[c] / l_ref[c]
            o_ref[tb * LANE:(tb + 1) * LANE, h * d:(h + 1) * d] = out_t.T.astype(o_ref.dtype)


def _attention(qt, k, vt, *, tq=512, tk=1024, wide=2 * LANE):
    _, s, d = k.shape
    grp = ATT_HEADS // ATT_KV_HEADS
    mq = grp * tq
    return pl.pallas_call(
        functools.partial(_attn_body, tq=tq, grp=grp, wide=wide),
        grid=(ATT_KV_HEADS, s // tq, s // tk),
        in_specs=[
            pl.BlockSpec((grp, d, tq), lambda h, i, j: (h, 0, i)),
            pl.BlockSpec((1, tk, d), lambda h, i, j: (h, j, 0)),
            pl.BlockSpec((1, d, tk), lambda h, i, j: (h, 0, j)),
        ],
        out_specs=pl.BlockSpec((tq, grp * d), lambda h, i, j: (i, h)),
        out_shape=jax.ShapeDtypeStruct((s, ATT_HEADS * d), BF16),
        scratch_shapes=[
            pltpu.VMEM((mq // wide, d, wide), BF16),
            pltpu.VMEM((mq // LANE, 1, LANE), F32),
            pltpu.VMEM((mq // LANE, 1, LANE), F32),
            pltpu.VMEM((mq // LANE, d, LANE), F32),
            pltpu.VMEM((mq // LANE, tk, LANE), F32),
            pltpu.VMEM((mq // LANE, tk, LANE), BF16),
        ],
        compiler_params=_params("parallel", "parallel", "arbitrary"),
        name="attention",
    )(qt, k, vt)


def _pool_body(prev_ref, x_ref, next_ref, g_ref, w_ref, ls_ref, o_ref, xc_ref, *, tm, s_total):
    i = pl.program_id(0)
    halo = SUBLANE
    g = g_ref[...]
    xm = x_ref[...]
    xc_ref[halo:halo + tm, :] = _rms(xm, g)
    xc_ref[0:halo, :] = jnp.where(i > 0, _rms(prev_ref[...], g), 0.0)
    xc_ref[halo + tm:2 * halo + tm, :] = jnp.where(i < pl.num_programs(0) - 1, _rms(next_ref[...], g), 0.0)
    row = i * tm + lax.broadcasted_iota(jnp.int32, (tm, 1), 0)
    p = xm.shape[1] // len(POOL_WINDOWS)
    outs = []
    for gi, window in enumerate(POOL_WINDOWS):
        hf = window // 2
        cols = slice(gi * p, (gi + 1) * p)
        acc = xc_ref[halo - hf:halo - hf + tm, cols]
        for o in range(-hf + 1, hf):
            acc = acc + xc_ref[halo + o:halo + o + tm, cols]
        cnt = (jnp.minimum(row + hf, s_total) - jnp.maximum(row - hf, 0)).astype(F32)
        pooled = acc / cnt - xc_ref[halo:halo + tm, cols]
        outs.append(jnp.dot(pooled.astype(BF16), w_ref[gi], preferred_element_type=F32))
    o_ref[...] = xm + jnp.concatenate(outs, axis=-1) * ls_ref[...]


def _pool_mixer(h, g, pool_w, layer_scale, *, tm=256):
    s, d = h.shape
    nb = tm // SUBLANE
    ng, p, _ = pool_w.shape
    return pl.pallas_call(
        functools.partial(_pool_body, tm=tm, s_total=s),
        grid=(s // tm,),
        in_specs=[
            pl.BlockSpec((SUBLANE, d), lambda i: (jnp.maximum(i * nb - 1, 0), 0)),
            pl.BlockSpec((tm, d), lambda i: (i, 0)),
            pl.BlockSpec((SUBLANE, d), lambda i: (jnp.minimum((i + 1) * nb, s // SUBLANE - 1), 0)),
            pl.BlockSpec((1, d), lambda i: (0, 0)),
            pl.BlockSpec((ng, p, p), lambda i: (0, 0, 0)),
            pl.BlockSpec((1, d), lambda i: (0, 0)),
        ],
        out_specs=pl.BlockSpec((tm, d), lambda i: (i, 0)),
        out_shape=jax.ShapeDtypeStruct((s, d), F32),
        scratch_shapes=[pltpu.VMEM((tm + 2 * SUBLANE, d), F32)],
        compiler_params=_params("parallel"),
        name="pool_mixer",
    )(h, h, h, g.reshape(1, d), pool_w.astype(BF16), layer_scale.reshape(1, d))


NEG_INF = float("-inf")


def _top_values(x, count):
    rows = []
    for _ in range(count):
        m = jnp.max(x, axis=0, keepdims=True)
        rows.append(m)
        x = jnp.where(x == m, NEG_INF, x)
    return rows


def _stack_rows(rows, t, n=None):
    n = len(rows) if n is None else n
    rid = lax.broadcasted_iota(jnp.int32, (n, t), 0)
    out = jnp.full((n, t), NEG_INF, F32)
    for r, row in enumerate(rows):
        out = jnp.where(rid == r, row, out)
    return out


def _route_head(s1, s2):
    t = s1.shape[1]
    kk = PEER_TOPK
    v1 = _top_values(s1, kk + 1)
    v2 = _top_values(s2, kk + 1)
    v2_all = _stack_rows(v2[:kk], t)
    pieces = []
    for a in range(kk // 2):
        nb = kk // (a + 1)
        rows = kk if nb > SUBLANE else SUBLANE
        rid = lax.broadcasted_iota(jnp.int32, (rows, t), 0)
        pieces.append(jnp.where(rid < nb, v1[a] + v2_all[:rows], NEG_INF))
    tail = [v1[a] + v2[0] for a in range(kk // 2, kk + 1)] + [v1[0] + v2[kk]]
    pieces.append(_stack_rows(tail, t, kk))
    cand = jnp.concatenate(pieces, axis=0)
    best = _top_values(cand, kk + 1)
    tau = 0.5 * (best[kk - 1] + best[kk])
    z = jnp.ones_like(tau)
    for r in range(1, kk):
        z = z + jnp.exp(best[r] - best[0])
    thr = jnp.exp(tau - v2[0] - s1) / z
    e1 = jnp.exp(s1 - v1[0])
    e2 = jnp.exp(s2 - v2[0]) / z
    return thr, e1, e2


def _peer_route_body(h_ref, g_ref, wq_ref, keys_ref, xnt_ref, thr_ref, e1_ref, e2_ref):
    nk = PEER_NKEYS
    xnt = _rms(h_ref[...], g_ref[...]).T.astype(BF16)
    xnt_ref[...] = xnt
    qt = jnp.dot(wq_ref[...], xnt, preferred_element_type=F32)
    for hd in range(PEER_HEADS):
        q1 = qt[(2 * hd) * nk:(2 * hd + 1) * nk].astype(BF16)
        q2 = qt[(2 * hd + 1) * nk:(2 * hd + 2) * nk].astype(BF16)
        s1 = jnp.dot(keys_ref[2 * hd], q1, preferred_element_type=F32)
        s2 = jnp.dot(keys_ref[2 * hd + 1], q2, preferred_element_type=F32)
        thr, e1, e2 = _route_head(s1, s2)
        thr_ref[hd] = thr
        e1_ref[hd] = e1
        for c in range(e2.shape[1] // LANE):
            e2_ref[hd, c] = e2[:, c * LANE:(c + 1) * LANE]


def _peer_route(h, g, wq_t, keys, *, tt=256):
    s, d = h.shape
    nq = wq_t.shape[0]
    hk = jax.ShapeDtypeStruct((PEER_HEADS, PEER_NKEYS, s), F32)
    hck = jax.ShapeDtypeStruct((PEER_HEADS, s // LANE, PEER_NKEYS, LANE), F32)
    gate_spec = pl.BlockSpec((PEER_HEADS, PEER_NKEYS, tt), lambda i: (0, 0, i))
    chunk_spec = pl.BlockSpec((PEER_HEADS, tt // LANE, PEER_NKEYS, LANE), lambda i: (0, i, 0, 0))
    return pl.pallas_call(
        _peer_route_body,
        grid=(s // tt,),
        in_specs=[
            pl.BlockSpec((tt, d), lambda i: (i, 0)),
            pl.BlockSpec((1, d), lambda i: (0, 0)),
            pl.BlockSpec((nq, d), lambda i: (0, 0)),
            pl.BlockSpec(keys.shape, lambda i: (0, 0, 0)),
        ],
        out_specs=[pl.BlockSpec((d, tt), lambda i: (0, i)), gate_spec, gate_spec, chunk_spec],
        out_shape=[jax.ShapeDtypeStruct((d, s), BF16), hk, hk, hck],
        compiler_params=_params("parallel"),
        name="peer_route",
    )(h, g.reshape(1, d), wq_t, keys)


def _gelu(x):
    return 0.5 * x * (1.0 + lax.erf(x * (2.0 ** -0.5)))


def _peer_dense_body(xnt_ref, u_ref, vt_ref, thr_ref, e1_ref, e2_ref, o_ref, a_ref, p_ref, *, tt, eb, eg):
    e = pl.program_id(1)
    nk = PEER_NKEYS
    rb = 4 * SUBLANE
    chunks = tt // LANE

    @pl.when(e == 0)
    def _():
        o_ref[...] = jnp.zeros(o_ref.shape, F32)

    def activations(rows):
        a = jnp.dot(u_ref[rows, :], xnt_ref[...], preferred_element_type=F32)
        for c in range(chunks):
            a_ref[c, rows, :] = a[:, c * LANE:(c + 1) * LANE]

    def gated(rows):
        ks = range(rows.start // nk, rows.stop // nk)
        for c in range(chunks):
            lanes = slice(c * LANE, (c + 1) * LANE)
            for r in range(0, nk, rb):
                gates = {k: jnp.zeros((rb, LANE), F32) for k in ks}
                for hd in range(PEER_HEADS):
                    e2 = e2_ref[hd, c, r:r + rb, :]
                    for k in ks:
                        thr = thr_ref[hd, 0, k:k + 1, lanes]
                        e1 = e1_ref[hd, 0, k:k + 1, lanes]
                        gates[k] = gates[k] + jnp.where(e2 >= thr, e2 * e1, 0.0)
                for k in ks:
                    blk = slice(k * nk + r, k * nk + r + rb)
                    p_ref[c, blk, :] = (gates[k] * _gelu(a_ref[c, blk, :])).astype(BF16)

    def project(rows):
        p = jnp.concatenate([p_ref[c, rows, :] for c in range(chunks)], axis=1)
        o_ref[...] += jnp.dot(vt_ref[:, rows], p, preferred_element_type=F32)

    groups = [slice(r, r + eg) for r in range(0, eb, eg)]
    activations(groups[0])
    for g, rows in enumerate(groups):
        if g + 1 < len(groups):
            activations(groups[g + 1])
        gated(rows)
        project(rows)


def _peer_dense(xnt, u, vt, thr, e1, e2, *, tt=512, eb=1024, eg=512):
    d, s = xnt.shape
    ne = u.shape[0]
    nk = PEER_NKEYS
    rows = eb // nk
    split = lambda a: a.reshape(PEER_HEADS, ne // eb, rows, s)
    row_spec = pl.BlockSpec((PEER_HEADS, 1, rows, tt), lambda t, e: (0, e, 0, t))
    full_spec = pl.BlockSpec((PEER_HEADS, tt // LANE, nk, LANE), lambda t, e: (0, t, 0, 0))
    return pl.pallas_call(
        functools.partial(_peer_dense_body, tt=tt, eb=eb, eg=eg),
        grid=(s // tt, ne // eb),
        in_specs=[
            pl.BlockSpec((d, tt), lambda t, e: (0, t)),
            pl.BlockSpec((eb, d), lambda t, e: (e, 0)),
            pl.BlockSpec((d, eb), lambda t, e: (0, e)),
            row_spec, row_spec, full_spec,
        ],
        out_specs=pl.BlockSpec((d, tt), lambda t, e: (0, t)),
        out_shape=jax.ShapeDtypeStruct((d, s), F32),
        scratch_shapes=[pltpu.VMEM((tt // LANE, eb, LANE), F32), pltpu.VMEM((tt // LANE, eb, LANE), BF16)],
        compiler_params=_params("parallel", "arbitrary"),
        name="peer_dense",
    )(xnt, u, vt, split(thr), split(e1), e2)


def _add_t_body(h_ref, pt_ref, g_ref, o_ref, *, final_norm):
    y = h_ref[...] + pt_ref[...].T
    o_ref[...] = _rms(y, g_ref[...]) if final_norm else y


def _add_transposed(h, pt, g=None, *, tm=512):
    s, d = h.shape
    gain = jnp.ones((1, d), F32) if g is None else g.reshape(1, d)
    return pl.pallas_call(
        functools.partial(_add_t_body, final_norm=g is not None),
        grid=(s // tm,),
        in_specs=[
            pl.BlockSpec((tm, d), lambda i: (i, 0)),
            pl.BlockSpec((d, tm), lambda i: (0, i)),
            pl.BlockSpec((1, d), lambda i: (0, 0)),
        ],
        out_specs=pl.BlockSpec((tm, d), lambda i: (i, 0)),
        out_shape=jax.ShapeDtypeStruct((s, d), F32),
        compiler_params=_params("parallel"),
        name="add_transposed",
    )(h, pt, gain)


def _peer_ffn(h, g, w_query, sub_keys, u, v, final_g=None):
    nk = PEER_NKEYS
    keys = sub_keys.reshape(2 * PEER_HEADS, nk, -1).astype(BF16)
    xnt, thr, e1, e2 = _peer_route(h, g, w_query.T.astype(BF16), keys)
    pt = _peer_dense(xnt, u.astype(BF16), v.T.astype(BF16), thr, e1, e2)
    return _add_transposed(h, pt, final_g)


def _mlstm_prep_body(qp_ref, q_ref, qn_ref, kp_ref, k_ref, kn_ref, v_ref, cwq_ref, cwk_ref,
                     qo_ref, kto_ref, vo_ref, xc_ref, *, tm):
    i = pl.program_id(0)
    last = pl.num_programs(0) - 1
    halo = SUBLANE
    pad = CONV_W // 2

    def conv_silu(prev_ref, x_ref, next_ref, w_ref):
        xc_ref[halo:halo + tm, :] = x_ref[...]
        xc_ref[0:halo, :] = jnp.where(i > 0, prev_ref[...], 0.0)
        xc_ref[halo + tm:2 * halo + tm, :] = jnp.where(i < last, next_ref[...], 0.0)
        acc = xc_ref[halo - pad:halo - pad + tm, :] * w_ref[0:1, :]
        for j in range(1, CONV_W):
            acc = acc + xc_ref[halo - pad + j:halo - pad + j + tm, :] * w_ref[j:j + 1, :]
        return acc * jax.nn.sigmoid(acc)

    qo_ref[...] = conv_silu(qp_ref, q_ref, qn_ref, cwq_ref).astype(BF16)
    kc = conv_silu(kp_ref, k_ref, kn_ref, cwk_ref) * (ML_HEAD_DIM ** -0.5)
    kto_ref[...] = kc.T.astype(BF16)
    vo_ref[...] = v_ref[...].astype(BF16)


def _mlstm_prep(proj, conv_w, *, tm=256):
    s = proj.shape[0]
    w = ML_W
    nb = tm // SUBLANE
    prev = lambda c: pl.BlockSpec((SUBLANE, w), lambda i: (jnp.maximum(i * nb - 1, 0), c))
    main = lambda c: pl.BlockSpec((tm, w), lambda i: (i, c))
    nxt = lambda c: pl.BlockSpec((SUBLANE, w), lambda i: (jnp.minimum((i + 1) * nb, s // SUBLANE - 1), c))
    cw = pl.BlockSpec((CONV_W, w), lambda i: (0, 0))
    return pl.pallas_call(
        functools.partial(_mlstm_prep_body, tm=tm),
        grid=(s // tm,),
        in_specs=[prev(COL_MQ), main(COL_MQ), nxt(COL_MQ), prev(COL_MK), main(COL_MK), nxt(COL_MK),
                  main(COL_MV), cw, cw],
        out_specs=[
            pl.BlockSpec((tm, w), lambda i: (i, 0)),
            pl.BlockSpec((w, tm), lambda i: (0, i)),
            pl.BlockSpec((tm, w), lambda i: (i, 0)),
        ],
        out_shape=[
            jax.ShapeDtypeStruct((s, w), BF16),
            jax.ShapeDtypeStruct((w, s), BF16),
            jax.ShapeDtypeStruct((s, w), BF16),
        ],
        scratch_shapes=[pltpu.VMEM((tm + 2 * SUBLANE, w), F32)],
        compiler_params=_params("parallel"),
        name="mlstm_prep",
    )(proj, proj, proj, proj, proj, proj, proj, conv_w[:, :w], conv_w[:, w:])


def _log_sigmoid(x):
    return jnp.minimum(x, 0.0) - jnp.log1p(jnp.exp(-jnp.abs(x)))


def _mlstm_direction(q_ref, kt_ref, v_ref, g_ref, bias_ref, o_ref, s_ref, m_ref, *, backward):
    L = ML_CHUNK
    d = ML_HEAD_DIM
    row = lax.broadcasted_iota(jnp.int32, (L, L), 0)
    col = lax.broadcasted_iota(jnp.int32, (L, L), 1)
    keep = (col >= row) if backward else (col <= row)
    tri = keep.astype(BF16)
    ones = jnp.ones((L, LANE), BF16)

    gates = g_ref[...] + bias_ref[...]
    logf = _log_sigmoid(gates)
    hi = logf.astype(BF16)
    lo = (logf - hi.astype(F32)).astype(BF16)
    bcol = jnp.dot(tri, hi, preferred_element_type=F32) + jnp.dot(tri, lo, preferred_element_type=F32)
    gates_t = gates.T
    b_t = bcol.T
    base = 2 * ML_HEADS if backward else 0
    for h in range(ML_HEADS):
        idx = base // 2 + h
        ci, cf = base + h, base + ML_HEADS + h
        i_row = gates_t[ci:ci + 1, :]
        b_row = b_t[cf:cf + 1, :]
        b_col = bcol[:, cf:cf + 1]
        g_tot = b_row[:, 0:1] if backward else b_row[:, L - 1:L]
        a_row = g_tot - b_row + i_row
        a_max = jnp.max(a_row, axis=1, keepdims=True)
        w_row = jnp.exp(a_row - a_max)

        q = q_ref[:, h * d:(h + 1) * d]
        kt = kt_ref[h * d:(h + 1) * d, :]
        v_aug = jnp.concatenate([v_ref[:, h * d:(h + 1) * d], ones], axis=1)
        s_prev = s_ref[idx]
        m_prev = m_ref[idx][:, 0:1]

        dlog = jnp.where(keep, b_col + (i_row - b_row), NEG_INF)
        inter_log = b_col + m_prev
        m_j = jnp.maximum(inter_log, jnp.max(dlog, axis=1, keepdims=True))
        sqk = jnp.dot(q, kt, preferred_element_type=F32) * jnp.exp(dlog - m_j)
        inter_w = jnp.exp(inter_log - m_j)
        tot = (inter_w * jnp.dot(q, s_prev.astype(BF16), preferred_element_type=F32)
               + jnp.dot(sqk.astype(BF16), v_aug, preferred_element_type=F32))
        den = tot[:, d:]
        inv = 1.0 / jnp.maximum(jnp.abs(den), jnp.exp(-m_j))
        o_ref[:, h * d:(h + 1) * d] = tot[:, :d] * jnp.concatenate([inv] * (d // LANE), axis=1)

        ktw = (kt.astype(F32) * w_row).astype(BF16)
        upd = jnp.dot(ktw, v_aug, preferred_element_type=F32)
        m_new = jnp.maximum(g_tot + m_prev, a_max)
        s_ref[idx] = jnp.exp(g_tot + m_prev - m_new) * s_prev + jnp.exp(a_max - m_new) * upd
        m_ref[idx] = jnp.broadcast_to(m_new, (1, LANE))


def _mlstm_body(qf_ref, ktf_ref, vf_ref, gf_ref, qb_ref, ktb_ref, vb_ref, gb_ref, bias_ref,
                hf_ref, hb_ref, s_ref, m_ref):
    @pl.when(pl.program_id(0) == 0)
    def _():
        s_ref[...] = jnp.zeros(s_ref.shape, F32)
        m_ref[...] = jnp.zeros(m_ref.shape, F32)

    _mlstm_direction(qf_ref, ktf_ref, vf_ref, gf_ref, bias_ref, hf_ref, s_ref, m_ref, backward=False)
    _mlstm_direction(qb_ref, ktb_ref, vb_ref, gb_ref, bias_ref, hb_ref, s_ref, m_ref, backward=True)


def _mlstm_scan(mq, mkt, mv, gates, b_gate):
    s, w = mq.shape
    L = ML_CHUNK
    nc = s // L
    d = ML_HEAD_DIM
    bias = jnp.pad(b_gate, (0, LANE - b_gate.shape[0])).reshape(1, LANE)
    fwd = lambda c: c
    bwd = lambda c: nc - 1 - c
    rows = lambda ix, n: pl.BlockSpec((L, n), lambda c: (ix(c), 0))
    cols = lambda ix: pl.BlockSpec((w, L), lambda c: (0, ix(c)))
    return pl.pallas_call(
        _mlstm_body,
        grid=(nc,),
        in_specs=[rows(fwd, w), cols(fwd), rows(fwd, w), rows(fwd, LANE),
                  rows(bwd, w), cols(bwd), rows(bwd, w), rows(bwd, LANE),
                  pl.BlockSpec((1, LANE), lambda c: (0, 0))],
        out_specs=[rows(fwd, w), rows(bwd, w)],
        out_shape=[jax.ShapeDtypeStruct((s, w), F32)] * 2,
        scratch_shapes=[pltpu.VMEM((2 * ML_HEADS, d, d + LANE), F32), pltpu.VMEM((2 * ML_HEADS, 1, LANE), F32)],
        compiler_params=_params("arbitrary"),
        name="mlstm_scan",
    )(mq, mkt, mv, gates, mq, mkt, mv, gates, bias)


def _mix_out_body(ya_ref, hf_ref, hb_ref, mo_ref, mg_ref, wa_ref, wm_ref, r_ref, o_ref, yml_ref):
    @pl.when(pl.program_id(1) == 0)
    def _():
        d = ML_HEAD_DIM
        for h in range(ML_HEADS):
            sl = slice(h * d, (h + 1) * d)
            hm = _rms(hf_ref[:, sl] + hb_ref[:, sl], mg_ref[:, sl])
            yml_ref[:, sl] = (hm * jax.nn.sigmoid(mo_ref[:, sl])).astype(BF16)

    o_ref[...] = (r_ref[...] + jnp.dot(ya_ref[...], wa_ref[...], preferred_element_type=F32)
                  + jnp.dot(yml_ref[...], wm_ref[...], preferred_element_type=F32))


def _mix_out(y_att, hf, hb, proj, ml_gain, w_att, w_ml, res, *, tm=512, tn=512):
    s, n = res.shape
    wa, wm = w_att.shape[0], w_ml.shape[0]
    return pl.pallas_call(
        _mix_out_body,
        grid=(s // tm, n // tn),
        in_specs=[
            pl.BlockSpec((tm, wa), lambda i, j: (i, 0)),
            pl.BlockSpec((tm, wm), lambda i, j: (i, 0)),
            pl.BlockSpec((tm, wm), lambda i, j: (i, 0)),
            pl.BlockSpec((tm, wm), lambda i, j: (i, COL_MO)),
            pl.BlockSpec((1, wm), lambda i, j: (0, 0)),
            pl.BlockSpec((wa, tn), lambda i, j: (0, j)),
            pl.BlockSpec((wm, tn), lambda i, j: (0, j)),
            pl.BlockSpec((tm, tn), lambda i, j: (i, j)),
        ],
        out_specs=pl.BlockSpec((tm, tn), lambda i, j: (i, j)),
        out_shape=jax.ShapeDtypeStruct((s, n), F32),
        scratch_shapes=[pltpu.VMEM((tm, wm), BF16)],
        compiler_params=_params("parallel", "arbitrary"),
        name="mix_out",
    )(y_att, hf, hb, proj, ml_gain.reshape(1, wm), w_att, w_ml, res)


def kernel(x, norm_mix_g, norm_ffn_g, w_in, b_gate, conv_w, q_gain, k_gain, ml_gain, w_out, pool_w,
           layer_scale, peer_wq, peer_keys, peer_u, peer_v, norm_f):
    b, s, d = x.shape
    h = x.reshape(b * s, d)

    wi = w_in[0]
    kv_end = ATT_Q + 2 * ATT_KV
    w_main = jnp.concatenate([wi[:, :ATT_Q], wi[:, kv_end:MAIN_W], wi[:, ATT_Q:kv_end]], axis=1).astype(BF16)
    w_gate = jnp.pad(wi[:, MAIN_W:], ((0, 0), (0, LANE - (wi.shape[1] - MAIN_W)))).astype(BF16)
    proj, gates = _norm_matmul(h, norm_mix_g[0], w_main, w_gate)
    qt, k, vt = _attn_prep(proj, q_gain[0], k_gain[0])
    y_att = _attention(qt, k, vt)
    mq, mkt, mv = _mlstm_prep(proj, conv_w[0])
    hf, hb = _mlstm_scan(mq, mkt, mv, gates, b_gate[0])
    wo = w_out[0].astype(BF16)
    h = _mix_out(y_att, hf, hb, proj, ml_gain[0], wo[:ATT_Q], wo[ATT_Q:], h)
    h = _peer_ffn(h, norm_ffn_g[0], peer_wq[0], peer_keys[0], peer_u[0], peer_v[0])

    h = _pool_mixer(h, norm_mix_g[1], pool_w[0], layer_scale[0])
    h = _peer_ffn(h, norm_ffn_g[1], peer_wq[1], peer_keys[1], peer_u[1], peer_v[1], final_g=norm_f)
    return h.reshape(b, s, d)
```

```python
import functools

import jax
import jax.numpy as jnp
from jax import lax
from jax.experimental import pallas as pl
from jax.experimental.pallas import tpu as pltpu

F32 = jnp.float32
BF16 = jnp.bfloat16
EPS = 1e-6
LOG2E = 1.4426950408889634

GRID_W = 64
ROPE_THETA = 10000.0
ATT_HEADS = 8
ATT_KV_HEADS = 2
ATT_HEAD_DIM = 128
ML_HEADS = 4
ML_HEAD_DIM = 256
ML_CHUNK = 128
CONV_W = 5
POOL_WINDOWS = (2, 4, 8, 16)
PEER_HEADS = 8
PEER_NKEYS = 128
PEER_TOPK = 16
PEER_BLOCK = 128

ATT_Q = ATT_HEADS * ATT_HEAD_DIM
ATT_KV = ATT_KV_HEADS * ATT_HEAD_DIM
ML_W = ML_HEADS * ML_HEAD_DIM
MAIN_W = ATT_Q + 2 * ATT_KV + 4 * ML_W
assert ATT_Q == ML_W
COL_AQ, COL_MQ, COL_MK, COL_MV, COL_MO = 0, 1, 2, 3, 4
COL_AK = (ATT_Q + 4 * ML_W) // ATT_KV
COL_AV = COL_AK + 1
LANE = 128
SUBLANE = 8
VMEM_LIMIT = 48 * 1024 * 1024


def _params(*sem):
    return pltpu.CompilerParams(dimension_semantics=sem, vmem_limit_bytes=VMEM_LIMIT)


def _rms(x, g):
    return x * lax.rsqrt(jnp.mean(x * x, axis=-1, keepdims=True) + EPS) * g


def _norm_mm_body(x_ref, g_ref, w_ref, wg_ref, o_ref, og_ref, xn_ref):
    @pl.when(pl.program_id(1) == 0)
    def _():
        xn = _rms(x_ref[...], g_ref[...]).astype(BF16)
        xn_ref[...] = xn
        og_ref[...] = jnp.dot(xn, wg_ref[...], preferred_element_type=F32)

    o_ref[...] = jnp.dot(xn_ref[...], w_ref[...], preferred_element_type=F32)


def _norm_matmul(x, g, w, wg, *, tm=512, tn=512):
    s, d = x.shape
    n = w.shape[1]
    return pl.pallas_call(
        _norm_mm_body,
        grid=(s // tm, n // tn),
        in_specs=[
            pl.BlockSpec((tm, d), lambda i, j: (i, 0)),
            pl.BlockSpec((1, d), lambda i, j: (0, 0)),
            pl.BlockSpec((d, tn), lambda i, j: (0, j)),
            pl.BlockSpec((d, LANE), lambda i, j: (0, 0)),
        ],
        out_specs=[
            pl.BlockSpec((tm, tn), lambda i, j: (i, j)),
            pl.BlockSpec((tm, LANE), lambda i, j: (i, 0)),
        ],
        out_shape=[jax.ShapeDtypeStruct((s, n), F32), jax.ShapeDtypeStruct((s, LANE), F32)],
        scratch_shapes=[pltpu.VMEM((tm, d), BF16)],
        compiler_params=_params("parallel", "arbitrary"),
        name="norm_matmul",
    )(x, g.reshape(1, d), w, wg)


def _rope_tables(s):
    rows = s // GRID_W
    r, c = jnp.meshgrid(jnp.arange(rows), jnp.arange(GRID_W), indexing="ij")
    pos = jnp.stack([r.reshape(-1), c.reshape(-1)], axis=-1).astype(F32)
    nf = ATT_HEAD_DIM // 4
    inv = ROPE_THETA ** (-jnp.arange(nf, dtype=F32) / nf)
    ang = pos[:, :, None] * inv
    cos, sin = jnp.cos(ang), jnp.sin(ang)
    ctab = jnp.stack([cos, cos], axis=2).reshape(s, ATT_HEAD_DIM)
    stab = jnp.stack([-sin, sin], axis=2).reshape(s, ATT_HEAD_DIM)
    return ctab, stab


def _rope(y, ctab, stab, first_half):
    nf = ATT_HEAD_DIM // 4
    swapped = jnp.where(first_half, pltpu.roll(y, ATT_HEAD_DIM - nf, 1), pltpu.roll(y, nf, 1))
    return y * ctab + swapped * stab


def _attn_prep_body(aq_ref, ak_ref, av_ref, c_ref, s_ref, qg_ref, kg_ref, qt_ref, k_ref, vt_ref):
    d = ATT_HEAD_DIM
    ctab, stab = c_ref[...], s_ref[...]
    lane = lax.broadcasted_iota(jnp.int32, ctab.shape, 1)
    first_half = (lane % (d // 2)) < (d // 4)
    scale = d ** -0.5 * LOG2E
    for h in range(ATT_HEADS):
        y = _rms(aq_ref[:, h * d:(h + 1) * d], qg_ref[...])
        qt_ref[h] = (_rope(y, ctab, stab, first_half) * scale).T.astype(BF16)
    for h in range(ATT_KV_HEADS):
        y = _rms(ak_ref[:, h * d:(h + 1) * d], kg_ref[...])
        k_ref[h] = _rope(y, ctab, stab, first_half).astype(BF16)
        vt_ref[h] = av_ref[:, h * d:(h + 1) * d].T.astype(BF16)


def _attn_prep(proj, q_gain, k_gain, *, tm=512):
    s = proj.shape[0]
    d = ATT_HEAD_DIM
    ctab, stab = _rope_tables(s)
    return pl.pallas_call(
        _attn_prep_body,
        grid=(s // tm,),
        in_specs=[
            pl.BlockSpec((tm, ATT_Q), lambda i: (i, COL_AQ)),
            pl.BlockSpec((tm, ATT_KV), lambda i: (i, COL_AK)),
            pl.BlockSpec((tm, ATT_KV), lambda i: (i, COL_AV)),
            pl.BlockSpec((tm, d), lambda i: (i, 0)),
            pl.BlockSpec((tm, d), lambda i: (i, 0)),
            pl.BlockSpec((1, d), lambda i: (0, 0)),
            pl.BlockSpec((1, d), lambda i: (0, 0)),
        ],
        out_specs=[
            pl.BlockSpec((ATT_HEADS, d, tm), lambda i: (0, 0, i)),
            pl.BlockSpec((ATT_KV_HEADS, tm, d), lambda i: (0, i, 0)),
            pl.BlockSpec((ATT_KV_HEADS, d, tm), lambda i: (0, 0, i)),
        ],
        out_shape=[
            jax.ShapeDtypeStruct((ATT_HEADS, d, s), BF16),
            jax.ShapeDtypeStruct((ATT_KV_HEADS, s, d), BF16),
            jax.ShapeDtypeStruct((ATT_KV_HEADS, d, s), BF16),
        ],
        compiler_params=_params("parallel"),
        name="attn_prep",
    )(proj, proj, proj, ctab, stab, q_gain.reshape(1, d), k_gain.reshape(1, d))


def _col_reduce(x, pair, final):
    fold = 8 * SUBLANE
    if x.shape[0] > fold:
        acc = x[:fold]
        for r in range(fold, x.shape[0], fold):
            acc = pair(acc, x[r:r + fold])
        x = acc
    while x.shape[0] > SUBLANE:
        half = x.shape[0] // 2
        x = pair(x[:half], x[half:])
    return final(x, axis=0, keepdims=True)


def _attn_body(qt_ref, k_ref, vt_ref, o_ref, q_all, m_ref, l_ref, acc_ref, s_ref, p_ref, *, tq, grp, wide):
    j = pl.program_id(2)
    d = ATT_HEAD_DIM
    mq = grp * tq
    chunks = mq // LANE

    @pl.when(j == 0)
    def _():
        for h in range(grp):
            for t in range(0, tq, wide):
                q_all[(h * tq + t) // wide] = qt_ref[h, :, t:t + wide]
        m_ref[...] = jnp.full(m_ref.shape, -jnp.inf, F32)
        l_ref[...] = jnp.zeros(l_ref.shape, F32)
        acc_ref[...] = jnp.zeros(acc_ref.shape, F32)

    k = k_ref[0]
    vt = vt_ref[0]
    per = wide // LANE
    groups = mq // wide

    for g in range(groups):
        st = jnp.dot(k, q_all[g], preferred_element_type=F32)
        for c in range(per):
            s_ref[g * per + c] = st[:, c * LANE:(c + 1) * LANE]
    alphas = []
    for ci in range(chunks):
        sc = s_ref[ci]
        m_prev = m_ref[ci]
        m_new = jnp.maximum(m_prev, _col_reduce(sc, jnp.maximum, jnp.max))
        alpha = jnp.exp2(m_prev - m_new)
        p = jnp.exp2(sc - m_new)
        l_ref[ci] = alpha * l_ref[ci] + _col_reduce(p, jnp.add, jnp.sum)
        m_ref[ci] = m_new
        p_ref[ci] = p.astype(BF16)
        alphas.append(alpha)
    pt = jnp.concatenate([p_ref[ci] for ci in range(chunks)], axis=1)
    pv = jnp.dot(vt, pt, preferred_element_type=F32)
    for ci in range(chunks):
        acc_ref[ci] = alphas[ci] * acc_ref[ci] + pv[:, ci * LANE:(ci + 1) * LANE]

    @pl.when(j == pl.num_programs(2) - 1)
    def _():
        per_head = tq // LANE
        for c in range(chunks):
            h, tb = divmod(c, per_head)
            out_t = acc_ref[c] / l_ref[c]
            o_ref[tb * LANE:(tb + 1) * LANE, h * d:(h + 1) * d] = out_t.T.astype(o_ref.dtype)


def _attention(qt, k, vt, *, tq=512, tk=1024, wide=2 * LANE):
    _, s, d = k.shape
    grp = ATT_HEADS // ATT_KV_HEADS
    mq = grp * tq
    return pl.pallas_call(
        functools.partial(_attn_body, tq=tq, grp=grp, wide=wide),
        grid=(ATT_KV_HEADS, s // tq, s // tk),
        in_specs=[
            pl.BlockSpec((grp, d, tq), lambda h, i, j: (h, 0, i)),
            pl.BlockSpec((1, tk, d), lambda h, i, j: (h, j, 0)),
            pl.BlockSpec((1, d, tk), lambda h, i, j: (h, 0, j)),
        ],
        out_specs=pl.BlockSpec((tq, grp * d), lambda h, i, j: (i, h)),
        out_shape=jax.ShapeDtypeStruct((s, ATT_HEADS * d), BF16),
        scratch_shapes=[
            pltpu.VMEM((mq // wide, d, wide), BF16),
            pltpu.VMEM((mq // LANE, 1, LANE), F32),
            pltpu.VMEM((mq // LANE, 1, LANE), F32),
            pltpu.VMEM((mq // LANE, d, LANE), F32),
            pltpu.VMEM((mq // LANE, tk, LANE), F32),
            pltpu.VMEM((mq // LANE, tk, LANE), BF16),
        ],
        compiler_params=_params("parallel", "parallel", "arbitrary"),
        name="attention",
    )(qt, k, vt)


def _pool_body(prev_ref, x_ref, next_ref, g_ref, w_ref, ls_ref, o_ref, xc_ref, *, tm, s_total):
    i = pl.program_id(0)
    halo = SUBLANE
    g = g_ref[...]
    xm = x_ref[...]
    xc_ref[halo:halo + tm, :] = _rms(xm, g)
    xc_ref[0:halo, :] = jnp.where(i > 0, _rms(prev_ref[...], g), 0.0)
    xc_ref[halo + tm:2 * halo + tm, :] = jnp.where(i < pl.num_programs(0) - 1, _rms(next_ref[...], g), 0.0)
    row = i * tm + lax.broadcasted_iota(jnp.int32, (tm, 1), 0)
    p = xm.shape[1] // len(POOL_WINDOWS)
    outs = []
    for gi, window in enumerate(POOL_WINDOWS):
        hf = window // 2
        cols = slice(gi * p, (gi + 1) * p)
        acc = xc_ref[halo - hf:halo - hf + tm, cols]
        for o in range(-hf + 1, hf):
            acc = acc + xc_ref[halo + o:halo + o + tm, cols]
        cnt = (jnp.minimum(row + hf, s_total) - jnp.maximum(row - hf, 0)).astype(F32)
        pooled = acc / cnt - xc_ref[halo:halo + tm, cols]
        outs.append(jnp.dot(pooled.astype(BF16), w_ref[gi], preferred_element_type=F32))
    o_ref[...] = xm + jnp.concatenate(outs, axis=-1) * ls_ref[...]


def _pool_mixer(h, g, pool_w, layer_scale, *, tm=256):
    s, d = h.shape
    nb = tm // SUBLANE
    ng, p, _ = pool_w.shape
    return pl.pallas_call(
        functools.partial(_pool_body, tm=tm, s_total=s),
        grid=(s // tm,),
        in_specs=[
            pl.BlockSpec((SUBLANE, d), lambda i: (jnp.maximum(i * nb - 1, 0), 0)),
            pl.BlockSpec((tm, d), lambda i: (i, 0)),
            pl.BlockSpec((SUBLANE, d), lambda i: (jnp.minimum((i + 1) * nb, s // SUBLANE - 1), 0)),
            pl.BlockSpec((1, d), lambda i: (0, 0)),
            pl.BlockSpec((ng, p, p), lambda i: (0, 0, 0)),
            pl.BlockSpec((1, d), lambda i: (0, 0)),
        ],
        out_specs=pl.BlockSpec((tm, d), lambda i: (i, 0)),
        out_shape=jax.ShapeDtypeStruct((s, d), F32),
        scratch_shapes=[pltpu.VMEM((tm + 2 * SUBLANE, d), F32)],
        compiler_params=_params("parallel"),
        name="pool_mixer",
    )(h, h, h, g.reshape(1, d), pool_w.astype(BF16), layer_scale.reshape(1, d))


NEG_INF = float("-inf")


def _top_values(x, count):
    rows = []
    for _ in range(count):
        m = jnp.max(x, axis=0, keepdims=True)
        rows.append(m)
        x = jnp.where(x == m, NEG_INF, x)
    return rows


def _stack_rows(rows, t, n=None):
    n = len(rows) if n is None else n
    rid = lax.broadcasted_iota(jnp.int32, (n, t), 0)
    out = jnp.full((n, t), NEG_INF, F32)
    for r, row in enumerate(rows):
        out = jnp.where(rid == r, row, out)
    return out


def _route_head(s1, s2):
    t = s1.shape[1]
    kk = PEER_TOPK
    v1 = _top_values(s1, kk + 1)
    v2 = _top_values(s2, kk + 1)
    v2_all = _stack_rows(v2[:kk], t)
    pieces = []
    for a in range(kk // 2):
        nb = kk // (a + 1)
        rows = kk if nb > SUBLANE else SUBLANE
        rid = lax.broadcasted_iota(jnp.int32, (rows, t), 0)
        pieces.append(jnp.where(rid < nb, v1[a] + v2_all[:rows], NEG_INF))
    tail = [v1[a] + v2[0] for a in range(kk // 2, kk + 1)] + [v1[0] + v2[kk]]
    pieces.append(_stack_rows(tail, t, kk))
    cand = jnp.concatenate(pieces, axis=0)
    best = _top_values(cand, kk + 1)
    tau = 0.5 * (best[kk - 1] + best[kk])
    z = jnp.ones_like(tau)
    for r in range(1, kk):
        z = z + jnp.exp(best[r] - best[0])
    thr = jnp.exp(tau - v2[0] - s1) / z
    e1 = jnp.exp(s1 - v1[0])
    e2 = jnp.exp(s2 - v2[0]) / z
    return thr, e1, e2


def _peer_route_body(h_ref, g_ref, wq_ref, keys_ref, xnt_ref, thr_ref, e1_ref, e2_ref):
    nk = PEER_NKEYS
    xnt = _rms(h_ref[...], g_ref[...]).T.astype(BF16)
    xnt_ref[...] = xnt
    qt = jnp.dot(wq_ref[...], xnt, preferred_element_type=F32)
    for hd in range(PEER_HEADS):
        q1 = qt[(2 * hd) * nk:(2 * hd + 1) * nk].astype(BF16)
        q2 = qt[(2 * hd + 1) * nk:(2 * hd + 2) * nk].astype(BF16)
        s1 = jnp.dot(keys_ref[2 * hd], q1, preferred_element_type=F32)
        s2 = jnp.dot(keys_ref[2 * hd + 1], q2, preferred_element_type=F32)
        thr, e1, e2 = _route_head(s1, s2)
        thr_ref[hd] = thr
        e1_ref[hd] = e1
        for c in range(e2.shape[1] // LANE):
            e2_ref[hd, c] = e2[:, c * LANE:(c + 1) * LANE]


def _peer_route(h, g, wq_t, keys, *, tt=256):
    s, d = h.shape
    nq = wq_t.shape[0]
    hk = jax.ShapeDtypeStruct((PEER_HEADS, PEER_NKEYS, s), F32)
    hck = jax.ShapeDtypeStruct((PEER_HEADS, s // LANE, PEER_NKEYS, LANE), F32)
    gate_spec = pl.BlockSpec((PEER_HEADS, PEER_NKEYS, tt), lambda i: (0, 0, i))
    chunk_spec = pl.BlockSpec((PEER_HEADS, tt // LANE, PEER_NKEYS, LANE), lambda i: (0, i, 0, 0))
    return pl.pallas_call(
        _peer_route_body,
        grid=(s // tt,),
        in_specs=[
            pl.BlockSpec((tt, d), lambda i: (i, 0)),
            pl.BlockSpec((1, d), lambda i: (0, 0)),
            pl.BlockSpec((nq, d), lambda i: (0, 0)),
            pl.BlockSpec(keys.shape, lambda i: (0, 0, 0)),
        ],
        out_specs=[pl.BlockSpec((d, tt), lambda i: (0, i)), gate_spec, gate_spec, chunk_spec],
        out_shape=[jax.ShapeDtypeStruct((d, s), BF16), hk, hk, hck],
        compiler_params=_params("parallel"),
        name="peer_route",
    )(h, g.reshape(1, d), wq_t, keys)


def _gelu(x):
    return 0.5 * x * (1.0 + lax.erf(x * (2.0 ** -0.5)))


def _peer_dense_body(xnt_ref, u0_ref, u1_ref, vt0_ref, vt1_ref, thr0_ref, thr1_ref, e10_ref, e11_ref,
                     e20_ref, e21_ref, o_ref, a0_ref, a1_ref, p0_ref, p1_ref, *, tt, eb, nb):
    i = pl.program_id(0)
    nk = PEER_NKEYS
    rb = 4 * SUBLANE
    chunks = tt // LANE

    @pl.when(i == 0)
    def _():
        a1_ref[...] = jnp.zeros(a1_ref.shape, F32)
        p0_ref[...] = jnp.zeros(p0_ref.shape, BF16)

    first_pair = jnp.maximum(2 * i - 2, 0)

    @pl.when(first_pair % nb == 0)
    def _():
        o_ref[...] = jnp.zeros(o_ref.shape, F32)

    def activations(u_ref, a_ref, cs):
        lanes = slice(cs[0] * LANE, (cs[-1] + 1) * LANE)
        a = jnp.dot(u_ref[...], xnt_ref[:, lanes], preferred_element_type=F32)
        for n, c in enumerate(cs):
            a_ref[c] = a[:, n * LANE:(n + 1) * LANE]

    def gated(thr_ref, e1_ref, e2_ref, a_ref, p_ref, cs):
        ks = range(eb // nk)
        for c in cs:
            lanes = slice(c * LANE, (c + 1) * LANE)
            for r in range(0, nk, rb):
                gates = {k: jnp.zeros((rb, LANE), F32) for k in ks}
                for hd in range(PEER_HEADS):
                    e2 = e2_ref[hd, c, r:r + rb, :]
                    for k in ks:
                        thr = thr_ref[hd, 0, k:k + 1, lanes]
                        e1 = e1_ref[hd, 0, k:k + 1, lanes]
                        gates[k] = gates[k] + jnp.where(e2 >= thr, e2 * e1, 0.0)
                for k in ks:
                    blk = slice(k * nk + r, k * nk + r + rb)
                    p_ref[c, blk, :] = (gates[k] * _gelu(a_ref[c, blk, :])).astype(BF16)

    def project(vt_ref, p_ref, cs):
        lanes = slice(cs[0] * LANE, (cs[-1] + 1) * LANE)
        p = jnp.concatenate([p_ref[c] for c in cs], axis=1)
        o_ref[:, lanes] += jnp.dot(vt_ref[...], p, preferred_element_type=F32)

    def half_step(u_ref, a_new, thr_ref, e1_ref, e2_ref, a_old, p_new, vt_ref, p_old):
        half = chunks // 2
        for cs in (range(half), range(half, chunks)):
            activations(u_ref, a_new, cs)
            gated(thr_ref, e1_ref, e2_ref, a_old, p_new, cs[:len(cs) // 2])
            project(vt_ref, p_old, cs)
            gated(thr_ref, e1_ref, e2_ref, a_old, p_new, cs[len(cs) // 2:])

    half_step(u0_ref, a0_ref, thr0_ref, e10_ref, e20_ref, a1_ref, p1_ref, vt0_ref, p0_ref)
    half_step(u1_ref, a1_ref, thr1_ref, e11_ref, e21_ref, a0_ref, p0_ref, vt1_ref, p1_ref)


def _peer_dense(xnt, u, vt, thr, e1, e2, *, tt=512, eb=512):
    d, s = xnt.shape
    ne = u.shape[0]
    nk = PEER_NKEYS
    rows = eb // nk
    nb = ne // eb
    n = (s // tt) * nb
    assert nb % 2 == 0
    split = lambda a: a.reshape(PEER_HEADS, nb, rows, s)

    def pair(lag, half):
        return lambda i: divmod(jnp.clip(2 * i + half - lag, 0, n - 1), nb)

    def u_spec(half):
        return pl.BlockSpec((eb, d), lambda i: (pair(0, half)(i)[1], 0))

    def vt_spec(half):
        return pl.BlockSpec((d, eb), lambda i: (0, pair(2, half)(i)[1]))

    def row_spec(half):
        return pl.BlockSpec((PEER_HEADS, 1, rows, tt), lambda i: (0, pair(1, half)(i)[1], 0, pair(1, half)(i)[0]))

    def e2_spec(half):
        return pl.BlockSpec((PEER_HEADS, tt // LANE, nk, LANE), lambda i: (0, pair(1, half)(i)[0], 0, 0))

    slab = lambda dtype: pltpu.VMEM((tt // LANE, eb, LANE), dtype)
    return pl.pallas_call(
        functools.partial(_peer_dense_body, tt=tt, eb=eb, nb=nb),
        grid=(n // 2 + 1,),
        in_specs=[
            pl.BlockSpec((d, tt), lambda i: (0, pair(0, 0)(i)[0])),
            u_spec(0), u_spec(1), vt_spec(0), vt_spec(1),
            row_spec(0), row_spec(1), row_spec(0), row_spec(1), e2_spec(0), e2_spec(1),
        ],
        out_specs=pl.BlockSpec((d, tt), lambda i: (0, pair(2, 0)(i)[0])),
        out_shape=jax.ShapeDtypeStruct((d, s), F32),
        scratch_shapes=[slab(F32), slab(F32), slab(BF16), slab(BF16)],
        compiler_params=_params("arbitrary"),
        name="peer_dense",
    )(xnt, u, u, vt, vt, split(thr), split(thr), split(e1), split(e1), e2, e2)


def _add_t_body(h_ref, pt_ref, g_ref, o_ref, *, final_norm):
    y = h_ref[...] + pt_ref[...].T
    o_ref[...] = _rms(y, g_ref[...]) if final_norm else y


def _add_transposed(h, pt, g=None, *, tm=512):
    s, d = h.shape
    gain = jnp.ones((1, d), F32) if g is None else g.reshape(1, d)
    return pl.pallas_call(
        functools.partial(_add_t_body, final_norm=g is not None),
        grid=(s // tm,),
        in_specs=[
            pl.BlockSpec((tm, d), lambda i: (i, 0)),
            pl.BlockSpec((d, tm), lambda i: (0, i)),
            pl.BlockSpec((1, d), lambda i: (0, 0)),
        ],
        out_specs=pl.BlockSpec((tm, d), lambda i: (i, 0)),
        out_shape=jax.ShapeDtypeStruct((s, d), F32),
        compiler_params=_params("parallel"),
        name="add_transposed",
    )(h, pt, gain)


def _peer_ffn(h, g, w_query, sub_keys, u, v, final_g=None):
    nk = PEER_NKEYS
    keys = sub_keys.reshape(2 * PEER_HEADS, nk, -1).astype(BF16)
    xnt, thr, e1, e2 = _peer_route(h, g, w_query.T.astype(BF16), keys)
    pt = _peer_dense(xnt, u.astype(BF16), v.T.astype(BF16), thr, e1, e2)
    return _add_transposed(h, pt, final_g)


def _mlstm_prep_body(qp_ref, q_ref, qn_ref, kp_ref, k_ref, kn_ref, v_ref, cwq_ref, cwk_ref,
                     qo_ref, kto_ref, vo_ref, xc_ref, *, tm):
    i = pl.program_id(0)
    last = pl.num_programs(0) - 1
    halo = SUBLANE
    pad = CONV_W // 2

    def conv_silu(prev_ref, x_ref, next_ref, w_ref):
        xc_ref[halo:halo + tm, :] = x_ref[...]
        xc_ref[0:halo, :] = jnp.where(i > 0, prev_ref[...], 0.0)
        xc_ref[halo + tm:2 * halo + tm, :] = jnp.where(i < last, next_ref[...], 0.0)
        acc = xc_ref[halo - pad:halo - pad + tm, :] * w_ref[0:1, :]
        for j in range(1, CONV_W):
            acc = acc + xc_ref[halo - pad + j:halo - pad + j + tm, :] * w_ref[j:j + 1, :]
        return acc * jax.nn.sigmoid(acc)

    qo_ref[...] = conv_silu(qp_ref, q_ref, qn_ref, cwq_ref).astype(BF16)
    kc = conv_silu(kp_ref, k_ref, kn_ref, cwk_ref) * (ML_HEAD_DIM ** -0.5)
    kto_ref[...] = kc.T.astype(BF16)
    vo_ref[...] = v_ref[...].astype(BF16)


def _mlstm_prep(proj, conv_w, *, tm=256):
    s = proj.shape[0]
    w = ML_W
    nb = tm // SUBLANE
    prev = lambda c: pl.BlockSpec((SUBLANE, w), lambda i: (jnp.maximum(i * nb - 1, 0), c))
    main = lambda c: pl.BlockSpec((tm, w), lambda i: (i, c))
    nxt = lambda c: pl.BlockSpec((SUBLANE, w), lambda i: (jnp.minimum((i + 1) * nb, s // SUBLANE - 1), c))
    cw = pl.BlockSpec((CONV_W, w), lambda i: (0, 0))
    return pl.pallas_call(
        functools.partial(_mlstm_prep_body, tm=tm),
        grid=(s // tm,),
        in_specs=[prev(COL_MQ), main(COL_MQ), nxt(COL_MQ), prev(COL_MK), main(COL_MK), nxt(COL_MK),
                  main(COL_MV), cw, cw],
        out_specs=[
            pl.BlockSpec((tm, w), lambda i: (i, 0)),
            pl.BlockSpec((w, tm), lambda i: (0, i)),
            pl.BlockSpec((tm, w), lambda i: (i, 0)),
        ],
        out_shape=[
            jax.ShapeDtypeStruct((s, w), BF16),
            jax.ShapeDtypeStruct((w, s), BF16),
            jax.ShapeDtypeStruct((s, w), BF16),
        ],
        scratch_shapes=[pltpu.VMEM((tm + 2 * SUBLANE, w), F32)],
        compiler_params=_params("parallel"),
        name="mlstm_prep",
    )(proj, proj, proj, proj, proj, proj, proj, conv_w[:, :w], conv_w[:, w:])


def _log_sigmoid(x):
    return jnp.minimum(x, 0.0) - jnp.log1p(jnp.exp(-jnp.abs(x)))


def _mlstm_direction(q_ref, kt_ref, v_ref, g_ref, bias_ref, o_ref, s_ref, m_ref, *, backward):
    L = ML_CHUNK
    d = ML_HEAD_DIM
    row = lax.broadcasted_iota(jnp.int32, (L, L), 0)
    col = lax.broadcasted_iota(jnp.int32, (L, L), 1)
    keep = (col >= row) if backward else (col <= row)
    tri = keep.astype(BF16)
    ones = jnp.ones((L, LANE), BF16)

    gates = g_ref[...] + bias_ref[...]
    logf = _log_sigmoid(gates)
    hi = logf.astype(BF16)
    lo = (logf - hi.astype(F32)).astype(BF16)
    bcol = jnp.dot(tri, hi, preferred_element_type=F32) + jnp.dot(tri, lo, preferred_element_type=F32)
    gates_t = gates.T
    b_t = bcol.T
    base = 2 * ML_HEADS if backward else 0
    for h in range(ML_HEADS):
        idx = base // 2 + h
        ci, cf = base + h, base + ML_HEADS + h
        i_row = gates_t[ci:ci + 1, :]
        b_row = b_t[cf:cf + 1, :]
        b_col = bcol[:, cf:cf + 1]
        g_tot = b_row[:, 0:1] if backward else b_row[:, L - 1:L]
        a_row = g_tot - b_row + i_row
        a_max = jnp.max(a_row, axis=1, keepdims=True)
        w_row = jnp.exp(a_row - a_max)

        q = q_ref[:, h * d:(h + 1) * d]
        kt = kt_ref[h * d:(h + 1) * d, :]
        v_aug = jnp.concatenate([v_ref[:, h * d:(h + 1) * d], ones], axis=1)
        s_prev = s_ref[idx]
        m_prev = m_ref[idx][:, 0:1]

        dlog = jnp.where(keep, b_col + (i_row - b_row), NEG_INF)
        inter_log = b_col + m_prev
        m_j = jnp.maximum(inter_log, jnp.max(dlog, axis=1, keepdims=True))
        sqk = jnp.dot(q, kt, preferred_element_type=F32) * jnp.exp(dlog - m_j)
        inter_w = jnp.exp(inter_log - m_j)
        tot = (inter_w * jnp.dot(q, s_prev.astype(BF16), preferred_element_type=F32)
               + jnp.dot(sqk.astype(BF16), v_aug, preferred_element_type=F32))
        den = tot[:, d:]
        inv = 1.0 / jnp.maximum(jnp.abs(den), jnp.exp(-m_j))
        o_ref[:, h * d:(h + 1) * d] = tot[:, :d] * jnp.concatenate([inv] * (d // LANE), axis=1)

        ktw = (kt.astype(F32) * w_row).astype(BF16)
        upd = jnp.dot(ktw, v_aug, preferred_element_type=F32)
        m_new = jnp.maximum(g_tot + m_prev, a_max)
        s_ref[idx] = jnp.exp(g_tot + m_prev - m_new) * s_prev + jnp.exp(a_max - m_new) * upd
        m_ref[idx] = jnp.broadcast_to(m_new, (1, LANE))


def _mlstm_body(qf_ref, ktf_ref, vf_ref, gf_ref, qb_ref, ktb_ref, vb_ref, gb_ref, bias_ref,
                hf_ref, hb_ref, s_ref, m_ref):
    @pl.when(pl.program_id(0) == 0)
    def _():
        s_ref[...] = jnp.zeros(s_ref.shape, F32)
        m_ref[...] = jnp.zeros(m_ref.shape, F32)

    _mlstm_direction(qf_ref, ktf_ref, vf_ref, gf_ref, bias_ref, hf_ref, s_ref, m_ref, backward=False)
    _mlstm_direction(qb_ref, ktb_ref, vb_ref, gb_ref, bias_ref, hb_ref, s_ref, m_ref, backward=True)


def _mlstm_scan(mq, mkt, mv, gates, b_gate):
    s, w = mq.shape
    L = ML_CHUNK
    nc = s // L
    d = ML_HEAD_DIM
    bias = jnp.pad(b_gate, (0, LANE - b_gate.shape[0])).reshape(1, LANE)
    fwd = lambda c: c
    bwd = lambda c: nc - 1 - c
    rows = lambda ix, n: pl.BlockSpec((L, n), lambda c: (ix(c), 0))
    cols = lambda ix: pl.BlockSpec((w, L), lambda c: (0, ix(c)))
    return pl.pallas_call(
        _mlstm_body,
        grid=(nc,),
        in_specs=[rows(fwd, w), cols(fwd), rows(fwd, w), rows(fwd, LANE),
                  rows(bwd, w), cols(bwd), rows(bwd, w), rows(bwd, LANE),
                  pl.BlockSpec((1, LANE), lambda c: (0, 0))],
        out_specs=[rows(fwd, w), rows(bwd, w)],
        out_shape=[jax.ShapeDtypeStruct((s, w), F32)] * 2,
        scratch_shapes=[pltpu.VMEM((2 * ML_HEADS, d, d + LANE), F32), pltpu.VMEM((2 * ML_HEADS, 1, LANE), F32)],
        compiler_params=_params("arbitrary"),
        name="mlstm_scan",
    )(mq, mkt, mv, gates, mq, mkt, mv, gates, bias)


def _mix_out_body(ya_ref, hf_ref, hb_ref, mo_ref, mg_ref, wa_ref, wm_ref, r_ref, o_ref, yml_ref):
    @pl.when(pl.program_id(1) == 0)
    def _():
        d = ML_HEAD_DIM
        for h in range(ML_HEADS):
            sl = slice(h * d, (h + 1) * d)
            hm = _rms(hf_ref[:, sl] + hb_ref[:, sl], mg_ref[:, sl])
            yml_ref[:, sl] = (hm * jax.nn.sigmoid(mo_ref[:, sl])).astype(BF16)

    o_ref[...] = (r_ref[...] + jnp.dot(ya_ref[...], wa_ref[...], preferred_element_type=F32)
                  + jnp.dot(yml_ref[...], wm_ref[...], preferred_element_type=F32))


def _mix_out(y_att, hf, hb, proj, ml_gain, w_att, w_ml, res, *, tm=512, tn=512):
    s, n = res.shape
    wa, wm = w_att.shape[0], w_ml.shape[0]
    return pl.pallas_call(
        _mix_out_body,
        grid=(s // tm, n // tn),
        in_specs=[
            pl.BlockSpec((tm, wa), lambda i, j: (i, 0)),
            pl.BlockSpec((tm, wm), lambda i, j: (i, 0)),
            pl.BlockSpec((tm, wm), lambda i, j: (i, 0)),
            pl.BlockSpec((tm, wm), lambda i, j: (i, COL_MO)),
            pl.BlockSpec((1, wm), lambda i, j: (0, 0)),
            pl.BlockSpec((wa, tn), lambda i, j: (0, j)),
            pl.BlockSpec((wm, tn), lambda i, j: (0, j)),
            pl.BlockSpec((tm, tn), lambda i, j: (i, j)),
        ],
        out_specs=pl.BlockSpec((tm, tn), lambda i, j: (i, j)),
        out_shape=jax.ShapeDtypeStruct((s, n), F32),
        scratch_shapes=[pltpu.VMEM((tm, wm), BF16)],
        compiler_params=_params("parallel", "arbitrary"),
        name="mix_out",
    )(y_att, hf, hb, proj, ml_gain.reshape(1, wm), w_att, w_ml, res)


def kernel(x, norm_mix_g, norm_ffn_g, w_in, b_gate, conv_w, q_gain, k_gain, ml_gain, w_out, pool_w,
           layer_scale, peer_wq, peer_keys, peer_u, peer_v, norm_f):
    b, s, d = x.shape
    h = x.reshape(b * s, d)

    wi = w_in[0]
    kv_end = ATT_Q + 2 * ATT_KV
    w_main = jnp.concatenate([wi[:, :ATT_Q], wi[:, kv_end:MAIN_W], wi[:, ATT_Q:kv_end]], axis=1).astype(BF16)
    w_gate = jnp.pad(wi[:, MAIN_W:], ((0, 0), (0, LANE - (wi.shape[1] - MAIN_W)))).astype(BF16)
    proj, gates = _norm_matmul(h, norm_mix_g[0], w_main, w_gate)
    qt, k, vt = _attn_prep(proj, q_gain[0], k_gain[0])
    y_att = _attention(qt, k, vt)
    mq, mkt, mv = _mlstm_prep(proj, conv_w[0])
    hf, hb = _mlstm_scan(mq, mkt, mv, gates, b_gate[0])
    wo = w_out[0].astype(BF16)
    h = _mix_out(y_att, hf, hb, proj, ml_gain[0], wo[:ATT_Q], wo[ATT_Q:], h)
    h = _peer_ffn(h, norm_ffn_g[0], peer_wq[0], peer_keys[0], peer_u[0], peer_v[0])

    h = _pool_mixer(h, norm_mix_g[1], pool_w[0], layer_scale[0])
    h = _peer_ffn(h, norm_ffn_g[1], peer_wq[1], peer_keys[1], peer_u[1], peer_v[1], final_g=norm_f)
    return h.reshape(b, s, d)
```

```python
import functools

import jax
import jax.numpy as jnp
from jax import lax
from jax.experimental import pallas as pl
from jax.experimental.pallas import tpu as pltpu

F32 = jnp.float32
BF16 = jnp.bfloat16
EPS = 1e-6
LOG2E = 1.4426950408889634

GRID_W = 64
ROPE_THETA = 10000.0
ATT_HEADS = 8
ATT_KV_HEADS = 2
ATT_HEAD_DIM = 128
ML_HEADS = 4
ML_HEAD_DIM = 256
ML_CHUNK = 128
CONV_W = 5
POOL_WINDOWS = (2, 4, 8, 16)
PEER_HEADS = 8
PEER_NKEYS = 128
PEER_TOPK = 16
PEER_BLOCK = 128

ATT_Q = ATT_HEADS * ATT_HEAD_DIM
ATT_KV = ATT_KV_HEADS * ATT_HEAD_DIM
ML_W = ML_HEADS * ML_HEAD_DIM
MAIN_W = ATT_Q + 2 * ATT_KV + 4 * ML_W
assert ATT_Q == ML_W
COL_AQ, COL_MQ, COL_MK, COL_MV, COL_MO = 0, 1, 2, 3, 4
COL_AK = (ATT_Q + 4 * ML_W) // ATT_KV
COL_AV = COL_AK + 1
LANE = 128
SUBLANE = 8
VMEM_LIMIT = 48 * 1024 * 1024


def _params(*sem):
    return pltpu.CompilerParams(dimension_semantics=sem, vmem_limit_bytes=VMEM_LIMIT)


def _rms(x, g):
    return x * lax.rsqrt(jnp.mean(x * x, axis=-1, keepdims=True) + EPS) * g


def _norm_mm_body(x_ref, g_ref, w_ref, wg_ref, o_ref, og_ref, xn_ref):
    @pl.when(pl.program_id(1) == 0)
    def _():
        xn = _rms(x_ref[...], g_ref[...]).astype(BF16)
        xn_ref[...] = xn
        og_ref[...] = jnp.dot(xn, wg_ref[...], preferred_element_type=F32)

    o_ref[...] = jnp.dot(xn_ref[...], w_ref[...], preferred_element_type=F32)


def _norm_matmul(x, g, w, wg, *, tm=512, tn=512):
    s, d = x.shape
    n = w.shape[1]
    return pl.pallas_call(
        _norm_mm_body,
        grid=(s // tm, n // tn),
        in_specs=[
            pl.BlockSpec((tm, d), lambda i, j: (i, 0)),
            pl.BlockSpec((1, d), lambda i, j: (0, 0)),
            pl.BlockSpec((d, tn), lambda i, j: (0, j)),
            pl.BlockSpec((d, LANE), lambda i, j: (0, 0)),
        ],
        out_specs=[
            pl.BlockSpec((tm, tn), lambda i, j: (i, j)),
            pl.BlockSpec((tm, LANE), lambda i, j: (i, 0)),
        ],
        out_shape=[jax.ShapeDtypeStruct((s, n), F32), jax.ShapeDtypeStruct((s, LANE), F32)],
        scratch_shapes=[pltpu.VMEM((tm, d), BF16)],
        compiler_params=_params("parallel", "arbitrary"),
        name="norm_matmul",
    )(x, g.reshape(1, d), w, wg)


def _rope_tables(s):
    rows = s // GRID_W
    r, c = jnp.meshgrid(jnp.arange(rows), jnp.arange(GRID_W), indexing="ij")
    pos = jnp.stack([r.reshape(-1), c.reshape(-1)], axis=-1).astype(F32)
    nf = ATT_HEAD_DIM // 4
    inv = ROPE_THETA ** (-jnp.arange(nf, dtype=F32) / nf)
    ang = pos[:, :, None] * inv
    cos, sin = jnp.cos(ang), jnp.sin(ang)
    ctab = jnp.stack([cos, cos], axis=2).reshape(s, ATT_HEAD_DIM)
    stab = jnp.stack([-sin, sin], axis=2).reshape(s, ATT_HEAD_DIM)
    return ctab, stab


def _rope(y, ctab, stab, first_half):
    nf = ATT_HEAD_DIM // 4
    swapped = jnp.where(first_half, pltpu.roll(y, ATT_HEAD_DIM - nf, 1), pltpu.roll(y, nf, 1))
    return y * ctab + swapped * stab


def _attn_prep_body(aq_ref, ak_ref, av_ref, c_ref, s_ref, qg_ref, kg_ref, qt_ref, k_ref, vt_ref):
    d = ATT_HEAD_DIM
    ctab, stab = c_ref[...], s_ref[...]
    lane = lax.broadcasted_iota(jnp.int32, ctab.shape, 1)
    first_half = (lane % (d // 2)) < (d // 4)
    scale = d ** -0.5 * LOG2E
    for h in range(ATT_HEADS):
        y = _rms(aq_ref[:, h * d:(h + 1) * d], qg_ref[...])
        qt_ref[h] = (_rope(y, ctab, stab, first_half) * scale).T.astype(BF16)
    for h in range(ATT_KV_HEADS):
        y = _rms(ak_ref[:, h * d:(h + 1) * d], kg_ref[...])
        k_ref[h] = _rope(y, ctab, stab, first_half).astype(BF16)
        vt_ref[h] = av_ref[:, h * d:(h + 1) * d].T.astype(BF16)


def _attn_prep(proj, q_gain, k_gain, *, tm=512):
    s = proj.shape[0]
    d = ATT_HEAD_DIM
    ctab, stab = _rope_tables(s)
    return pl.pallas_call(
        _attn_prep_body,
        grid=(s // tm,),
        in_specs=[
            pl.BlockSpec((tm, ATT_Q), lambda i: (i, COL_AQ)),
            pl.BlockSpec((tm, ATT_KV), lambda i: (i, COL_AK)),
            pl.BlockSpec((tm, ATT_KV), lambda i: (i, COL_AV)),
            pl.BlockSpec((tm, d), lambda i: (i, 0)),
            pl.BlockSpec((tm, d), lambda i: (i, 0)),
            pl.BlockSpec((1, d), lambda i: (0, 0)),
            pl.BlockSpec((1, d), lambda i: (0, 0)),
        ],
        out_specs=[
            pl.BlockSpec((ATT_HEADS, d, tm), lambda i: (0, 0, i)),
            pl.BlockSpec((ATT_KV_HEADS, tm, d), lambda i: (0, i, 0)),
            pl.BlockSpec((ATT_KV_HEADS, d, tm), lambda i: (0, 0, i)),
        ],
        out_shape=[
            jax.ShapeDtypeStruct((ATT_HEADS, d, s), BF16),
            jax.ShapeDtypeStruct((ATT_KV_HEADS, s, d), BF16),
            jax.ShapeDtypeStruct((ATT_KV_HEADS, d, s), BF16),
        ],
        compiler_params=_params("parallel"),
        name="attn_prep",
    )(proj, proj, proj, ctab, stab, q_gain.reshape(1, d), k_gain.reshape(1, d))


def _col_reduce(x, pair, final):
    fold = 8 * SUBLANE
    if x.shape[0] > fold:
        acc = x[:fold]
        for r in range(fold, x.shape[0], fold):
            acc = pair(acc, x[r:r + fold])
        x = acc
    while x.shape[0] > SUBLANE:
        half = x.shape[0] // 2
        x = pair(x[:half], x[half:])
    return final(x, axis=0, keepdims=True)


def _attn_body(qt_ref, k_ref, vt_ref, o_ref, q_all, m_ref, l_ref, acc_ref, s_ref, p_ref, *, tq, grp, wide):
    j = pl.program_id(2)
    d = ATT_HEAD_DIM
    mq = grp * tq
    chunks = mq // LANE

    @pl.when(j == 0)
    def _():
        for h in range(grp):
            for t in range(0, tq, wide):
                q_all[(h * tq + t) // wide] = qt_ref[h, :, t:t + wide]
        m_ref[...] = jnp.full(m_ref.shape, -jnp.inf, F32)
        l_ref[...] = jnp.zeros(l_ref.shape, F32)
        acc_ref[...] = jnp.zeros(acc_ref.shape, F32)

    k = k_ref[0]
    vt = vt_ref[0]
    per = wide // LANE
    groups = mq // wide

    for g in range(groups):
        st = jnp.dot(k, q_all[g], preferred_element_type=F32)
        for c in range(per):
            s_ref[g * per + c] = st[:, c * LANE:(c + 1) * LANE]
    zero = jnp.minimum(j, 0)
    alphas = []
    for ci in range(chunks):
        sc = s_ref[ci]
        m_prev = m_ref[ci]
        m_new = jnp.maximum(m_prev, _col_reduce(sc, jnp.maximum, jnp.max))
        alpha = jnp.exp2(m_prev - m_new)
        p = jnp.exp2(sc - m_new)
        l_ref[ci] = alpha * l_ref[ci] + _col_reduce(p, jnp.add, jnp.sum)
        m_ref[ci] = m_new
        p_ref[ci] = p.astype(BF16)
        alphas.append(alpha)
    pt = jnp.concatenate([p_ref[ci + zero] for ci in range(chunks)], axis=1)
    pv = jnp.dot(vt, pt, preferred_element_type=F32)
    for ci in range(chunks):
        acc_ref[ci] = alphas[ci] * acc_ref[ci] + pv[:, ci * LANE:(ci + 1) * LANE]

    @pl.when(j == pl.num_programs(2) - 1)
    def _():
        per_head = tq // LANE
        for c in range(chunks):
            h, tb = divmod(c, per_head)
            out_t = acc_ref[c] / l_ref[c]
            o_ref[tb * LANE:(tb + 1) * LANE, h * d:(h + 1) * d] = out_t.T.astype(o_ref.dtype)


def _attention(qt, k, vt, *, tq=512, tk=1024, wide=2 * LANE):
    _, s, d = k.shape
    grp = ATT_HEADS // ATT_KV_HEADS
    mq = grp * tq
    return pl.pallas_call(
        functools.partial(_attn_body, tq=tq, grp=grp, wide=wide),
        grid=(ATT_KV_HEADS, s // tq, s // tk),
        in_specs=[
            pl.BlockSpec((grp, d, tq), lambda h, i, j: (h, 0, i)),
            pl.BlockSpec((1, tk, d), lambda h, i, j: (h, j, 0)),
            pl.BlockSpec((1, d, tk), lambda h, i, j: (h, 0, j)),
        ],
        out_specs=pl.BlockSpec((tq, grp * d), lambda h, i, j: (i, h)),
        out_shape=jax.ShapeDtypeStruct((s, ATT_HEADS * d), BF16),
        scratch_shapes=[
            pltpu.VMEM((mq // wide, d, wide), BF16),
            pltpu.VMEM((mq // LANE, 1, LANE), F32),
            pltpu.VMEM((mq // LANE, 1, LANE), F32),
            pltpu.VMEM((mq // LANE, d, LANE), F32),
            pltpu.VMEM((mq // LANE, tk, LANE), F32),
            pltpu.VMEM((mq // LANE, tk, LANE), BF16),
        ],
        compiler_params=_params("parallel", "parallel", "arbitrary"),
        name="attention",
    )(qt, k, vt)


def _pool_body(prev_ref, x_ref, next_ref, g_ref, w_ref, ls_ref, o_ref, xc_ref, *, tm, s_total):
    i = pl.program_id(0)
    halo = SUBLANE
    g = g_ref[...]
    xm = x_ref[...]
    xc_ref[halo:halo + tm, :] = _rms(xm, g)
    xc_ref[0:halo, :] = jnp.where(i > 0, _rms(prev_ref[...], g), 0.0)
    xc_ref[halo + tm:2 * halo + tm, :] = jnp.where(i < pl.num_programs(0) - 1, _rms(next_ref[...], g), 0.0)
    row = i * tm + lax.broadcasted_iota(jnp.int32, (tm, 1), 0)
    p = xm.shape[1] // len(POOL_WINDOWS)
    outs = []
    for gi, window in enumerate(POOL_WINDOWS):
        hf = window // 2
        cols = slice(gi * p, (gi + 1) * p)
        acc = xc_ref[halo - hf:halo - hf + tm, cols]
        for o in range(-hf + 1, hf):
            acc = acc + xc_ref[halo + o:halo + o + tm, cols]
        cnt = (jnp.minimum(row + hf, s_total) - jnp.maximum(row - hf, 0)).astype(F32)
        pooled = acc / cnt - xc_ref[halo:halo + tm, cols]
        outs.append(jnp.dot(pooled.astype(BF16), w_ref[gi], preferred_element_type=F32))
    o_ref[...] = xm + jnp.concatenate(outs, axis=-1) * ls_ref[...]


def _pool_mixer(h, g, pool_w, layer_scale, *, tm=256):
    s, d = h.shape
    nb = tm // SUBLANE
    ng, p, _ = pool_w.shape
    return pl.pallas_call(
        functools.partial(_pool_body, tm=tm, s_total=s),
        grid=(s // tm,),
        in_specs=[
            pl.BlockSpec((SUBLANE, d), lambda i: (jnp.maximum(i * nb - 1, 0), 0)),
            pl.BlockSpec((tm, d), lambda i: (i, 0)),
            pl.BlockSpec((SUBLANE, d), lambda i: (jnp.minimum((i + 1) * nb, s // SUBLANE - 1), 0)),
            pl.BlockSpec((1, d), lambda i: (0, 0)),
            pl.BlockSpec((ng, p, p), lambda i: (0, 0, 0)),
            pl.BlockSpec((1, d), lambda i: (0, 0)),
        ],
        out_specs=pl.BlockSpec((tm, d), lambda i: (i, 0)),
        out_shape=jax.ShapeDtypeStruct((s, d), F32),
        scratch_shapes=[pltpu.VMEM((tm + 2 * SUBLANE, d), F32)],
        compiler_params=_params("parallel"),
        name="pool_mixer",
    )(h, h, h, g.reshape(1, d), pool_w.astype(BF16), layer_scale.reshape(1, d))


NEG_INF = float("-inf")


def _top_values(x, count):
    rows = []
    for _ in range(count):
        m = jnp.max(x, axis=0, keepdims=True)
        rows.append(m)
        x = jnp.where(x == m, NEG_INF, x)
    return rows


def _stack_rows(rows, t, n=None):
    n = len(rows) if n is None else n
    rid = lax.broadcasted_iota(jnp.int32, (n, t), 0)
    out = jnp.full((n, t), NEG_INF, F32)
    for r, row in enumerate(rows):
        out = jnp.where(rid == r, row, out)
    return out


def _route_head(s1, s2):
    t = s1.shape[1]
    kk = PEER_TOPK
    v1 = _top_values(s1, kk + 1)
    v2 = _top_values(s2, kk + 1)
    v2_all = _stack_rows(v2[:kk], t)
    pieces = []
    for a in range(kk // 2):
        nb = kk // (a + 1)
        rows = kk if nb > SUBLANE else SUBLANE
        rid = lax.broadcasted_iota(jnp.int32, (rows, t), 0)
        pieces.append(jnp.where(rid < nb, v1[a] + v2_all[:rows], NEG_INF))
    tail = [v1[a] + v2[0] for a in range(kk // 2, kk + 1)] + [v1[0] + v2[kk]]
    pieces.append(_stack_rows(tail, t, kk))
    cand = jnp.concatenate(pieces, axis=0)
    best = _top_values(cand, kk + 1)
    tau = 0.5 * (best[kk - 1] + best[kk])
    z = jnp.ones_like(tau)
    for r in range(1, kk):
        z = z + jnp.exp(best[r] - best[0])
    thr = jnp.exp(tau - v2[0] - s1) / z
    e1 = jnp.exp(s1 - v1[0])
    e2 = jnp.exp(s2 - v2[0]) / z
    return thr, e1, e2


def _peer_route_body(h_ref, g_ref, wq_ref, keys_ref, xnt_ref, thr_ref, e1_ref, e2_ref):
    nk = PEER_NKEYS
    xnt = _rms(h_ref[...], g_ref[...]).T.astype(BF16)
    xnt_ref[...] = xnt
    qt = jnp.dot(wq_ref[...], xnt, preferred_element_type=F32)
    for hd in range(PEER_HEADS):
        q1 = qt[(2 * hd) * nk:(2 * hd + 1) * nk].astype(BF16)
        q2 = qt[(2 * hd + 1) * nk:(2 * hd + 2) * nk].astype(BF16)
        s1 = jnp.dot(keys_ref[2 * hd], q1, preferred_element_type=F32)
        s2 = jnp.dot(keys_ref[2 * hd + 1], q2, preferred_element_type=F32)
        thr, e1, e2 = _route_head(s1, s2)
        thr_ref[hd] = thr
        e1_ref[hd] = e1
        for c in range(e2.shape[1] // LANE):
            e2_ref[hd, c] = e2[:, c * LANE:(c + 1) * LANE]


def _peer_route(h, g, wq_t, keys, *, tt=256):
    s, d = h.shape
    nq = wq_t.shape[0]
    hk = jax.ShapeDtypeStruct((PEER_HEADS, PEER_NKEYS, s), F32)
    hck = jax.ShapeDtypeStruct((PEER_HEADS, s // LANE, PEER_NKEYS, LANE), F32)
    gate_spec = pl.BlockSpec((PEER_HEADS, PEER_NKEYS, tt), lambda i: (0, 0, i))
    chunk_spec = pl.BlockSpec((PEER_HEADS, tt // LANE, PEER_NKEYS, LANE), lambda i: (0, i, 0, 0))
    return pl.pallas_call(
        _peer_route_body,
        grid=(s // tt,),
        in_specs=[
            pl.BlockSpec((tt, d), lambda i: (i, 0)),
            pl.BlockSpec((1, d), lambda i: (0, 0)),
            pl.BlockSpec((nq, d), lambda i: (0, 0)),
            pl.BlockSpec(keys.shape, lambda i: (0, 0, 0)),
        ],
        out_specs=[pl.BlockSpec((d, tt), lambda i: (0, i)), gate_spec, gate_spec, chunk_spec],
        out_shape=[jax.ShapeDtypeStruct((d, s), BF16), hk, hk, hck],
        compiler_params=_params("parallel"),
        name="peer_route",
    )(h, g.reshape(1, d), wq_t, keys)


def _gelu(x):
    return 0.5 * x * (1.0 + lax.erf(x * (2.0 ** -0.5)))


def _peer_dense_body(xnt_ref, u0_ref, u1_ref, vt0_ref, vt1_ref, thr0_ref, thr1_ref, e10_ref, e11_ref,
                     e20_ref, e21_ref, o_ref, a0_ref, a1_ref, p0_ref, p1_ref, *, tt, eb, nb):
    i = pl.program_id(0)
    nk = PEER_NKEYS
    rb = 4 * SUBLANE
    chunks = tt // LANE

    @pl.when(i == 0)
    def _():
        a1_ref[...] = jnp.zeros(a1_ref.shape, F32)
        p0_ref[...] = jnp.zeros(p0_ref.shape, BF16)

    first_pair = jnp.maximum(2 * i - 2, 0)

    @pl.when(first_pair % nb == 0)
    def _():
        o_ref[...] = jnp.zeros(o_ref.shape, F32)

    def activations(u_ref, a_ref, cs):
        lanes = slice(cs[0] * LANE, (cs[-1] + 1) * LANE)
        a = jnp.dot(u_ref[...], xnt_ref[:, lanes], preferred_element_type=F32)
        for n, c in enumerate(cs):
            a_ref[c] = a[:, n * LANE:(n + 1) * LANE]

    def gated(thr_ref, e1_ref, e2_ref, a_ref, p_ref, cs, row0):
        ks = range(eb // nk)
        for c in cs:
            lanes = slice(c * LANE, (c + 1) * LANE)
            for r in range(0, nk, rb):
                gates = {k: jnp.zeros((rb, LANE), F32) for k in ks}
                for hd in range(PEER_HEADS):
                    e2 = e2_ref[hd, c, r:r + rb, :]
                    for k in ks:
                        thr = thr_ref[hd, 0, row0 + k:row0 + k + 1, lanes]
                        e1 = e1_ref[hd, 0, row0 + k:row0 + k + 1, lanes]
                        gates[k] = gates[k] + jnp.where(e2 >= thr, e2 * e1, 0.0)
                for k in ks:
                    blk = slice(k * nk + r, k * nk + r + rb)
                    p_ref[c, blk, :] = (gates[k] * _gelu(a_ref[c, blk, :])).astype(BF16)

    def project(vt_ref, p_ref, cs):
        lanes = slice(cs[0] * LANE, (cs[-1] + 1) * LANE)
        p = jnp.concatenate([p_ref[c] for c in cs], axis=1)
        o_ref[:, lanes] += jnp.dot(vt_ref[...], p, preferred_element_type=F32)

    def half_step(u_ref, a_new, thr_ref, e1_ref, e2_ref, a_old, p_new, vt_ref, p_old, row0):
        half = chunks // 2
        for cs in (range(half), range(half, chunks)):
            activations(u_ref, a_new, cs)
            gated(thr_ref, e1_ref, e2_ref, a_old, p_new, cs[:len(cs) // 2], row0)
            project(vt_ref, p_old, cs)
            gated(thr_ref, e1_ref, e2_ref, a_old, p_new, cs[len(cs) // 2:], row0)

    rows = eb // nk
    half_step(u0_ref, a0_ref, thr0_ref, e10_ref, e20_ref, a1_ref, p1_ref, vt0_ref, p0_ref, rows)
    half_step(u1_ref, a1_ref, thr1_ref, e11_ref, e21_ref, a0_ref, p0_ref, vt1_ref, p1_ref, 0)


def _peer_dense(xnt, u_all, vt_all, layer, thr, e1, e2, *, tt=512, eb=512):
    d, s = xnt.shape
    ne = u_all.shape[1]
    nk = PEER_NKEYS
    rows = eb // nk
    nb = ne // eb
    n = (s // tt) * nb
    assert nb % 2 == 0 and 2 * rows == SUBLANE
    split = lambda a: a.reshape(PEER_HEADS, nb // 2, SUBLANE, s)

    def pair(lag, half):
        return lambda i: divmod(jnp.clip(2 * i + half - lag, 0, n - 1), nb)

    def u_spec(half):
        return pl.BlockSpec((None, eb, d), lambda i: (layer, pair(0, half)(i)[1], 0))

    def vt_spec(half):
        return pl.BlockSpec((None, d, eb), lambda i: (layer, 0, pair(2, half)(i)[1]))

    def row_spec(half):
        return pl.BlockSpec((PEER_HEADS, 1, SUBLANE, tt),
                            lambda i: (0, pair(1, half)(i)[1] // 2, 0, pair(1, half)(i)[0]))

    def e2_spec(half):
        return pl.BlockSpec((PEER_HEADS, tt // LANE, nk, LANE), lambda i: (0, pair(1, half)(i)[0], 0, 0))

    slab = lambda dtype: pltpu.VMEM((tt // LANE, eb, LANE), dtype)
    return pl.pallas_call(
        functools.partial(_peer_dense_body, tt=tt, eb=eb, nb=nb),
        grid=(n // 2 + 1,),
        in_specs=[
            pl.BlockSpec((d, tt), lambda i: (0, pair(0, 0)(i)[0])),
            u_spec(0), u_spec(1), vt_spec(0), vt_spec(1),
            row_spec(0), row_spec(1), row_spec(0), row_spec(1), e2_spec(0), e2_spec(1),
        ],
        out_specs=pl.BlockSpec((d, tt), lambda i: (0, pair(2, 0)(i)[0])),
        out_shape=jax.ShapeDtypeStruct((d, s), F32),
        scratch_shapes=[slab(F32), slab(F32), slab(BF16), slab(BF16)],
        compiler_params=_params("arbitrary"),
        name="peer_dense",
    )(xnt, u_all, u_all, vt_all, vt_all, split(thr), split(thr), split(e1), split(e1), e2, e2)


def _add_t_body(h_ref, pt_ref, g_ref, o_ref, *, final_norm):
    y = h_ref[...] + pt_ref[...].T
    o_ref[...] = _rms(y, g_ref[...]) if final_norm else y


def _add_transposed(h, pt, g=None, *, tm=512):
    s, d = h.shape
    gain = jnp.ones((1, d), F32) if g is None else g.reshape(1, d)
    return pl.pallas_call(
        functools.partial(_add_t_body, final_norm=g is not None),
        grid=(s // tm,),
        in_specs=[
            pl.BlockSpec((tm, d), lambda i: (i, 0)),
            pl.BlockSpec((d, tm), lambda i: (0, i)),
            pl.BlockSpec((1, d), lambda i: (0, 0)),
        ],
        out_specs=pl.BlockSpec((tm, d), lambda i: (i, 0)),
        out_shape=jax.ShapeDtypeStruct((s, d), F32),
        compiler_params=_params("parallel"),
        name="add_transposed",
    )(h, pt, gain)


def _peer_ffn(h, g, w_query, sub_keys, u_all, vt_all, layer, final_g=None):
    nk = PEER_NKEYS
    keys = sub_keys.reshape(2 * PEER_HEADS, nk, -1).astype(BF16)
    xnt, thr, e1, e2 = _peer_route(h, g, w_query.T.astype(BF16), keys)
    pt = _peer_dense(xnt, u_all, vt_all, layer, thr, e1, e2)
    return _add_transposed(h, pt, final_g)


def _mlstm_prep_body(qp_ref, q_ref, qn_ref, kp_ref, k_ref, kn_ref, v_ref, cwq_ref, cwk_ref,
                     qo_ref, kto_ref, vo_ref, xc_ref, *, tm):
    i = pl.program_id(0)
    last = pl.num_programs(0) - 1
    halo = SUBLANE
    pad = CONV_W // 2

    def conv_silu(prev_ref, x_ref, next_ref, w_ref):
        xc_ref[halo:halo + tm, :] = x_ref[...]
        xc_ref[0:halo, :] = jnp.where(i > 0, prev_ref[...], 0.0)
        xc_ref[halo + tm:2 * halo + tm, :] = jnp.where(i < last, next_ref[...], 0.0)
        acc = xc_ref[halo - pad:halo - pad + tm, :] * w_ref[0:1, :]
        for j in range(1, CONV_W):
            acc = acc + xc_ref[halo - pad + j:halo - pad + j + tm, :] * w_ref[j:j + 1, :]
        return acc * jax.nn.sigmoid(acc)

    qo_ref[...] = conv_silu(qp_ref, q_ref, qn_ref, cwq_ref).astype(BF16)
    kc = conv_silu(kp_ref, k_ref, kn_ref, cwk_ref) * (ML_HEAD_DIM ** -0.5)
    kto_ref[...] = kc.T.astype(BF16)
    vo_ref[...] = v_ref[...].astype(BF16)


def _mlstm_prep(proj, conv_w, *, tm=256):
    s = proj.shape[0]
    w = ML_W
    nb = tm // SUBLANE
    prev = lambda c: pl.BlockSpec((SUBLANE, w), lambda i: (jnp.maximum(i * nb - 1, 0), c))
    main = lambda c: pl.BlockSpec((tm, w), lambda i: (i, c))
    nxt = lambda c: pl.BlockSpec((SUBLANE, w), lambda i: (jnp.minimum((i + 1) * nb, s // SUBLANE - 1), c))
    cw = pl.BlockSpec((CONV_W, w), lambda i: (0, 0))
    return pl.pallas_call(
        functools.partial(_mlstm_prep_body, tm=tm),
        grid=(s // tm,),
        in_specs=[prev(COL_MQ), main(COL_MQ), nxt(COL_MQ), prev(COL_MK), main(COL_MK), nxt(COL_MK),
                  main(COL_MV), cw, cw],
        out_specs=[
            pl.BlockSpec((tm, w), lambda i: (i, 0)),
            pl.BlockSpec((w, tm), lambda i: (0, i)),
            pl.BlockSpec((tm, w), lambda i: (i, 0)),
        ],
        out_shape=[
            jax.ShapeDtypeStruct((s, w), BF16),
            jax.ShapeDtypeStruct((w, s), BF16),
            jax.ShapeDtypeStruct((s, w), BF16),
        ],
        scratch_shapes=[pltpu.VMEM((tm + 2 * SUBLANE, w), F32)],
        compiler_params=_params("parallel"),
        name="mlstm_prep",
    )(proj, proj, proj, proj, proj, proj, proj, conv_w[:, :w], conv_w[:, w:])


def _log_sigmoid(x):
    return jnp.minimum(x, 0.0) - jnp.log1p(jnp.exp(-jnp.abs(x)))


def _mlstm_direction(q_ref, kt_ref, v_ref, g_ref, bias_ref, o_ref, s_ref, m_ref, *, backward):
    L = ML_CHUNK
    d = ML_HEAD_DIM
    row = lax.broadcasted_iota(jnp.int32, (L, L), 0)
    col = lax.broadcasted_iota(jnp.int32, (L, L), 1)
    keep = (col >= row) if backward else (col <= row)
    tri = keep.astype(BF16)
    ones = jnp.ones((L, LANE), BF16)

    gates = g_ref[...] + bias_ref[...]
    logf = _log_sigmoid(gates)
    hi = logf.astype(BF16)
    lo = (logf - hi.astype(F32)).astype(BF16)
    bcol = jnp.dot(tri, hi, preferred_element_type=F32) + jnp.dot(tri, lo, preferred_element_type=F32)
    gates_t = gates.T
    b_t = bcol.T
    base = 2 * ML_HEADS if backward else 0
    for h in range(ML_HEADS):
        idx = base // 2 + h
        ci, cf = base + h, base + ML_HEADS + h
        i_row = gates_t[ci:ci + 1, :]
        b_row = b_t[cf:cf + 1, :]
        b_col = bcol[:, cf:cf + 1]
        g_tot = b_row[:, 0:1] if backward else b_row[:, L - 1:L]
        a_row = g_tot - b_row + i_row
        a_max = jnp.max(a_row, axis=1, keepdims=True)
        w_row = jnp.exp(a_row - a_max)

        q = q_ref[:, h * d:(h + 1) * d]
        kt = kt_ref[h * d:(h + 1) * d, :]
        v_aug = jnp.concatenate([v_ref[:, h * d:(h + 1) * d], ones], axis=1)
        s_prev = s_ref[idx]
        m_prev = m_ref[idx][:, 0:1]

        dlog = jnp.where(keep, b_col + (i_row - b_row), NEG_INF)
        inter_log = b_col + m_prev
        m_j = jnp.maximum(inter_log, jnp.max(dlog, axis=1, keepdims=True))
        sqk = jnp.dot(q, kt, preferred_element_type=F32) * jnp.exp(dlog - m_j)
        inter_w = jnp.exp(inter_log - m_j)
        tot = (inter_w * jnp.dot(q, s_prev.astype(BF16), preferred_element_type=F32)
               + jnp.dot(sqk.astype(BF16), v_aug, preferred_element_type=F32))
        den = tot[:, d:]
        inv = 1.0 / jnp.maximum(jnp.abs(den), jnp.exp(-m_j))
        o_ref[:, h * d:(h + 1) * d] = tot[:, :d] * jnp.concatenate([inv] * (d // LANE), axis=1)

        ktw = (kt.astype(F32) * w_row).astype(BF16)
        upd = jnp.dot(ktw, v_aug, preferred_element_type=F32)
        m_new = jnp.maximum(g_tot + m_prev, a_max)
        s_ref[idx] = jnp.exp(g_tot + m_prev - m_new) * s_prev + jnp.exp(a_max - m_new) * upd
        m_ref[idx] = jnp.broadcast_to(m_new, (1, LANE))


def _mlstm_body(qf_ref, ktf_ref, vf_ref, gf_ref, qb_ref, ktb_ref, vb_ref, gb_ref, bias_ref,
                hf_ref, hb_ref, s_ref, m_ref):
    @pl.when(pl.program_id(0) == 0)
    def _():
        s_ref[...] = jnp.zeros(s_ref.shape, F32)
        m_ref[...] = jnp.zeros(m_ref.shape, F32)

    _mlstm_direction(qf_ref, ktf_ref, vf_ref, gf_ref, bias_ref, hf_ref, s_ref, m_ref, backward=False)
    _mlstm_direction(qb_ref, ktb_ref, vb_ref, gb_ref, bias_ref, hb_ref, s_ref, m_ref, backward=True)


def _mlstm_scan(mq, mkt, mv, gates, b_gate):
    s, w = mq.shape
    L = ML_CHUNK
    nc = s // L
    d = ML_HEAD_DIM
    bias = jnp.pad(b_gate, (0, LANE - b_gate.shape[0])).reshape(1, LANE)
    fwd = lambda c: c
    bwd = lambda c: nc - 1 - c
    rows = lambda ix, n: pl.BlockSpec((L, n), lambda c: (ix(c), 0))
    cols = lambda ix: pl.BlockSpec((w, L), lambda c: (0, ix(c)))
    return pl.pallas_call(
        _mlstm_body,
        grid=(nc,),
        in_specs=[rows(fwd, w), cols(fwd), rows(fwd, w), rows(fwd, LANE),
                  rows(bwd, w), cols(bwd), rows(bwd, w), rows(bwd, LANE),
                  pl.BlockSpec((1, LANE), lambda c: (0, 0))],
        out_specs=[rows(fwd, w), rows(bwd, w)],
        out_shape=[jax.ShapeDtypeStruct((s, w), F32)] * 2,
        scratch_shapes=[pltpu.VMEM((2 * ML_HEADS, d, d + LANE), F32), pltpu.VMEM((2 * ML_HEADS, 1, LANE), F32)],
        compiler_params=_params("arbitrary"),
        name="mlstm_scan",
    )(mq, mkt, mv, gates, mq, mkt, mv, gates, bias)


def _mix_out_body(ya_ref, hf_ref, hb_ref, mo_ref, mg_ref, wa_ref, wm_ref, r_ref, o_ref, yml_ref):
    @pl.when(pl.program_id(1) == 0)
    def _():
        d = ML_HEAD_DIM
        for h in range(ML_HEADS):
            sl = slice(h * d, (h + 1) * d)
            hm = _rms(hf_ref[:, sl] + hb_ref[:, sl], mg_ref[:, sl])
            yml_ref[:, sl] = (hm * jax.nn.sigmoid(mo_ref[:, sl])).astype(BF16)

    o_ref[...] = (r_ref[...] + jnp.dot(ya_ref[...], wa_ref[...], preferred_element_type=F32)
                  + jnp.dot(yml_ref[...], wm_ref[...], preferred_element_type=F32))


def _mix_out(y_att, hf, hb, proj, ml_gain, w_att, w_ml, res, *, tm=512, tn=512):
    s, n = res.shape
    wa, wm = w_att.shape[0], w_ml.shape[0]
    return pl.pallas_call(
        _mix_out_body,
        grid=(s // tm, n // tn),
        in_specs=[
            pl.BlockSpec((tm, wa), lambda i, j: (i, 0)),
            pl.BlockSpec((tm, wm), lambda i, j: (i, 0)),
            pl.BlockSpec((tm, wm), lambda i, j: (i, 0)),
            pl.BlockSpec((tm, wm), lambda i, j: (i, COL_MO)),
            pl.BlockSpec((1, wm), lambda i, j: (0, 0)),
            pl.BlockSpec((wa, tn), lambda i, j: (0, j)),
            pl.BlockSpec((wm, tn), lambda i, j: (0, j)),
            pl.BlockSpec((tm, tn), lambda i, j: (i, j)),
        ],
        out_specs=pl.BlockSpec((tm, tn), lambda i, j: (i, j)),
        out_shape=jax.ShapeDtypeStruct((s, n), F32),
        scratch_shapes=[pltpu.VMEM((tm, wm), BF16)],
        compiler_params=_params("parallel", "arbitrary"),
        name="mix_out",
    )(y_att, hf, hb, proj, ml_gain.reshape(1, wm), w_att, w_ml, res)


def kernel(x, norm_mix_g, norm_ffn_g, w_in, b_gate, conv_w, q_gain, k_gain, ml_gain, w_out, pool_w,
           layer_scale, peer_wq, peer_keys, peer_u, peer_v, norm_f):
    b, s, d = x.shape
    h = x.reshape(b * s, d)

    wi = w_in[0]
    kv_end = ATT_Q + 2 * ATT_KV
    w_main = jnp.concatenate([wi[:, :ATT_Q], wi[:, kv_end:MAIN_W], wi[:, ATT_Q:kv_end]], axis=1).astype(BF16)
    w_gate = jnp.pad(wi[:, MAIN_W:], ((0, 0), (0, LANE - (wi.shape[1] - MAIN_W)))).astype(BF16)
    proj, gates = _norm_matmul(h, norm_mix_g[0], w_main, w_gate)
    qt, k, vt = _attn_prep(proj, q_gain[0], k_gain[0])
    y_att = _attention(qt, k, vt)
    mq, mkt, mv = _mlstm_prep(proj, conv_w[0])
    hf, hb = _mlstm_scan(mq, mkt, mv, gates, b_gate[0])
    wo = w_out[0].astype(BF16)
    h = _mix_out(y_att, hf, hb, proj, ml_gain[0], wo[:ATT_Q], wo[ATT_Q:], h)
    u_all = peer_u.astype(BF16)
    vt_all = jnp.swapaxes(peer_v, 1, 2).astype(BF16)
    h = _peer_ffn(h, norm_ffn_g[0], peer_wq[0], peer_keys[0], u_all, vt_all, 0)

    h = _pool_mixer(h, norm_mix_g[1], pool_w[0], layer_scale[0])
    h = _peer_ffn(h, norm_ffn_g[1], peer_wq[1], peer_keys[1], u_all, vt_all, 1, final_g=norm_f)
    return h.reshape(b, s, d)
```

```python
import functools

import jax
import jax.numpy as jnp
from jax import lax
from jax.experimental import pallas as pl
from jax.experimental.pallas import tpu as pltpu

F32 = jnp.float32
BF16 = jnp.bfloat16
EPS = 1e-6
LOG2E = 1.4426950408889634

GRID_W = 64
ROPE_THETA = 10000.0
ATT_HEADS = 8
ATT_KV_HEADS = 2
ATT_HEAD_DIM = 128
ML_HEADS = 4
ML_HEAD_DIM = 256
ML_CHUNK = 128
CONV_W = 5
POOL_WINDOWS = (2, 4, 8, 16)
PEER_HEADS = 8
PEER_NKEYS = 128
PEER_TOPK = 16
PEER_BLOCK = 128

ATT_Q = ATT_HEADS * ATT_HEAD_DIM
ATT_KV = ATT_KV_HEADS * ATT_HEAD_DIM
ML_W = ML_HEADS * ML_HEAD_DIM
MAIN_W = ATT_Q + 2 * ATT_KV + 4 * ML_W
assert ATT_Q == ML_W
COL_AQ, COL_MQ, COL_MK, COL_MV, COL_MO = 0, 1, 2, 3, 4
COL_AK = (ATT_Q + 4 * ML_W) // ATT_KV
COL_AV = COL_AK + 1
LANE = 128
SUBLANE = 8
VMEM_LIMIT = 48 * 1024 * 1024


def _params(*sem):
    return pltpu.CompilerParams(dimension_semantics=sem, vmem_limit_bytes=VMEM_LIMIT)


def _rms(x, g):
    return x * lax.rsqrt(jnp.mean(x * x, axis=-1, keepdims=True) + EPS) * g


def _norm_mm_body(x_ref, g_ref, w_ref, wg_ref, o_ref, og_ref, xn_ref):
    @pl.when(pl.program_id(1) == 0)
    def _():
        xn = _rms(x_ref[...], g_ref[...]).astype(BF16)
        xn_ref[...] = xn
        og_ref[...] = jnp.dot(xn, wg_ref[...], preferred_element_type=F32)

    o_ref[...] = jnp.dot(xn_ref[...], w_ref[...], preferred_element_type=F32)


def _norm_matmul(x, g, w, wg, *, tm=512, tn=1408):
    s, d = x.shape
    n = w.shape[1]
    return pl.pallas_call(
        _norm_mm_body,
        grid=(s // tm, n // tn),
        in_specs=[
            pl.BlockSpec((tm, d), lambda i, j: (i, 0)),
            pl.BlockSpec((1, d), lambda i, j: (0, 0)),
            pl.BlockSpec((d, tn), lambda i, j: (0, j)),
            pl.BlockSpec((d, LANE), lambda i, j: (0, 0)),
        ],
        out_specs=[
            pl.BlockSpec((tm, tn), lambda i, j: (i, j)),
            pl.BlockSpec((tm, LANE), lambda i, j: (i, 0)),
        ],
        out_shape=[jax.ShapeDtypeStruct((s, n), F32), jax.ShapeDtypeStruct((s, LANE), F32)],
        scratch_shapes=[pltpu.VMEM((tm, d), BF16)],
        compiler_params=_params("parallel", "arbitrary"),
        name="norm_matmul",
    )(x, g.reshape(1, d), w, wg)


def _rope_tables(s):
    rows = s // GRID_W
    r, c = jnp.meshgrid(jnp.arange(rows), jnp.arange(GRID_W), indexing="ij")
    pos = jnp.stack([r.reshape(-1), c.reshape(-1)], axis=-1).astype(F32)
    nf = ATT_HEAD_DIM // 4
    inv = ROPE_THETA ** (-jnp.arange(nf, dtype=F32) / nf)
    ang = pos[:, :, None] * inv
    cos, sin = jnp.cos(ang), jnp.sin(ang)
    ctab = jnp.stack([cos, cos], axis=2).reshape(s, ATT_HEAD_DIM)
    stab = jnp.stack([-sin, sin], axis=2).reshape(s, ATT_HEAD_DIM)
    return ctab, stab


def _rope(y, ctab, stab, first_half):
    nf = ATT_HEAD_DIM // 4
    swapped = jnp.where(first_half, pltpu.roll(y, ATT_HEAD_DIM - nf, 1), pltpu.roll(y, nf, 1))
    return y * ctab + swapped * stab


def _attn_prep_body(aq_ref, ak_ref, av_ref, c_ref, s_ref, qg_ref, kg_ref, qt_ref, k_ref, vt_ref):
    d = ATT_HEAD_DIM
    ctab, stab = c_ref[...], s_ref[...]
    lane = lax.broadcasted_iota(jnp.int32, ctab.shape, 1)
    first_half = (lane % (d // 2)) < (d // 4)
    scale = d ** -0.5 * LOG2E
    for h in range(ATT_HEADS):
        y = _rms(aq_ref[:, h * d:(h + 1) * d], qg_ref[...])
        qt_ref[h] = (_rope(y, ctab, stab, first_half) * scale).T.astype(BF16)
    for h in range(ATT_KV_HEADS):
        y = _rms(ak_ref[:, h * d:(h + 1) * d], kg_ref[...])
        k_ref[h] = _rope(y, ctab, stab, first_half).astype(BF16)
        vt_ref[h] = av_ref[:, h * d:(h + 1) * d].T.astype(BF16)


def _attn_prep(proj, q_gain, k_gain, *, tm=512):
    s = proj.shape[0]
    d = ATT_HEAD_DIM
    ctab, stab = _rope_tables(s)
    return pl.pallas_call(
        _attn_prep_body,
        grid=(s // tm,),
        in_specs=[
            pl.BlockSpec((tm, ATT_Q), lambda i: (i, COL_AQ)),
            pl.BlockSpec((tm, ATT_KV), lambda i: (i, COL_AK)),
            pl.BlockSpec((tm, ATT_KV), lambda i: (i, COL_AV)),
            pl.BlockSpec((tm, d), lambda i: (i, 0)),
            pl.BlockSpec((tm, d), lambda i: (i, 0)),
            pl.BlockSpec((1, d), lambda i: (0, 0)),
            pl.BlockSpec((1, d), lambda i: (0, 0)),
        ],
        out_specs=[
            pl.BlockSpec((ATT_HEADS, d, tm), lambda i: (0, 0, i)),
            pl.BlockSpec((ATT_KV_HEADS, tm, d), lambda i: (0, i, 0)),
            pl.BlockSpec((ATT_KV_HEADS, d, tm), lambda i: (0, 0, i)),
        ],
        out_shape=[
            jax.ShapeDtypeStruct((ATT_HEADS, d, s), BF16),
            jax.ShapeDtypeStruct((ATT_KV_HEADS, s, d), BF16),
            jax.ShapeDtypeStruct((ATT_KV_HEADS, d, s), BF16),
        ],
        compiler_params=_params("parallel"),
        name="attn_prep",
    )(proj, proj, proj, ctab, stab, q_gain.reshape(1, d), k_gain.reshape(1, d))


def _col_reduce(x, pair, final):
    fold = 8 * SUBLANE
    if x.shape[0] > fold:
        acc = x[:fold]
        for r in range(fold, x.shape[0], fold):
            acc = pair(acc, x[r:r + fold])
        x = acc
    while x.shape[0] > SUBLANE:
        half = x.shape[0] // 2
        x = pair(x[:half], x[half:])
    return final(x, axis=0, keepdims=True)


def _attn_body(qt_ref, k_ref, vt_ref, o_ref, q_all, m_ref, l_ref, acc_ref, s_ref, p_ref, *, tq, grp, wide):
    j = pl.program_id(2)
    d = ATT_HEAD_DIM
    mq = grp * tq
    chunks = mq // LANE

    @pl.when(j == 0)
    def _():
        for h in range(grp):
            for t in range(0, tq, wide):
                q_all[(h * tq + t) // wide] = qt_ref[h, :, t:t + wide]
        m_ref[...] = jnp.full(m_ref.shape, -jnp.inf, F32)
        l_ref[...] = jnp.zeros(l_ref.shape, F32)
        acc_ref[...] = jnp.zeros(acc_ref.shape, F32)

    k = k_ref[0]
    vt = vt_ref[0]
    per = wide // LANE
    groups = mq // wide

    for g in range(groups):
        st = jnp.dot(k, q_all[g], preferred_element_type=F32)
        for c in range(per):
            s_ref[g * per + c] = st[:, c * LANE:(c + 1) * LANE]
    alphas = []
    for ci in range(chunks):
        sc = s_ref[ci]
        m_prev = m_ref[ci]
        m_new = jnp.maximum(m_prev, _col_reduce(sc, jnp.maximum, jnp.max))
        alpha = jnp.exp2(m_prev - m_new)
        p = jnp.exp2(sc - m_new)
        l_ref[ci] = alpha * l_ref[ci] + _col_reduce(p, jnp.add, jnp.sum)
        m_ref[ci] = m_new
        p_ref[ci] = p.astype(BF16)
        alphas.append(alpha)
    pt = jnp.concatenate([p_ref[ci] for ci in range(chunks)], axis=1)
    pv = jnp.dot(vt, pt, preferred_element_type=F32)
    for ci in range(chunks):
        acc_ref[ci] = alphas[ci] * acc_ref[ci] + pv[:, ci * LANE:(ci + 1) * LANE]

    @pl.when(j == pl.num_programs(2) - 1)
    def _():
        per_head = tq // LANE
        for c in range(chunks):
            h, tb = divmod(c, per_head)
            out_t = acc_ref[c] / l_ref[c]
            o_ref[tb * LANE:(tb + 1) * LANE, h * d:(h + 1) * d] = out_t.T.astype(o_ref.dtype)


def _attention(qt, k, vt, *, tq=512, tk=1024, wide=2 * LANE):
    _, s, d = k.shape
    grp = ATT_HEADS // ATT_KV_HEADS
    mq = grp * tq
    return pl.pallas_call(
        functools.partial(_attn_body, tq=tq, grp=grp, wide=wide),
        grid=(ATT_KV_HEADS, s // tq, s // tk),
        in_specs=[
            pl.BlockSpec((grp, d, tq), lambda h, i, j: (h, 0, i)),
            pl.BlockSpec((1, tk, d), lambda h, i, j: (h, j, 0)),
            pl.BlockSpec((1, d, tk), lambda h, i, j: (h, 0, j)),
        ],
        out_specs=pl.BlockSpec((tq, grp * d), lambda h, i, j: (i, h)),
        out_shape=jax.ShapeDtypeStruct((s, ATT_HEADS * d), BF16),
        scratch_shapes=[
            pltpu.VMEM((mq // wide, d, wide), BF16),
            pltpu.VMEM((mq // LANE, 1, LANE), F32),
            pltpu.VMEM((mq // LANE, 1, LANE), F32),
            pltpu.VMEM((mq // LANE, d, LANE), F32),
            pltpu.VMEM((mq // LANE, tk, LANE), F32),
            pltpu.VMEM((mq // LANE, tk, LANE), BF16),
        ],
        compiler_params=_params("parallel", "parallel", "arbitrary"),
        name="attention",
    )(qt, k, vt)


def _pool_body(prev_ref, x_ref, next_ref, g_ref, w_ref, ls_ref, o_ref, xc_ref, *, tm, s_total):
    i = pl.program_id(0)
    halo = SUBLANE
    g = g_ref[...]
    xm = x_ref[...]
    xc_ref[halo:halo + tm, :] = _rms(xm, g)
    xc_ref[0:halo, :] = jnp.where(i > 0, _rms(prev_ref[...], g), 0.0)
    xc_ref[halo + tm:2 * halo + tm, :] = jnp.where(i < pl.num_programs(0) - 1, _rms(next_ref[...], g), 0.0)
    row = i * tm + lax.broadcasted_iota(jnp.int32, (tm, 1), 0)
    p = xm.shape[1] // len(POOL_WINDOWS)
    outs = []
    for gi, window in enumerate(POOL_WINDOWS):
        hf = window // 2
        cols = slice(gi * p, (gi + 1) * p)
        acc = xc_ref[halo - hf:halo - hf + tm, cols]
        for o in range(-hf + 1, hf):
            acc = acc + xc_ref[halo + o:halo + o + tm, cols]
        cnt = (jnp.minimum(row + hf, s_total) - jnp.maximum(row - hf, 0)).astype(F32)
        pooled = acc / cnt - xc_ref[halo:halo + tm, cols]
        outs.append(jnp.dot(pooled.astype(BF16), w_ref[gi], preferred_element_type=F32))
    o_ref[...] = xm + jnp.concatenate(outs, axis=-1) * ls_ref[...]


def _pool_mixer(h, g, pool_w, layer_scale, *, tm=256):
    s, d = h.shape
    nb = tm // SUBLANE
    ng, p, _ = pool_w.shape
    return pl.pallas_call(
        functools.partial(_pool_body, tm=tm, s_total=s),
        grid=(s // tm,),
        in_specs=[
            pl.BlockSpec((SUBLANE, d), lambda i: (jnp.maximum(i * nb - 1, 0), 0)),
            pl.BlockSpec((tm, d), lambda i: (i, 0)),
            pl.BlockSpec((SUBLANE, d), lambda i: (jnp.minimum((i + 1) * nb, s // SUBLANE - 1), 0)),
            pl.BlockSpec((1, d), lambda i: (0, 0)),
            pl.BlockSpec((ng, p, p), lambda i: (0, 0, 0)),
            pl.BlockSpec((1, d), lambda i: (0, 0)),
        ],
        out_specs=pl.BlockSpec((tm, d), lambda i: (i, 0)),
        out_shape=jax.ShapeDtypeStruct((s, d), F32),
        scratch_shapes=[pltpu.VMEM((tm + 2 * SUBLANE, d), F32)],
        compiler_params=_params("parallel"),
        name="pool_mixer",
    )(h, h, h, g.reshape(1, d), pool_w.astype(BF16), layer_scale.reshape(1, d))


NEG_INF = float("-inf")


def _top_values(x, count):
    rows = []
    for _ in range(count):
        m = jnp.max(x, axis=0, keepdims=True)
        rows.append(m)
        x = jnp.where(x == m, NEG_INF, x)
    return rows


def _stack_rows(rows, t, n=None):
    n = len(rows) if n is None else n
    rid = lax.broadcasted_iota(jnp.int32, (n, t), 0)
    out = jnp.full((n, t), NEG_INF, F32)
    for r, row in enumerate(rows):
        out = jnp.where(rid == r, row, out)
    return out


def _route_head(s1, s2):
    t = s1.shape[1]
    kk = PEER_TOPK
    v1 = _top_values(s1, kk + 1)
    v2 = _top_values(s2, kk + 1)
    v2_all = _stack_rows(v2[:kk], t)
    pieces = []
    for a in range(kk // 2):
        nb = kk // (a + 1)
        rows = kk if nb > SUBLANE else SUBLANE
        rid = lax.broadcasted_iota(jnp.int32, (rows, t), 0)
        pieces.append(jnp.where(rid < nb, v1[a] + v2_all[:rows], NEG_INF))
    tail = [v1[a] + v2[0] for a in range(kk // 2, kk + 1)] + [v1[0] + v2[kk]]
    pieces.append(_stack_rows(tail, t, kk))
    cand = jnp.concatenate(pieces, axis=0)
    best = _top_values(cand, kk + 1)
    tau = 0.5 * (best[kk - 1] + best[kk])
    z = jnp.ones_like(tau)
    for r in range(1, kk):
        z = z + jnp.exp(best[r] - best[0])
    thr = jnp.exp(tau - v2[0] - s1) / z
    e1 = jnp.exp(s1 - v1[0])
    e2 = jnp.exp(s2 - v2[0]) / z
    return thr, e1, e2


def _peer_route_body(h_ref, g_ref, wq_ref, keys_ref, xnt_ref, thr_ref, e1_ref, e2_ref):
    nk = PEER_NKEYS
    xnt = _rms(h_ref[...], g_ref[...]).T.astype(BF16)
    xnt_ref[...] = xnt
    qt = jnp.dot(wq_ref[...], xnt, preferred_element_type=F32)
    for hd in range(PEER_HEADS):
        q1 = qt[(2 * hd) * nk:(2 * hd + 1) * nk].astype(BF16)
        q2 = qt[(2 * hd + 1) * nk:(2 * hd + 2) * nk].astype(BF16)
        s1 = jnp.dot(keys_ref[2 * hd], q1, preferred_element_type=F32)
        s2 = jnp.dot(keys_ref[2 * hd + 1], q2, preferred_element_type=F32)
        thr, e1, e2 = _route_head(s1, s2)
        thr_ref[hd] = thr
        e1_ref[hd] = e1
        for c in range(e2.shape[1] // LANE):
            e2_ref[hd, c] = e2[:, c * LANE:(c + 1) * LANE]


def _peer_route(h, g, wq_t, keys, *, tt=256):
    s, d = h.shape
    nq = wq_t.shape[0]
    hk = jax.ShapeDtypeStruct((PEER_HEADS, PEER_NKEYS, s), F32)
    hck = jax.ShapeDtypeStruct((PEER_HEADS, s // LANE, PEER_NKEYS, LANE), F32)
    gate_spec = pl.BlockSpec((PEER_HEADS, PEER_NKEYS, tt), lambda i: (0, 0, i))
    chunk_spec = pl.BlockSpec((PEER_HEADS, tt // LANE, PEER_NKEYS, LANE), lambda i: (0, i, 0, 0))
    return pl.pallas_call(
        _peer_route_body,
        grid=(s // tt,),
        in_specs=[
            pl.BlockSpec((tt, d), lambda i: (i, 0)),
            pl.BlockSpec((1, d), lambda i: (0, 0)),
            pl.BlockSpec((nq, d), lambda i: (0, 0)),
            pl.BlockSpec(keys.shape, lambda i: (0, 0, 0)),
        ],
        out_specs=[pl.BlockSpec((d, tt), lambda i: (0, i)), gate_spec, gate_spec, chunk_spec],
        out_shape=[jax.ShapeDtypeStruct((d, s), BF16), hk, hk, hck],
        compiler_params=_params("parallel"),
        name="peer_route",
    )(h, g.reshape(1, d), wq_t, keys)


def _gelu(x):
    return 0.5 * x * (1.0 + lax.erf(x * (2.0 ** -0.5)))


def _peer_dense_body(xnt_ref, u0_ref, u1_ref, vt0_ref, vt1_ref, thr0_ref, thr1_ref, e10_ref, e11_ref,
                     e20_ref, e21_ref, o_ref, a0_ref, a1_ref, p0_ref, p1_ref, *, tt, eb, nb):
    i = pl.program_id(0)
    nk = PEER_NKEYS
    rb = 4 * SUBLANE
    chunks = tt // LANE

    @pl.when(i == 0)
    def _():
        a1_ref[...] = jnp.zeros(a1_ref.shape, F32)
        p0_ref[...] = jnp.zeros(p0_ref.shape, BF16)

    first_pair = jnp.maximum(2 * i - 2, 0)

    @pl.when(first_pair % nb == 0)
    def _():
        o_ref[...] = jnp.zeros(o_ref.shape, F32)

    def activations(u_ref, a_ref, cs):
        lanes = slice(cs[0] * LANE, (cs[-1] + 1) * LANE)
        a = jnp.dot(u_ref[...], xnt_ref[:, lanes], preferred_element_type=F32)
        for n, c in enumerate(cs):
            a_ref[c] = a[:, n * LANE:(n + 1) * LANE]

    def gated(thr_ref, e1_ref, e2_ref, a_ref, p_ref, cs, row0):
        ks = range(eb // nk)
        for c in cs:
            lanes = slice(c * LANE, (c + 1) * LANE)
            for r in range(0, nk, rb):
                gates = {k: jnp.zeros((rb, LANE), F32) for k in ks}
                for hd in range(PEER_HEADS):
                    e2 = e2_ref[hd, c, r:r + rb, :]
                    for k in ks:
                        thr = thr_ref[hd, 0, row0 + k:row0 + k + 1, lanes]
                        e1 = e1_ref[hd, 0, row0 + k:row0 + k + 1, lanes]
                        gates[k] = gates[k] + jnp.where(e2 >= thr, e2 * e1, 0.0)
                for k in ks:
                    blk = slice(k * nk + r, k * nk + r + rb)
                    p_ref[c, blk, :] = (gates[k] * _gelu(a_ref[c, blk, :])).astype(BF16)

    def project(vt_ref, p_ref, cs):
        lanes = slice(cs[0] * LANE, (cs[-1] + 1) * LANE)
        p = jnp.concatenate([p_ref[c] for c in cs], axis=1)
        o_ref[:, lanes] += jnp.dot(vt_ref[...], p, preferred_element_type=F32)

    def half_step(u_ref, a_new, thr_ref, e1_ref, e2_ref, a_old, p_new, vt_ref, p_old, row0):
        half = chunks // 2
        for cs in (range(half), range(half, chunks)):
            activations(u_ref, a_new, cs)
            gated(thr_ref, e1_ref, e2_ref, a_old, p_new, cs[:len(cs) // 2], row0)
            project(vt_ref, p_old, cs)
            gated(thr_ref, e1_ref, e2_ref, a_old, p_new, cs[len(cs) // 2:], row0)

    rows = eb // nk
    half_step(u0_ref, a0_ref, thr0_ref, e10_ref, e20_ref, a1_ref, p1_ref, vt0_ref, p0_ref, rows)
    half_step(u1_ref, a1_ref, thr1_ref, e11_ref, e21_ref, a0_ref, p0_ref, vt1_ref, p1_ref, 0)


def _peer_dense(xnt, u_all, vt_all, layer, thr, e1, e2, *, tt=512, eb=512):
    d, s = xnt.shape
    ne = u_all.shape[1]
    nk = PEER_NKEYS
    rows = eb // nk
    nb = ne // eb
    n = (s // tt) * nb
    assert nb % 2 == 0 and 2 * rows == SUBLANE
    split = lambda a: a.reshape(PEER_HEADS, nb // 2, SUBLANE, s)

    def pair(lag, half):
        return lambda i: divmod(jnp.clip(2 * i + half - lag, 0, n - 1), nb)

    def u_spec(half):
        return pl.BlockSpec((None, eb, d), lambda i: (layer, pair(0, half)(i)[1], 0))

    def vt_spec(half):
        return pl.BlockSpec((None, d, eb), lambda i: (layer, 0, pair(2, half)(i)[1]))

    def row_spec(half):
        return pl.BlockSpec((PEER_HEADS, 1, SUBLANE, tt),
                            lambda i: (0, pair(1, half)(i)[1] // 2, 0, pair(1, half)(i)[0]))

    def e2_spec(half):
        return pl.BlockSpec((PEER_HEADS, tt // LANE, nk, LANE), lambda i: (0, pair(1, half)(i)[0], 0, 0))

    slab = lambda dtype: pltpu.VMEM((tt // LANE, eb, LANE), dtype)
    return pl.pallas_call(
        functools.partial(_peer_dense_body, tt=tt, eb=eb, nb=nb),
        grid=(n // 2 + 1,),
        in_specs=[
            pl.BlockSpec((d, tt), lambda i: (0, pair(0, 0)(i)[0])),
            u_spec(0), u_spec(1), vt_spec(0), vt_spec(1),
            row_spec(0), row_spec(1), row_spec(0), row_spec(1), e2_spec(0), e2_spec(1),
        ],
        out_specs=pl.BlockSpec((d, tt), lambda i: (0, pair(2, 0)(i)[0])),
        out_shape=jax.ShapeDtypeStruct((d, s), F32),
        scratch_shapes=[slab(F32), slab(F32), slab(BF16), slab(BF16)],
        compiler_params=_params("arbitrary"),
        name="peer_dense",
    )(xnt, u_all, u_all, vt_all, vt_all, split(thr), split(thr), split(e1), split(e1), e2, e2)


def _add_t_body(h_ref, pt_ref, g_ref, o_ref, *, final_norm):
    y = h_ref[...] + pt_ref[...].T
    o_ref[...] = _rms(y, g_ref[...]) if final_norm else y


def _add_transposed(h, pt, g=None, *, tm=512):
    s, d = h.shape
    gain = jnp.ones((1, d), F32) if g is None else g.reshape(1, d)
    return pl.pallas_call(
        functools.partial(_add_t_body, final_norm=g is not None),
        grid=(s // tm,),
        in_specs=[
            pl.BlockSpec((tm, d), lambda i: (i, 0)),
            pl.BlockSpec((d, tm), lambda i: (0, i)),
            pl.BlockSpec((1, d), lambda i: (0, 0)),
        ],
        out_specs=pl.BlockSpec((tm, d), lambda i: (i, 0)),
        out_shape=jax.ShapeDtypeStruct((s, d), F32),
        compiler_params=_params("parallel"),
        name="add_transposed",
    )(h, pt, gain)


def _peer_ffn(h, g, w_query, sub_keys, u_all, vt_all, layer, final_g=None):
    nk = PEER_NKEYS
    keys = sub_keys.reshape(2 * PEER_HEADS, nk, -1).astype(BF16)
    xnt, thr, e1, e2 = _peer_route(h, g, w_query.T.astype(BF16), keys)
    pt = _peer_dense(xnt, u_all, vt_all, layer, thr, e1, e2)
    return _add_transposed(h, pt, final_g)


def _mlstm_prep_body(qp_ref, q_ref, qn_ref, kp_ref, k_ref, kn_ref, v_ref, cwq_ref, cwk_ref,
                     qo_ref, kto_ref, vo_ref, xc_ref, *, tm):
    i = pl.program_id(0)
    last = pl.num_programs(0) - 1
    halo = SUBLANE
    pad = CONV_W // 2

    def conv_silu(prev_ref, x_ref, next_ref, w_ref):
        xc_ref[halo:halo + tm, :] = x_ref[...]
        xc_ref[0:halo, :] = jnp.where(i > 0, prev_ref[...], 0.0)
        xc_ref[halo + tm:2 * halo + tm, :] = jnp.where(i < last, next_ref[...], 0.0)
        acc = xc_ref[halo - pad:halo - pad + tm, :] * w_ref[0:1, :]
        for j in range(1, CONV_W):
            acc = acc + xc_ref[halo - pad + j:halo - pad + j + tm, :] * w_ref[j:j + 1, :]
        return acc * jax.nn.sigmoid(acc)

    qo_ref[...] = conv_silu(qp_ref, q_ref, qn_ref, cwq_ref).astype(BF16)
    kc = conv_silu(kp_ref, k_ref, kn_ref, cwk_ref) * (ML_HEAD_DIM ** -0.5)
    kto_ref[...] = kc.T.astype(BF16)
    vo_ref[...] = v_ref[...].astype(BF16)


def _mlstm_prep(proj, conv_w, *, tm=256):
    s = proj.shape[0]
    w = ML_W
    nb = tm // SUBLANE
    prev = lambda c: pl.BlockSpec((SUBLANE, w), lambda i: (jnp.maximum(i * nb - 1, 0), c))
    main = lambda c: pl.BlockSpec((tm, w), lambda i: (i, c))
    nxt = lambda c: pl.BlockSpec((SUBLANE, w), lambda i: (jnp.minimum((i + 1) * nb, s // SUBLANE - 1), c))
    cw = pl.BlockSpec((CONV_W, w), lambda i: (0, 0))
    return pl.pallas_call(
        functools.partial(_mlstm_prep_body, tm=tm),
        grid=(s // tm,),
        in_specs=[prev(COL_MQ), main(COL_MQ), nxt(COL_MQ), prev(COL_MK), main(COL_MK), nxt(COL_MK),
                  main(COL_MV), cw, cw],
        out_specs=[
            pl.BlockSpec((tm, w), lambda i: (i, 0)),
            pl.BlockSpec((w, tm), lambda i: (0, i)),
            pl.BlockSpec((tm, w), lambda i: (i, 0)),
        ],
        out_shape=[
            jax.ShapeDtypeStruct((s, w), BF16),
            jax.ShapeDtypeStruct((w, s), BF16),
            jax.ShapeDtypeStruct((s, w), BF16),
        ],
        scratch_shapes=[pltpu.VMEM((tm + 2 * SUBLANE, w), F32)],
        compiler_params=_params("parallel"),
        name="mlstm_prep",
    )(proj, proj, proj, proj, proj, proj, proj, conv_w[:, :w], conv_w[:, w:])


def _log_sigmoid(x):
    return jnp.minimum(x, 0.0) - jnp.log1p(jnp.exp(-jnp.abs(x)))


def _mlstm_direction(q_ref, kt_ref, v_ref, g_ref, bias_ref, o_ref, s_ref, m_ref, *, backward):
    L = ML_CHUNK
    d = ML_HEAD_DIM
    row = lax.broadcasted_iota(jnp.int32, (L, L), 0)
    col = lax.broadcasted_iota(jnp.int32, (L, L), 1)
    keep = (col >= row) if backward else (col <= row)
    tri = keep.astype(BF16)
    ones = jnp.ones((L, LANE), BF16)

    gates = g_ref[...] + bias_ref[...]
    logf = _log_sigmoid(gates)
    hi = logf.astype(BF16)
    lo = (logf - hi.astype(F32)).astype(BF16)
    bcol = jnp.dot(tri, hi, preferred_element_type=F32) + jnp.dot(tri, lo, preferred_element_type=F32)
    gates_t = gates.T
    b_t = bcol.T
    base = 2 * ML_HEADS if backward else 0
    for h in range(ML_HEADS):
        idx = base // 2 + h
        ci, cf = base + h, base + ML_HEADS + h
        i_row = gates_t[ci:ci + 1, :]
        b_row = b_t[cf:cf + 1, :]
        b_col = bcol[:, cf:cf + 1]
        g_tot = b_row[:, 0:1] if backward else b_row[:, L - 1:L]
        a_row = g_tot - b_row + i_row
        a_max = jnp.max(a_row, axis=1, keepdims=True)
        w_row = jnp.exp(a_row - a_max)

        q = q_ref[:, h * d:(h + 1) * d]
        kt = kt_ref[h * d:(h + 1) * d, :]
        v_aug = jnp.concatenate([v_ref[:, h * d:(h + 1) * d], ones], axis=1)
        s_prev = s_ref[idx]
        m_prev = m_ref[idx][:, 0:1]

        dlog = jnp.where(keep, b_col + (i_row - b_row), NEG_INF)
        inter_log = b_col + m_prev
        m_j = jnp.maximum(inter_log, jnp.max(dlog, axis=1, keepdims=True))
        sqk = jnp.dot(q, kt, preferred_element_type=F32) * jnp.exp(dlog - m_j)
        inter_w = jnp.exp(inter_log - m_j)
        tot = (inter_w * jnp.dot(q, s_prev.astype(BF16), preferred_element_type=F32)
               + jnp.dot(sqk.astype(BF16), v_aug, preferred_element_type=F32))
        den = tot[:, d:]
        inv = 1.0 / jnp.maximum(jnp.abs(den), jnp.exp(-m_j))
        o_ref[:, h * d:(h + 1) * d] = tot[:, :d] * jnp.concatenate([inv] * (d // LANE), axis=1)

        ktw = (kt.astype(F32) * w_row).astype(BF16)
        upd = jnp.dot(ktw, v_aug, preferred_element_type=F32)
        m_new = jnp.maximum(g_tot + m_prev, a_max)
        s_ref[idx] = jnp.exp(g_tot + m_prev - m_new) * s_prev + jnp.exp(a_max - m_new) * upd
        m_ref[idx] = jnp.broadcast_to(m_new, (1, LANE))


def _mlstm_body(qf_ref, ktf_ref, vf_ref, gf_ref, qb_ref, ktb_ref, vb_ref, gb_ref, bias_ref,
                hf_ref, hb_ref, s_ref, m_ref):
    @pl.when(pl.program_id(0) == 0)
    def _():
        s_ref[...] = jnp.zeros(s_ref.shape, F32)
        m_ref[...] = jnp.zeros(m_ref.shape, F32)

    _mlstm_direction(qf_ref, ktf_ref, vf_ref, gf_ref, bias_ref, hf_ref, s_ref, m_ref, backward=False)
    _mlstm_direction(qb_ref, ktb_ref, vb_ref, gb_ref, bias_ref, hb_ref, s_ref, m_ref, backward=True)


def _mlstm_scan(mq, mkt, mv, gates, b_gate):
    s, w = mq.shape
    L = ML_CHUNK
    nc = s // L
    d = ML_HEAD_DIM
    bias = jnp.pad(b_gate, (0, LANE - b_gate.shape[0])).reshape(1, LANE)
    fwd = lambda c: c
    bwd = lambda c: nc - 1 - c
    rows = lambda ix, n: pl.BlockSpec((L, n), lambda c: (ix(c), 0))
    cols = lambda ix: pl.BlockSpec((w, L), lambda c: (0, ix(c)))
    return pl.pallas_call(
        _mlstm_body,
        grid=(nc,),
        in_specs=[rows(fwd, w), cols(fwd), rows(fwd, w), rows(fwd, LANE),
                  rows(bwd, w), cols(bwd), rows(bwd, w), rows(bwd, LANE),
                  pl.BlockSpec((1, LANE), lambda c: (0, 0))],
        out_specs=[rows(fwd, w), rows(bwd, w)],
        out_shape=[jax.ShapeDtypeStruct((s, w), F32)] * 2,
        scratch_shapes=[pltpu.VMEM((2 * ML_HEADS, d, d + LANE), F32), pltpu.VMEM((2 * ML_HEADS, 1, LANE), F32)],
        compiler_params=_params("arbitrary"),
        name="mlstm_scan",
    )(mq, mkt, mv, gates, mq, mkt, mv, gates, bias)


def _mix_out_body(ya_ref, hf_ref, hb_ref, mo_ref, mg_ref, wa_ref, wm_ref, r_ref, o_ref, yml_ref):
    @pl.when(pl.program_id(1) == 0)
    def _():
        d = ML_HEAD_DIM
        for h in range(ML_HEADS):
            sl = slice(h * d, (h + 1) * d)
            hm = _rms(hf_ref[:, sl] + hb_ref[:, sl], mg_ref[:, sl])
            yml_ref[:, sl] = (hm * jax.nn.sigmoid(mo_ref[:, sl])).astype(BF16)

    o_ref[...] = (r_ref[...] + jnp.dot(ya_ref[...], wa_ref[...], preferred_element_type=F32)
                  + jnp.dot(yml_ref[...], wm_ref[...], preferred_element_type=F32))


def _mix_out(y_att, hf, hb, proj, ml_gain, w_att, w_ml, res, *, tm=512, tn=1024):
    s, n = res.shape
    wa, wm = w_att.shape[0], w_ml.shape[0]
    return pl.pallas_call(
        _mix_out_body,
        grid=(s // tm, n // tn),
        in_specs=[
            pl.BlockSpec((tm, wa), lambda i, j: (i, 0)),
            pl.BlockSpec((tm, wm), lambda i, j: (i, 0)),
            pl.BlockSpec((tm, wm), lambda i, j: (i, 0)),
            pl.BlockSpec((tm, wm), lambda i, j: (i, COL_MO)),
            pl.BlockSpec((1, wm), lambda i, j: (0, 0)),
            pl.BlockSpec((wa, tn), lambda i, j: (0, j)),
            pl.BlockSpec((wm, tn), lambda i, j: (0, j)),
            pl.BlockSpec((tm, tn), lambda i, j: (i, j)),
        ],
        out_specs=pl.BlockSpec((tm, tn), lambda i, j: (i, j)),
        out_shape=jax.ShapeDtypeStruct((s, n), F32),
        scratch_shapes=[pltpu.VMEM((tm, wm), BF16)],
        compiler_params=_params("parallel", "arbitrary"),
        name="mix_out",
    )(y_att, hf, hb, proj, ml_gain.reshape(1, wm), w_att, w_ml, res)


def kernel(x, norm_mix_g, norm_ffn_g, w_in, b_gate, conv_w, q_gain, k_gain, ml_gain, w_out, pool_w,
           layer_scale, peer_wq, peer_keys, peer_u, peer_v, norm_f):
    b, s, d = x.shape
    h = x.reshape(b * s, d)

    wi = w_in[0]
    kv_end = ATT_Q + 2 * ATT_KV
    w_main = jnp.concatenate([wi[:, :ATT_Q], wi[:, kv_end:MAIN_W], wi[:, ATT_Q:kv_end]], axis=1).astype(BF16)
    w_gate = jnp.pad(wi[:, MAIN_W:], ((0, 0), (0, LANE - (wi.shape[1] - MAIN_W)))).astype(BF16)
    proj, gates = _norm_matmul(h, norm_mix_g[0], w_main, w_gate)
    qt, k, vt = _attn_prep(proj, q_gain[0], k_gain[0])
    y_att = _attention(qt, k, vt)
    mq, mkt, mv = _mlstm_prep(proj, conv_w[0])
    hf, hb = _mlstm_scan(mq, mkt, mv, gates, b_gate[0])
    wo = w_out[0].astype(BF16)
    h = _mix_out(y_att, hf, hb, proj, ml_gain[0], wo[:ATT_Q], wo[ATT_Q:], h)
    u_all = peer_u.astype(BF16)
    vt_all = jnp.swapaxes(peer_v, 1, 2).astype(BF16)
    h = _peer_ffn(h, norm_ffn_g[0], peer_wq[0], peer_keys[0], u_all, vt_all, 0)

    h = _pool_mixer(h, norm_mix_g[1], pool_w[0], layer_scale[0])
    h = _peer_ffn(h, norm_ffn_g[1], peer_wq[1], peer_keys[1], u_all, vt_all, 1, final_g=norm_f)
    return h.reshape(b, s, d)
```

```python
import functools

import jax
import jax.numpy as jnp
from jax import lax
from jax.experimental import pallas as pl
from jax.experimental.pallas import tpu as pltpu

F32 = jnp.float32
BF16 = jnp.bfloat16
EPS = 1e-6
LOG2E = 1.4426950408889634

GRID_W = 64
ROPE_THETA = 10000.0
ATT_HEADS = 8
ATT_KV_HEADS = 2
ATT_HEAD_DIM = 128
ML_HEADS = 4
ML_HEAD_DIM = 256
ML_CHUNK = 128
CONV_W = 5
POOL_WINDOWS = (2, 4, 8, 16)
PEER_HEADS = 8
PEER_NKEYS = 128
PEER_TOPK = 16
PEER_BLOCK = 128

ATT_Q = ATT_HEADS * ATT_HEAD_DIM
ATT_KV = ATT_KV_HEADS * ATT_HEAD_DIM
ML_W = ML_HEADS * ML_HEAD_DIM
MAIN_W = ATT_Q + 2 * ATT_KV + 4 * ML_W
assert ATT_Q == ML_W
COL_AQ, COL_MQ, COL_MK, COL_MV, COL_MO = 0, 1, 2, 3, 4
COL_AK = (ATT_Q + 4 * ML_W) // ATT_KV
COL_AV = COL_AK + 1
LANE = 128
SUBLANE = 8
VMEM_LIMIT = 48 * 1024 * 1024


def _params(*sem):
    return pltpu.CompilerParams(dimension_semantics=sem, vmem_limit_bytes=VMEM_LIMIT)


def _rms(x, g):
    return x * lax.rsqrt(jnp.mean(x * x, axis=-1, keepdims=True) + EPS) * g


def _norm_mm_body(x_ref, g_ref, w_ref, wg_ref, o_ref, og_ref, xn_ref):
    @pl.when(pl.program_id(1) == 0)
    def _():
        xn = _rms(x_ref[...], g_ref[...]).astype(BF16)
        xn_ref[...] = xn
        og_ref[...] = jnp.dot(xn, wg_ref[...], preferred_element_type=F32)

    o_ref[...] = jnp.dot(xn_ref[...], w_ref[...], preferred_element_type=F32)


def _norm_matmul(x, g, w, wg, *, tm=512, tn=1408):
    s, d = x.shape
    n = w.shape[1]
    return pl.pallas_call(
        _norm_mm_body,
        grid=(s // tm, n // tn),
        in_specs=[
            pl.BlockSpec((tm, d), lambda i, j: (i, 0)),
            pl.BlockSpec((1, d), lambda i, j: (0, 0)),
            pl.BlockSpec((d, tn), lambda i, j: (0, j)),
            pl.BlockSpec((d, LANE), lambda i, j: (0, 0)),
        ],
        out_specs=[
            pl.BlockSpec((tm, tn), lambda i, j: (i, j)),
            pl.BlockSpec((tm, LANE), lambda i, j: (i, 0)),
        ],
        out_shape=[jax.ShapeDtypeStruct((s, n), F32), jax.ShapeDtypeStruct((s, LANE), F32)],
        scratch_shapes=[pltpu.VMEM((tm, d), BF16)],
        compiler_params=_params("parallel", "arbitrary"),
        name="norm_matmul",
    )(x, g.reshape(1, d), w, wg)


def _rope_tables(s):
    rows = s // GRID_W
    r, c = jnp.meshgrid(jnp.arange(rows), jnp.arange(GRID_W), indexing="ij")
    pos = jnp.stack([r.reshape(-1), c.reshape(-1)], axis=-1).astype(F32)
    nf = ATT_HEAD_DIM // 4
    inv = ROPE_THETA ** (-jnp.arange(nf, dtype=F32) / nf)
    ang = pos[:, :, None] * inv
    cos, sin = jnp.cos(ang), jnp.sin(ang)
    ctab = jnp.stack([cos, cos], axis=2).reshape(s, ATT_HEAD_DIM)
    stab = jnp.stack([-sin, sin], axis=2).reshape(s, ATT_HEAD_DIM)
    return ctab, stab


def _rope(y, ctab, stab, first_half):
    nf = ATT_HEAD_DIM // 4
    swapped = jnp.where(first_half, pltpu.roll(y, ATT_HEAD_DIM - nf, 1), pltpu.roll(y, nf, 1))
    return y * ctab + swapped * stab


def _attn_prep_body(aq_ref, ak_ref, av_ref, c_ref, s_ref, qg_ref, kg_ref, qt_ref, k_ref, vt_ref):
    d = ATT_HEAD_DIM
    ctab, stab = c_ref[...], s_ref[...]
    lane = lax.broadcasted_iota(jnp.int32, ctab.shape, 1)
    first_half = (lane % (d // 2)) < (d // 4)
    scale = d ** -0.5 * LOG2E
    for h in range(ATT_HEADS):
        y = _rms(aq_ref[:, h * d:(h + 1) * d], qg_ref[...])
        qt_ref[h] = (_rope(y, ctab, stab, first_half) * scale).T.astype(BF16)
    for h in range(ATT_KV_HEADS):
        y = _rms(ak_ref[:, h * d:(h + 1) * d], kg_ref[...])
        k_ref[h] = _rope(y, ctab, stab, first_half).astype(BF16)
        vt_ref[h] = av_ref[:, h * d:(h + 1) * d].T.astype(BF16)


def _attn_prep(proj, q_gain, k_gain, *, tm=512):
    s = proj.shape[0]
    d = ATT_HEAD_DIM
    ctab, stab = _rope_tables(s)
    return pl.pallas_call(
        _attn_prep_body,
        grid=(s // tm,),
        in_specs=[
            pl.BlockSpec((tm, ATT_Q), lambda i: (i, COL_AQ)),
            pl.BlockSpec((tm, ATT_KV), lambda i: (i, COL_AK)),
            pl.BlockSpec((tm, ATT_KV), lambda i: (i, COL_AV)),
            pl.BlockSpec((tm, d), lambda i: (i, 0)),
            pl.BlockSpec((tm, d), lambda i: (i, 0)),
            pl.BlockSpec((1, d), lambda i: (0, 0)),
            pl.BlockSpec((1, d), lambda i: (0, 0)),
        ],
        out_specs=[
            pl.BlockSpec((ATT_HEADS, d, tm), lambda i: (0, 0, i)),
            pl.BlockSpec((ATT_KV_HEADS, tm, d), lambda i: (0, i, 0)),
            pl.BlockSpec((ATT_KV_HEADS, d, tm), lambda i: (0, 0, i)),
        ],
        out_shape=[
            jax.ShapeDtypeStruct((ATT_HEADS, d, s), BF16),
            jax.ShapeDtypeStruct((ATT_KV_HEADS, s, d), BF16),
            jax.ShapeDtypeStruct((ATT_KV_HEADS, d, s), BF16),
        ],
        compiler_params=_params("parallel"),
        name="attn_prep",
    )(proj, proj, proj, ctab, stab, q_gain.reshape(1, d), k_gain.reshape(1, d))


def _col_reduce(x, pair, final):
    fold = 8 * SUBLANE
    if x.shape[0] > fold:
        acc = x[:fold]
        for r in range(fold, x.shape[0], fold):
            acc = pair(acc, x[r:r + fold])
        x = acc
    while x.shape[0] > SUBLANE:
        half = x.shape[0] // 2
        x = pair(x[:half], x[half:])
    return final(x, axis=0, keepdims=True)


def _attn_body(qt_ref, k_ref, vt_ref, o_ref, q_all, m_ref, l_ref, acc_ref, s_ref, p_ref, *, tq, grp, wide):
    j = pl.program_id(2)
    d = ATT_HEAD_DIM
    mq = grp * tq
    chunks = mq // LANE

    @pl.when(j == 0)
    def _():
        for h in range(grp):
            for t in range(0, tq, wide):
                q_all[(h * tq + t) // wide] = qt_ref[h, :, t:t + wide]
        m_ref[...] = jnp.full(m_ref.shape, -jnp.inf, F32)
        l_ref[...] = jnp.zeros(l_ref.shape, F32)
        acc_ref[...] = jnp.zeros(acc_ref.shape, F32)

    k = k_ref[0]
    vt = vt_ref[0]
    per = wide // LANE
    groups = mq // wide

    for g in range(groups):
        st = jnp.dot(k, q_all[g], preferred_element_type=F32)
        for c in range(per):
            s_ref[g * per + c] = st[:, c * LANE:(c + 1) * LANE]
    alphas = []
    for ci in range(chunks):
        sc = s_ref[ci]
        m_prev = m_ref[ci]
        m_new = jnp.maximum(m_prev, _col_reduce(sc, jnp.maximum, jnp.max))
        alpha = jnp.exp2(m_prev - m_new)
        p = jnp.exp2(sc - m_new)
        l_ref[ci] = alpha * l_ref[ci] + _col_reduce(p, jnp.add, jnp.sum)
        m_ref[ci] = m_new
        p_ref[ci] = p.astype(BF16)
        alphas.append(alpha)
    pt = jnp.concatenate([p_ref[ci] for ci in range(chunks)], axis=1)
    pv = jnp.dot(vt, pt, preferred_element_type=F32)
    for ci in range(chunks):
        acc_ref[ci] = alphas[ci] * acc_ref[ci] + pv[:, ci * LANE:(ci + 1) * LANE]

    @pl.when(j == pl.num_programs(2) - 1)
    def _():
        per_head = tq // LANE
        for c in range(chunks):
            h, tb = divmod(c, per_head)
            out_t = acc_ref[c] / l_ref[c]
            o_ref[tb * LANE:(tb + 1) * LANE, h * d:(h + 1) * d] = out_t.T.astype(o_ref.dtype)


def _attention(qt, k, vt, *, tq=1024, tk=1024, wide=2 * LANE):
    _, s, d = k.shape
    grp = ATT_HEADS // ATT_KV_HEADS
    mq = grp * tq
    return pl.pallas_call(
        functools.partial(_attn_body, tq=tq, grp=grp, wide=wide),
        grid=(ATT_KV_HEADS, s // tq, s // tk),
        in_specs=[
            pl.BlockSpec((grp, d, tq), lambda h, i, j: (h, 0, i)),
            pl.BlockSpec((1, tk, d), lambda h, i, j: (h, j, 0)),
            pl.BlockSpec((1, d, tk), lambda h, i, j: (h, 0, j)),
        ],
        out_specs=pl.BlockSpec((tq, grp * d), lambda h, i, j: (i, h)),
        out_shape=jax.ShapeDtypeStruct((s, ATT_HEADS * d), BF16),
        scratch_shapes=[
            pltpu.VMEM((mq // wide, d, wide), BF16),
            pltpu.VMEM((mq // LANE, 1, LANE), F32),
            pltpu.VMEM((mq // LANE, 1, LANE), F32),
            pltpu.VMEM((mq // LANE, d, LANE), F32),
            pltpu.VMEM((mq // LANE, tk, LANE), F32),
            pltpu.VMEM((mq // LANE, tk, LANE), BF16),
        ],
        compiler_params=_params("parallel", "parallel", "arbitrary"),
        name="attention",
    )(qt, k, vt)


def _pool_body(prev_ref, x_ref, next_ref, g_ref, w_ref, ls_ref, o_ref, xc_ref, *, tm, s_total):
    i = pl.program_id(0)
    halo = SUBLANE
    g = g_ref[...]
    xm = x_ref[...]
    xc_ref[halo:halo + tm, :] = _rms(xm, g)
    xc_ref[0:halo, :] = jnp.where(i > 0, _rms(prev_ref[...], g), 0.0)
    xc_ref[halo + tm:2 * halo + tm, :] = jnp.where(i < pl.num_programs(0) - 1, _rms(next_ref[...], g), 0.0)
    row = i * tm + lax.broadcasted_iota(jnp.int32, (tm, 1), 0)
    p = xm.shape[1] // len(POOL_WINDOWS)
    outs = []
    for gi, window in enumerate(POOL_WINDOWS):
        hf = window // 2
        cols = slice(gi * p, (gi + 1) * p)
        acc = xc_ref[halo - hf:halo - hf + tm, cols]
        for o in range(-hf + 1, hf):
            acc = acc + xc_ref[halo + o:halo + o + tm, cols]
        cnt = (jnp.minimum(row + hf, s_total) - jnp.maximum(row - hf, 0)).astype(F32)
        pooled = acc / cnt - xc_ref[halo:halo + tm, cols]
        outs.append(jnp.dot(pooled.astype(BF16), w_ref[gi], preferred_element_type=F32))
    o_ref[...] = xm + jnp.concatenate(outs, axis=-1) * ls_ref[...]


def _pool_mixer(h, g, pool_w, layer_scale, *, tm=256):
    s, d = h.shape
    nb = tm // SUBLANE
    ng, p, _ = pool_w.shape
    return pl.pallas_call(
        functools.partial(_pool_body, tm=tm, s_total=s),
        grid=(s // tm,),
        in_specs=[
            pl.BlockSpec((SUBLANE, d), lambda i: (jnp.maximum(i * nb - 1, 0), 0)),
            pl.BlockSpec((tm, d), lambda i: (i, 0)),
            pl.BlockSpec((SUBLANE, d), lambda i: (jnp.minimum((i + 1) * nb, s // SUBLANE - 1), 0)),
            pl.BlockSpec((1, d), lambda i: (0, 0)),
            pl.BlockSpec((ng, p, p), lambda i: (0, 0, 0)),
            pl.BlockSpec((1, d), lambda i: (0, 0)),
        ],
        out_specs=pl.BlockSpec((tm, d), lambda i: (i, 0)),
        out_shape=jax.ShapeDtypeStruct((s, d), F32),
        scratch_shapes=[pltpu.VMEM((tm + 2 * SUBLANE, d), F32)],
        compiler_params=_params("parallel"),
        name="pool_mixer",
    )(h, h, h, g.reshape(1, d), pool_w.astype(BF16), layer_scale.reshape(1, d))


NEG_INF = float("-inf")


def _top_values(x, count):
    rows = []
    for _ in range(count):
        m = jnp.max(x, axis=0, keepdims=True)
        rows.append(m)
        x = jnp.where(x == m, NEG_INF, x)
    return rows


def _stack_rows(rows, t, n=None):
    n = len(rows) if n is None else n
    rid = lax.broadcasted_iota(jnp.int32, (n, t), 0)
    out = jnp.full((n, t), NEG_INF, F32)
    for r, row in enumerate(rows):
        out = jnp.where(rid == r, row, out)
    return out


def _route_head(s1, s2):
    t = s1.shape[1]
    kk = PEER_TOPK
    v1 = _top_values(s1, kk + 1)
    v2 = _top_values(s2, kk + 1)
    v2_all = _stack_rows(v2[:kk], t)
    pieces = []
    for a in range(kk // 2):
        nb = kk // (a + 1)
        rows = kk if nb > SUBLANE else SUBLANE
        rid = lax.broadcasted_iota(jnp.int32, (rows, t), 0)
        pieces.append(jnp.where(rid < nb, v1[a] + v2_all[:rows], NEG_INF))
    tail = [v1[a] + v2[0] for a in range(kk // 2, kk + 1)] + [v1[0] + v2[kk]]
    pieces.append(_stack_rows(tail, t, kk))
    cand = jnp.concatenate(pieces, axis=0)
    best = _top_values(cand, kk + 1)
    tau = 0.5 * (best[kk - 1] + best[kk])
    z = jnp.ones_like(tau)
    for r in range(1, kk):
        z = z + jnp.exp(best[r] - best[0])
    thr = jnp.exp(tau - v2[0] - s1) / z
    e1 = jnp.exp(s1 - v1[0])
    e2 = jnp.exp(s2 - v2[0]) / z
    return thr, e1, e2


def _peer_route_body(h_ref, g_ref, wq_ref, keys_ref, xnt_ref, thr_ref, e1_ref, e2_ref):
    nk = PEER_NKEYS
    xnt = _rms(h_ref[...], g_ref[...]).T.astype(BF16)
    xnt_ref[...] = xnt
    qt = jnp.dot(wq_ref[...], xnt, preferred_element_type=F32)
    for hd in range(PEER_HEADS):
        q1 = qt[(2 * hd) * nk:(2 * hd + 1) * nk].astype(BF16)
        q2 = qt[(2 * hd + 1) * nk:(2 * hd + 2) * nk].astype(BF16)
        s1 = jnp.dot(keys_ref[2 * hd], q1, preferred_element_type=F32)
        s2 = jnp.dot(keys_ref[2 * hd + 1], q2, preferred_element_type=F32)
        thr, e1, e2 = _route_head(s1, s2)
        thr_ref[hd] = thr
        e1_ref[hd] = e1
        for c in range(e2.shape[1] // LANE):
            e2_ref[hd, c] = e2[:, c * LANE:(c + 1) * LANE]


def _peer_route(h, g, wq_t, keys, *, tt=256):
    s, d = h.shape
    nq = wq_t.shape[0]
    hk = jax.ShapeDtypeStruct((PEER_HEADS, PEER_NKEYS, s), F32)
    hck = jax.ShapeDtypeStruct((PEER_HEADS, s // LANE, PEER_NKEYS, LANE), F32)
    gate_spec = pl.BlockSpec((PEER_HEADS, PEER_NKEYS, tt), lambda i: (0, 0, i))
    chunk_spec = pl.BlockSpec((PEER_HEADS, tt // LANE, PEER_NKEYS, LANE), lambda i: (0, i, 0, 0))
    return pl.pallas_call(
        _peer_route_body,
        grid=(s // tt,),
        in_specs=[
            pl.BlockSpec((tt, d), lambda i: (i, 0)),
            pl.BlockSpec((1, d), lambda i: (0, 0)),
            pl.BlockSpec((nq, d), lambda i: (0, 0)),
            pl.BlockSpec(keys.shape, lambda i: (0, 0, 0)),
        ],
        out_specs=[pl.BlockSpec((d, tt), lambda i: (0, i)), gate_spec, gate_spec, chunk_spec],
        out_shape=[jax.ShapeDtypeStruct((d, s), BF16), hk, hk, hck],
        compiler_params=_params("parallel"),
        name="peer_route",
    )(h, g.reshape(1, d), wq_t, keys)


def _gelu(x):
    return 0.5 * x * (1.0 + lax.erf(x * (2.0 ** -0.5)))


def _peer_dense_body(xnt_ref, u0_ref, u1_ref, vt0_ref, vt1_ref, thr0_ref, thr1_ref, e10_ref, e11_ref,
                     e20_ref, e21_ref, o_ref, a0_ref, a1_ref, p0_ref, p1_ref, *, tt, eb, nb):
    i = pl.program_id(0)
    nk = PEER_NKEYS
    rb = 4 * SUBLANE
    chunks = tt // LANE

    @pl.when(i == 0)
    def _():
        a1_ref[...] = jnp.zeros(a1_ref.shape, F32)
        p0_ref[...] = jnp.zeros(p0_ref.shape, BF16)

    first_pair = jnp.maximum(2 * i - 2, 0)

    @pl.when(first_pair % nb == 0)
    def _():
        o_ref[...] = jnp.zeros(o_ref.shape, F32)

    def activations(u_ref, a_ref, cs):
        lanes = slice(cs[0] * LANE, (cs[-1] + 1) * LANE)
        a = jnp.dot(u_ref[...], xnt_ref[:, lanes], preferred_element_type=F32)
        for n, c in enumerate(cs):
            a_ref[c] = a[:, n * LANE:(n + 1) * LANE]

    def gated(thr_ref, e1_ref, e2_ref, a_ref, p_ref, cs, row0):
        ks = range(eb // nk)
        for c in cs:
            lanes = slice(c * LANE, (c + 1) * LANE)
            for r in range(0, nk, rb):
                gates = {k: jnp.zeros((rb, LANE), F32) for k in ks}
                for hd in range(PEER_HEADS):
                    e2 = e2_ref[hd, c, r:r + rb, :]
                    for k in ks:
                        thr = thr_ref[hd, 0, row0 + k:row0 + k + 1, lanes]
                        e1 = e1_ref[hd, 0, row0 + k:row0 + k + 1, lanes]
                        gates[k] = gates[k] + jnp.where(e2 >= thr, e2 * e1, 0.0)
                for k in ks:
                    blk = slice(k * nk + r, k * nk + r + rb)
                    p_ref[c, blk, :] = (gates[k] * _gelu(a_ref[c, blk, :])).astype(BF16)

    def project(vt_ref, p_ref, cs):
        lanes = slice(cs[0] * LANE, (cs[-1] + 1) * LANE)
        p = jnp.concatenate([p_ref[c] for c in cs], axis=1)
        o_ref[:, lanes] += jnp.dot(vt_ref[...], p, preferred_element_type=F32)

    def half_step(u_ref, a_new, thr_ref, e1_ref, e2_ref, a_old, p_new, vt_ref, p_old, row0):
        half = chunks // 2
        for cs in (range(half), range(half, chunks)):
            activations(u_ref, a_new, cs)
            gated(thr_ref, e1_ref, e2_ref, a_old, p_new, cs[:len(cs) // 2], row0)
            project(vt_ref, p_old, cs)
            gated(thr_ref, e1_ref, e2_ref, a_old, p_new, cs[len(cs) // 2:], row0)

    rows = eb // nk
    half_step(u0_ref, a0_ref, thr0_ref, e10_ref, e20_ref, a1_ref, p1_ref, vt0_ref, p0_ref, rows)
    half_step(u1_ref, a1_ref, thr1_ref, e11_ref, e21_ref, a0_ref, p0_ref, vt1_ref, p1_ref, 0)


def _peer_dense(xnt, u_all, vt_all, layer, thr, e1, e2, *, tt=512, eb=512):
    d, s = xnt.shape
    ne = u_all.shape[1]
    nk = PEER_NKEYS
    rows = eb // nk
    nb = ne // eb
    n = (s // tt) * nb
    assert nb % 2 == 0 and 2 * rows == SUBLANE
    split = lambda a: a.reshape(PEER_HEADS, nb // 2, SUBLANE, s)

    def pair(lag, half):
        return lambda i: divmod(jnp.clip(2 * i + half - lag, 0, n - 1), nb)

    def u_spec(half):
        return pl.BlockSpec((None, eb, d), lambda i: (layer, pair(0, half)(i)[1], 0))

    def vt_spec(half):
        return pl.BlockSpec((None, d, eb), lambda i: (layer, 0, pair(2, half)(i)[1]))

    def row_spec(half):
        return pl.BlockSpec((PEER_HEADS, 1, SUBLANE, tt),
                            lambda i: (0, pair(1, half)(i)[1] // 2, 0, pair(1, half)(i)[0]))

    def e2_spec(half):
        return pl.BlockSpec((PEER_HEADS, tt // LANE, nk, LANE), lambda i: (0, pair(1, half)(i)[0], 0, 0))

    slab = lambda dtype: pltpu.VMEM((tt // LANE, eb, LANE), dtype)
    return pl.pallas_call(
        functools.partial(_peer_dense_body, tt=tt, eb=eb, nb=nb),
        grid=(n // 2 + 1,),
        in_specs=[
            pl.BlockSpec((d, tt), lambda i: (0, pair(0, 0)(i)[0])),
            u_spec(0), u_spec(1), vt_spec(0), vt_spec(1),
            row_spec(0), row_spec(1), row_spec(0), row_spec(1), e2_spec(0), e2_spec(1),
        ],
        out_specs=pl.BlockSpec((d, tt), lambda i: (0, pair(2, 0)(i)[0])),
        out_shape=jax.ShapeDtypeStruct((d, s), F32),
        scratch_shapes=[slab(F32), slab(F32), slab(BF16), slab(BF16)],
        compiler_params=_params("arbitrary"),
        name="peer_dense",
    )(xnt, u_all, u_all, vt_all, vt_all, split(thr), split(thr), split(e1), split(e1), e2, e2)


def _add_t_body(h_ref, pt_ref, g_ref, o_ref, *, final_norm):
    y = h_ref[...] + pt_ref[...].T
    o_ref[...] = _rms(y, g_ref[...]) if final_norm else y


def _add_transposed(h, pt, g=None, *, tm=512):
    s, d = h.shape
    gain = jnp.ones((1, d), F32) if g is None else g.reshape(1, d)
    return pl.pallas_call(
        functools.partial(_add_t_body, final_norm=g is not None),
        grid=(s // tm,),
        in_specs=[
            pl.BlockSpec((tm, d), lambda i: (i, 0)),
            pl.BlockSpec((d, tm), lambda i: (0, i)),
            pl.BlockSpec((1, d), lambda i: (0, 0)),
        ],
        out_specs=pl.BlockSpec((tm, d), lambda i: (i, 0)),
        out_shape=jax.ShapeDtypeStruct((s, d), F32),
        compiler_params=_params("parallel"),
        name="add_transposed",
    )(h, pt, gain)


def _peer_ffn(h, g, w_query, sub_keys, u_all, vt_all, layer, final_g=None):
    nk = PEER_NKEYS
    keys = sub_keys.reshape(2 * PEER_HEADS, nk, -1).astype(BF16)
    xnt, thr, e1, e2 = _peer_route(h, g, w_query.T.astype(BF16), keys)
    pt = _peer_dense(xnt, u_all, vt_all, layer, thr, e1, e2)
    return _add_transposed(h, pt, final_g)


def _mlstm_prep_body(qp_ref, q_ref, qn_ref, kp_ref, k_ref, kn_ref, v_ref, cwq_ref, cwk_ref,
                     qo_ref, kto_ref, vo_ref, xc_ref, *, tm):
    i = pl.program_id(0)
    last = pl.num_programs(0) - 1
    halo = SUBLANE
    pad = CONV_W // 2

    def conv_silu(prev_ref, x_ref, next_ref, w_ref):
        xc_ref[halo:halo + tm, :] = x_ref[...]
        xc_ref[0:halo, :] = jnp.where(i > 0, prev_ref[...], 0.0)
        xc_ref[halo + tm:2 * halo + tm, :] = jnp.where(i < last, next_ref[...], 0.0)
        acc = xc_ref[halo - pad:halo - pad + tm, :] * w_ref[0:1, :]
        for j in range(1, CONV_W):
            acc = acc + xc_ref[halo - pad + j:halo - pad + j + tm, :] * w_ref[j:j + 1, :]
        return acc * jax.nn.sigmoid(acc)

    qo_ref[...] = conv_silu(qp_ref, q_ref, qn_ref, cwq_ref).astype(BF16)
    kc = conv_silu(kp_ref, k_ref, kn_ref, cwk_ref) * (ML_HEAD_DIM ** -0.5)
    kto_ref[...] = kc.T.astype(BF16)
    vo_ref[...] = v_ref[...].astype(BF16)


def _mlstm_prep(proj, conv_w, *, tm=256):
    s = proj.shape[0]
    w = ML_W
    nb = tm // SUBLANE
    prev = lambda c: pl.BlockSpec((SUBLANE, w), lambda i: (jnp.maximum(i * nb - 1, 0), c))
    main = lambda c: pl.BlockSpec((tm, w), lambda i: (i, c))
    nxt = lambda c: pl.BlockSpec((SUBLANE, w), lambda i: (jnp.minimum((i + 1) * nb, s // SUBLANE - 1), c))
    cw = pl.BlockSpec((CONV_W, w), lambda i: (0, 0))
    return pl.pallas_call(
        functools.partial(_mlstm_prep_body, tm=tm),
        grid=(s // tm,),
        in_specs=[prev(COL_MQ), main(COL_MQ), nxt(COL_MQ), prev(COL_MK), main(COL_MK), nxt(COL_MK),
                  main(COL_MV), cw, cw],
        out_specs=[
            pl.BlockSpec((tm, w), lambda i: (i, 0)),
            pl.BlockSpec((w, tm), lambda i: (0, i)),
            pl.BlockSpec((tm, w), lambda i: (i, 0)),
        ],
        out_shape=[
            jax.ShapeDtypeStruct((s, w), BF16),
            jax.ShapeDtypeStruct((w, s), BF16),
            jax.ShapeDtypeStruct((s, w), BF16),
        ],
        scratch_shapes=[pltpu.VMEM((tm + 2 * SUBLANE, w), F32)],
        compiler_params=_params("parallel"),
        name="mlstm_prep",
    )(proj, proj, proj, proj, proj, proj, proj, conv_w[:, :w], conv_w[:, w:])


def _log_sigmoid(x):
    return jnp.minimum(x, 0.0) - jnp.log1p(jnp.exp(-jnp.abs(x)))


def _mlstm_direction(q_ref, kt_ref, v_ref, g_ref, bias_ref, o_ref, s_ref, m_ref, *, backward):
    L = ML_CHUNK
    d = ML_HEAD_DIM
    row = lax.broadcasted_iota(jnp.int32, (L, L), 0)
    col = lax.broadcasted_iota(jnp.int32, (L, L), 1)
    keep = (col >= row) if backward else (col <= row)
    tri = keep.astype(BF16)
    ones = jnp.ones((L, LANE), BF16)

    gates = g_ref[...] + bias_ref[...]
    logf = _log_sigmoid(gates)
    hi = logf.astype(BF16)
    lo = (logf - hi.astype(F32)).astype(BF16)
    bcol = jnp.dot(tri, hi, preferred_element_type=F32) + jnp.dot(tri, lo, preferred_element_type=F32)
    gates_t = gates.T
    b_t = bcol.T
    base = 2 * ML_HEADS if backward else 0
    for h in range(ML_HEADS):
        idx = base // 2 + h
        ci, cf = base + h, base + ML_HEADS + h
        i_row = gates_t[ci:ci + 1, :]
        b_row = b_t[cf:cf + 1, :]
        b_col = bcol[:, cf:cf + 1]
        g_tot = b_row[:, 0:1] if backward else b_row[:, L - 1:L]
        a_row = g_tot - b_row + i_row
        a_max = jnp.max(a_row, axis=1, keepdims=True)
        w_row = jnp.exp(a_row - a_max)

        q = q_ref[:, h * d:(h + 1) * d]
        kt = kt_ref[h * d:(h + 1) * d, :]
        v_aug = jnp.concatenate([v_ref[:, h * d:(h + 1) * d], ones], axis=1)
        s_prev = s_ref[idx]
        m_prev = m_ref[idx][:, 0:1]

        dlog = jnp.where(keep, b_col + (i_row - b_row), NEG_INF)
        inter_log = b_col + m_prev
        m_j = jnp.maximum(inter_log, jnp.max(dlog, axis=1, keepdims=True))
        sqk = jnp.dot(q, kt, preferred_element_type=F32) * jnp.exp(dlog - m_j)
        inter_w = jnp.exp(inter_log - m_j)
        tot = (inter_w * jnp.dot(q, s_prev.astype(BF16), preferred_element_type=F32)
               + jnp.dot(sqk.astype(BF16), v_aug, preferred_element_type=F32))
        den = tot[:, d:]
        inv = 1.0 / jnp.maximum(jnp.abs(den), jnp.exp(-m_j))
        o_ref[:, h * d:(h + 1) * d] = tot[:, :d] * jnp.concatenate([inv] * (d // LANE), axis=1)

        ktw = (kt.astype(F32) * w_row).astype(BF16)
        upd = jnp.dot(ktw, v_aug, preferred_element_type=F32)
        m_new = jnp.maximum(g_tot + m_prev, a_max)
        s_ref[idx] = jnp.exp(g_tot + m_prev - m_new) * s_prev + jnp.exp(a_max - m_new) * upd
        m_ref[idx] = jnp.broadcast_to(m_new, (1, LANE))


def _mlstm_body(qf_ref, ktf_ref, vf_ref, gf_ref, qb_ref, ktb_ref, vb_ref, gb_ref, bias_ref,
                hf_ref, hb_ref, s_ref, m_ref):
    @pl.when(pl.program_id(0) == 0)
    def _():
        s_ref[...] = jnp.zeros(s_ref.shape, F32)
        m_ref[...] = jnp.zeros(m_ref.shape, F32)

    _mlstm_direction(qf_ref, ktf_ref, vf_ref, gf_ref, bias_ref, hf_ref, s_ref, m_ref, backward=False)
    _mlstm_direction(qb_ref, ktb_ref, vb_ref, gb_ref, bias_ref, hb_ref, s_ref, m_ref, backward=True)


def _mlstm_scan(mq, mkt, mv, gates, b_gate):
    s, w = mq.shape
    L = ML_CHUNK
    nc = s // L
    d = ML_HEAD_DIM
    bias = jnp.pad(b_gate, (0, LANE - b_gate.shape[0])).reshape(1, LANE)
    fwd = lambda c: c
    bwd = lambda c: nc - 1 - c
    rows = lambda ix, n: pl.BlockSpec((L, n), lambda c: (ix(c), 0))
    cols = lambda ix: pl.BlockSpec((w, L), lambda c: (0, ix(c)))
    return pl.pallas_call(
        _mlstm_body,
        grid=(nc,),
        in_specs=[rows(fwd, w), cols(fwd), rows(fwd, w), rows(fwd, LANE),
                  rows(bwd, w), cols(bwd), rows(bwd, w), rows(bwd, LANE),
                  pl.BlockSpec((1, LANE), lambda c: (0, 0))],
        out_specs=[rows(fwd, w), rows(bwd, w)],
        out_shape=[jax.ShapeDtypeStruct((s, w), F32)] * 2,
        scratch_shapes=[pltpu.VMEM((2 * ML_HEADS, d, d + LANE), F32), pltpu.VMEM((2 * ML_HEADS, 1, LANE), F32)],
        compiler_params=_params("arbitrary"),
        name="mlstm_scan",
    )(mq, mkt, mv, gates, mq, mkt, mv, gates, bias)


def _mix_out_body(ya_ref, hf_ref, hb_ref, mo_ref, mg_ref, wa_ref, wm_ref, r_ref, o_ref, yml_ref):
    @pl.when(pl.program_id(1) == 0)
    def _():
        d = ML_HEAD_DIM
        for h in range(ML_HEADS):
            sl = slice(h * d, (h + 1) * d)
            hm = _rms(hf_ref[:, sl] + hb_ref[:, sl], mg_ref[:, sl])
            yml_ref[:, sl] = (hm * jax.nn.sigmoid(mo_ref[:, sl])).astype(BF16)

    o_ref[...] = (r_ref[...] + jnp.dot(ya_ref[...], wa_ref[...], preferred_element_type=F32)
                  + jnp.dot(yml_ref[...], wm_ref[...], preferred_element_type=F32))


def _mix_out(y_att, hf, hb, proj, ml_gain, w_att, w_ml, res, *, tm=512, tn=1024):
    s, n = res.shape
    wa, wm = w_att.shape[0], w_ml.shape[0]
    return pl.pallas_call(
        _mix_out_body,
        grid=(s // tm, n // tn),
        in_specs=[
            pl.BlockSpec((tm, wa), lambda i, j: (i, 0)),
            pl.BlockSpec((tm, wm), lambda i, j: (i, 0)),
            pl.BlockSpec((tm, wm), lambda i, j: (i, 0)),
            pl.BlockSpec((tm, wm), lambda i, j: (i, COL_MO)),
            pl.BlockSpec((1, wm), lambda i, j: (0, 0)),
            pl.BlockSpec((wa, tn), lambda i, j: (0, j)),
            pl.BlockSpec((wm, tn), lambda i, j: (0, j)),
            pl.BlockSpec((tm, tn), lambda i, j: (i, j)),
        ],
        out_specs=pl.BlockSpec((tm, tn), lambda i, j: (i, j)),
        out_shape=jax.ShapeDtypeStruct((s, n), F32),
        scratch_shapes=[pltpu.VMEM((tm, wm), BF16)],
        compiler_params=_params("parallel", "arbitrary"),
        name="mix_out",
    )(y_att, hf, hb, proj, ml_gain.reshape(1, wm), w_att, w_ml, res)


def kernel(x, norm_mix_g, norm_ffn_g, w_in, b_gate, conv_w, q_gain, k_gain, ml_gain, w_out, pool_w,
           layer_scale, peer_wq, peer_keys, peer_u, peer_v, norm_f):
    b, s, d = x.shape
    h = x.reshape(b * s, d)

    wi = w_in[0]
    kv_end = ATT_Q + 2 * ATT_KV
    w_main = jnp.concatenate([wi[:, :ATT_Q], wi[:, kv_end:MAIN_W], wi[:, ATT_Q:kv_end]], axis=1).astype(BF16)
    w_gate = jnp.pad(wi[:, MAIN_W:], ((0, 0), (0, LANE - (wi.shape[1] - MAIN_W)))).astype(BF16)
    proj, gates = _norm_matmul(h, norm_mix_g[0], w_main, w_gate)
    qt, k, vt = _attn_prep(proj, q_gain[0], k_gain[0])
    y_att = _attention(qt, k, vt)
    mq, mkt, mv = _mlstm_prep(proj, conv_w[0])
    hf, hb = _mlstm_scan(mq, mkt, mv, gates, b_gate[0])
    wo = w_out[0].astype(BF16)
    h = _mix_out(y_att, hf, hb, proj, ml_gain[0], wo[:ATT_Q], wo[ATT_Q:], h)
    u_all = peer_u.astype(BF16)
    vt_all = jnp.swapaxes(peer_v, 1, 2).astype(BF16)
    h = _peer_ffn(h, norm_ffn_g[0], peer_wq[0], peer_keys[0], u_all, vt_all, 0)

    h = _pool_mixer(h, norm_mix_g[1], pool_w[0], layer_scale[0])
    h = _peer_ffn(h, norm_ffn_g[1], peer_wq[1], peer_keys[1], u_all, vt_all, 1, final_g=norm_f)
    return h.reshape(b, s, d)
```

```python
import functools

import jax
import jax.numpy as jnp
from jax import lax
from jax.experimental import pallas as pl
from jax.experimental.pallas import tpu as pltpu

F32 = jnp.float32
BF16 = jnp.bfloat16
EPS = 1e-6
LOG2E = 1.4426950408889634

GRID_W = 64
ROPE_THETA = 10000.0
ATT_HEADS = 8
ATT_KV_HEADS = 2
ATT_HEAD_DIM = 128
ML_HEADS = 4
ML_HEAD_DIM = 256
ML_CHUNK = 128
CONV_W = 5
POOL_WINDOWS = (2, 4, 8, 16)
PEER_HEADS = 8
PEER_NKEYS = 128
PEER_TOPK = 16
PEER_BLOCK = 128

ATT_Q = ATT_HEADS * ATT_HEAD_DIM
ATT_KV = ATT_KV_HEADS * ATT_HEAD_DIM
ML_W = ML_HEADS * ML_HEAD_DIM
MAIN_W = ATT_Q + 2 * ATT_KV + 4 * ML_W
assert ATT_Q == ML_W
COL_AQ, COL_MQ, COL_MK, COL_MV, COL_MO = 0, 1, 2, 3, 4
COL_AK = (ATT_Q + 4 * ML_W) // ATT_KV
COL_AV = COL_AK + 1
LANE = 128
SUBLANE = 8
VMEM_LIMIT = 48 * 1024 * 1024


def _params(*sem):
    return pltpu.CompilerParams(dimension_semantics=sem, vmem_limit_bytes=VMEM_LIMIT)


def _rms(x, g):
    return x * lax.rsqrt(jnp.mean(x * x, axis=-1, keepdims=True) + EPS) * g


def _norm_mm_body(x_ref, g_ref, w_ref, wg_ref, o_ref, og_ref, xn_ref):
    @pl.when(pl.program_id(1) == 0)
    def _():
        xn = _rms(x_ref[...], g_ref[...]).astype(BF16)
        xn_ref[...] = xn
        og_ref[...] = jnp.dot(xn, wg_ref[...], preferred_element_type=F32)

    o_ref[...] = jnp.dot(xn_ref[...], w_ref[...], preferred_element_type=F32)


def _norm_matmul(x, g, w, wg, *, tm=512, tn=1408):
    s, d = x.shape
    n = w.shape[1]
    return pl.pallas_call(
        _norm_mm_body,
        grid=(s // tm, n // tn),
        in_specs=[
            pl.BlockSpec((tm, d), lambda i, j: (i, 0)),
            pl.BlockSpec((1, d), lambda i, j: (0, 0)),
            pl.BlockSpec((d, tn), lambda i, j: (0, j)),
            pl.BlockSpec((d, LANE), lambda i, j: (0, 0)),
        ],
        out_specs=[
            pl.BlockSpec((tm, tn), lambda i, j: (i, j)),
            pl.BlockSpec((tm, LANE), lambda i, j: (i, 0)),
        ],
        out_shape=[jax.ShapeDtypeStruct((s, n), F32), jax.ShapeDtypeStruct((s, LANE), F32)],
        scratch_shapes=[pltpu.VMEM((tm, d), BF16)],
        compiler_params=_params("parallel", "arbitrary"),
        name="norm_matmul",
    )(x, g.reshape(1, d), w, wg)


def _rope_tables(s):
    rows = s // GRID_W
    r, c = jnp.meshgrid(jnp.arange(rows), jnp.arange(GRID_W), indexing="ij")
    pos = jnp.stack([r.reshape(-1), c.reshape(-1)], axis=-1).astype(F32)
    nf = ATT_HEAD_DIM // 4
    inv = ROPE_THETA ** (-jnp.arange(nf, dtype=F32) / nf)
    ang = pos[:, :, None] * inv
    cos, sin = jnp.cos(ang), jnp.sin(ang)
    ctab = jnp.stack([cos, cos], axis=2).reshape(s, ATT_HEAD_DIM)
    stab = jnp.stack([-sin, sin], axis=2).reshape(s, ATT_HEAD_DIM)
    return ctab, stab


def _rope(y, ctab, stab, first_half):
    nf = ATT_HEAD_DIM // 4
    swapped = jnp.where(first_half, pltpu.roll(y, ATT_HEAD_DIM - nf, 1), pltpu.roll(y, nf, 1))
    return y * ctab + swapped * stab


def _attn_prep_body(aq_ref, ak_ref, av_ref, c_ref, s_ref, qg_ref, kg_ref, qt_ref, k_ref, vt_ref):
    d = ATT_HEAD_DIM
    ctab, stab = c_ref[...], s_ref[...]
    lane = lax.broadcasted_iota(jnp.int32, ctab.shape, 1)
    first_half = (lane % (d // 2)) < (d // 4)
    scale = d ** -0.5 * LOG2E
    for h in range(ATT_HEADS):
        y = _rms(aq_ref[:, h * d:(h + 1) * d], qg_ref[...])
        qt_ref[h] = (_rope(y, ctab, stab, first_half) * scale).T.astype(BF16)
    for h in range(ATT_KV_HEADS):
        y = _rms(ak_ref[:, h * d:(h + 1) * d], kg_ref[...])
        k_ref[h] = _rope(y, ctab, stab, first_half).astype(BF16)
        vt_ref[h] = av_ref[:, h * d:(h + 1) * d].T.astype(BF16)


def _attn_prep(proj, q_gain, k_gain, *, tm=512):
    s = proj.shape[0]
    d = ATT_HEAD_DIM
    ctab, stab = _rope_tables(s)
    return pl.pallas_call(
        _attn_prep_body,
        grid=(s // tm,),
        in_specs=[
            pl.BlockSpec((tm, ATT_Q), lambda i: (i, COL_AQ)),
            pl.BlockSpec((tm, ATT_KV), lambda i: (i, COL_AK)),
            pl.BlockSpec((tm, ATT_KV), lambda i: (i, COL_AV)),
            pl.BlockSpec((tm, d), lambda i: (i, 0)),
            pl.BlockSpec((tm, d), lambda i: (i, 0)),
            pl.BlockSpec((1, d), lambda i: (0, 0)),
            pl.BlockSpec((1, d), lambda i: (0, 0)),
        ],
        out_specs=[
            pl.BlockSpec((ATT_HEADS, d, tm), lambda i: (0, 0, i)),
            pl.BlockSpec((ATT_KV_HEADS, tm, d), lambda i: (0, i, 0)),
            pl.BlockSpec((ATT_KV_HEADS, d, tm), lambda i: (0, 0, i)),
        ],
        out_shape=[
            jax.ShapeDtypeStruct((ATT_HEADS, d, s), BF16),
            jax.ShapeDtypeStruct((ATT_KV_HEADS, s, d), BF16),
            jax.ShapeDtypeStruct((ATT_KV_HEADS, d, s), BF16),
        ],
        compiler_params=_params("parallel"),
        name="attn_prep",
    )(proj, proj, proj, ctab, stab, q_gain.reshape(1, d), k_gain.reshape(1, d))


def _col_reduce(x, pair, final):
    fold = 8 * SUBLANE
    if x.shape[0] > fold:
        acc = x[:fold]
        for r in range(fold, x.shape[0], fold):
            acc = pair(acc, x[r:r + fold])
        x = acc
    while x.shape[0] > SUBLANE:
        half = x.shape[0] // 2
        x = pair(x[:half], x[half:])
    return final(x, axis=0, keepdims=True)


def _attn_body(qt_ref, k_ref, vt_ref, o_ref, q_all, m_ref, l_ref, acc_ref, s_ref, p_ref, *, tq, grp, wide):
    j = pl.program_id(2)
    d = ATT_HEAD_DIM
    mq = grp * tq
    chunks = mq // LANE

    @pl.when(j == 0)
    def _():
        for h in range(grp):
            for t in range(0, tq, wide):
                q_all[(h * tq + t) // wide] = qt_ref[h, :, t:t + wide]
        m_ref[...] = jnp.full(m_ref.shape, -jnp.inf, F32)
        l_ref[...] = jnp.zeros(l_ref.shape, F32)
        acc_ref[...] = jnp.zeros(acc_ref.shape, F32)

    k = k_ref[0]
    vt = vt_ref[0]
    per = wide // LANE
    groups = mq // wide

    for g in range(groups):
        st = jnp.dot(k, q_all[g], preferred_element_type=F32)
        for c in range(per):
            s_ref[g * per + c] = st[:, c * LANE:(c + 1) * LANE]
    alphas = []
    for ci in range(chunks):
        sc = s_ref[ci]
        m_prev = m_ref[ci]
        m_new = jnp.maximum(m_prev, _col_reduce(sc, jnp.maximum, jnp.max))
        alpha = jnp.exp2(m_prev - m_new)
        p = jnp.exp2(sc - m_new)
        l_ref[ci] = alpha * l_ref[ci] + _col_reduce(p, jnp.add, jnp.sum)
        m_ref[ci] = m_new
        p_ref[ci] = p.astype(BF16)
        alphas.append(alpha)
    pt = jnp.concatenate([p_ref[ci] for ci in range(chunks)], axis=1)
    pv = jnp.dot(vt, pt, preferred_element_type=F32)
    for ci in range(chunks):
        acc_ref[ci] = alphas[ci] * acc_ref[ci] + pv[:, ci * LANE:(ci + 1) * LANE]

    @pl.when(j == pl.num_programs(2) - 1)
    def _():
        per_head = tq // LANE
        for c in range(chunks):
            h, tb = divmod(c, per_head)
            out_t = acc_ref[c] / l_ref[c]
            o_ref[tb * LANE:(tb + 1) * LANE, h * d:(h + 1) * d] = out_t.T.astype(o_ref.dtype)


def _attention(qt, k, vt, *, tq=1024, tk=1024, wide=2 * LANE):
    _, s, d = k.shape
    grp = ATT_HEADS // ATT_KV_HEADS
    mq = grp * tq
    return pl.pallas_call(
        functools.partial(_attn_body, tq=tq, grp=grp, wide=wide),
        grid=(ATT_KV_HEADS, s // tq, s // tk),
        in_specs=[
            pl.BlockSpec((grp, d, tq), lambda h, i, j: (h, 0, i)),
            pl.BlockSpec((1, tk, d), lambda h, i, j: (h, j, 0)),
            pl.BlockSpec((1, d, tk), lambda h, i, j: (h, 0, j)),
        ],
        out_specs=pl.BlockSpec((tq, grp * d), lambda h, i, j: (i, h)),
        out_shape=jax.ShapeDtypeStruct((s, ATT_HEADS * d), BF16),
        scratch_shapes=[
            pltpu.VMEM((mq // wide, d, wide), BF16),
            pltpu.VMEM((mq // LANE, 1, LANE), F32),
            pltpu.VMEM((mq // LANE, 1, LANE), F32),
            pltpu.VMEM((mq // LANE, d, LANE), F32),
            pltpu.VMEM((mq // LANE, tk, LANE), F32),
            pltpu.VMEM((mq // LANE, tk, LANE), BF16),
        ],
        compiler_params=_params("parallel", "parallel", "arbitrary"),
        name="attention",
    )(qt, k, vt)


def _pool_body(prev_ref, x_ref, next_ref, g_ref, w_ref, ls_ref, o_ref, xc_ref, *, tm, s_total):
    i = pl.program_id(0)
    halo = SUBLANE
    g = g_ref[...]
    xm = x_ref[...]
    xc_ref[halo:halo + tm, :] = _rms(xm, g)
    xc_ref[0:halo, :] = jnp.where(i > 0, _rms(prev_ref[...], g), 0.0)
    xc_ref[halo + tm:2 * halo + tm, :] = jnp.where(i < pl.num_programs(0) - 1, _rms(next_ref[...], g), 0.0)
    row = i * tm + lax.broadcasted_iota(jnp.int32, (tm, 1), 0)
    p = xm.shape[1] // len(POOL_WINDOWS)
    outs = []
    for gi, window in enumerate(POOL_WINDOWS):
        hf = window // 2
        cols = slice(gi * p, (gi + 1) * p)
        acc = xc_ref[halo - hf:halo - hf + tm, cols]
        for o in range(-hf + 1, hf):
            acc = acc + xc_ref[halo + o:halo + o + tm, cols]
        cnt = (jnp.minimum(row + hf, s_total) - jnp.maximum(row - hf, 0)).astype(F32)
        pooled = acc / cnt - xc_ref[halo:halo + tm, cols]
        outs.append(jnp.dot(pooled.astype(BF16), w_ref[gi], preferred_element_type=F32))
    o_ref[...] = xm + jnp.concatenate(outs, axis=-1) * ls_ref[...]


def _pool_mixer(h, g, pool_w, layer_scale, *, tm=256):
    s, d = h.shape
    nb = tm // SUBLANE
    ng, p, _ = pool_w.shape
    return pl.pallas_call(
        functools.partial(_pool_body, tm=tm, s_total=s),
        grid=(s // tm,),
        in_specs=[
            pl.BlockSpec((SUBLANE, d), lambda i: (jnp.maximum(i * nb - 1, 0), 0)),
            pl.BlockSpec((tm, d), lambda i: (i, 0)),
            pl.BlockSpec((SUBLANE, d), lambda i: (jnp.minimum((i + 1) * nb, s // SUBLANE - 1), 0)),
            pl.BlockSpec((1, d), lambda i: (0, 0)),
            pl.BlockSpec((ng, p, p), lambda i: (0, 0, 0)),
            pl.BlockSpec((1, d), lambda i: (0, 0)),
        ],
        out_specs=pl.BlockSpec((tm, d), lambda i: (i, 0)),
        out_shape=jax.ShapeDtypeStruct((s, d), F32),
        scratch_shapes=[pltpu.VMEM((tm + 2 * SUBLANE, d), F32)],
        compiler_params=_params("parallel"),
        name="pool_mixer",
    )(h, h, h, g.reshape(1, d), pool_w.astype(BF16), layer_scale.reshape(1, d))


NEG_INF = float("-inf")


def _sorting_network(n):
    pairs = []
    p = 1
    while p < n:
        k = p
        while k >= 1:
            for j in range(k % p, n - k, 2 * k):
                for i in range(min(k, n - j - k)):
                    if (i + j) // (2 * p) == (i + j + k) // (2 * p):
                        pairs.append((i + j, i + j + k))
            k //= 2
        p *= 2
    return pairs


def _top_values_sorted(x, count):
    tiles = [x[r:r + SUBLANE] for r in range(0, x.shape[0], SUBLANE)]
    n = len(tiles)
    for a, b in _sorting_network(n):
        hi, lo = jnp.maximum(tiles[a], tiles[b]), jnp.minimum(tiles[a], tiles[b])
        tiles[a], tiles[b] = hi, lo
    rows = []
    for r in range(count):
        m = jnp.max(tiles[0], axis=0, keepdims=True)
        rows.append(m)
        hit = tiles[0] == m
        live = min(n, count - r)
        for i in range(live - 1):
            tiles[i] = jnp.where(hit, tiles[i + 1], tiles[i])
        tiles[live - 1] = jnp.where(hit, tiles[live] if live < n else NEG_INF, tiles[live - 1])
    return rows


def _stack_rows(rows, t, n=None):
    n = len(rows) if n is None else n
    rid = lax.broadcasted_iota(jnp.int32, (n, t), 0)
    out = jnp.full((n, t), NEG_INF, F32)
    for r, row in enumerate(rows):
        out = jnp.where(rid == r, row, out)
    return out


def _route_head(s1, s2):
    t = s1.shape[1]
    kk = PEER_TOPK
    v1 = _top_values_sorted(s1, kk + 1)
    v2 = _top_values_sorted(s2, kk + 1)
    v2_all = _stack_rows(v2[:kk], t)
    pieces = []
    for a in range(kk // 2):
        nb = kk // (a + 1)
        rows = kk if nb > SUBLANE else SUBLANE
        rid = lax.broadcasted_iota(jnp.int32, (rows, t), 0)
        pieces.append(jnp.where(rid < nb, v1[a] + v2_all[:rows], NEG_INF))
    tail = [v1[a] + v2[0] for a in range(kk // 2, kk + 1)] + [v1[0] + v2[kk]]
    pieces.append(_stack_rows(tail, t, kk))
    cand = jnp.concatenate(pieces, axis=0)
    pad = (1 << (cand.shape[0] // SUBLANE - 1).bit_length()) * SUBLANE - cand.shape[0]
    cand = jnp.concatenate([cand, jnp.full((pad, t), NEG_INF, F32)], axis=0)
    best = _top_values_sorted(cand, kk + 1)
    tau = 0.5 * (best[kk - 1] + best[kk])
    z = jnp.ones_like(tau)
    for r in range(1, kk):
        z = z + jnp.exp(best[r] - best[0])
    thr = jnp.exp(tau - v2[0] - s1) / z
    e1 = jnp.exp(s1 - v1[0])
    e2 = jnp.exp(s2 - v2[0]) / z
    return thr, e1, e2


def _peer_route_body(h_ref, g_ref, wq_ref, keys_ref, xnt_ref, thr_ref, e1_ref, e2_ref):
    nk = PEER_NKEYS
    xnt = _rms(h_ref[...], g_ref[...]).T.astype(BF16)
    xnt_ref[...] = xnt
    qt = jnp.dot(wq_ref[...], xnt, preferred_element_type=F32)
    for hd in range(PEER_HEADS):
        q1 = qt[(2 * hd) * nk:(2 * hd + 1) * nk].astype(BF16)
        q2 = qt[(2 * hd + 1) * nk:(2 * hd + 2) * nk].astype(BF16)
        s1 = jnp.dot(keys_ref[2 * hd], q1, preferred_element_type=F32)
        s2 = jnp.dot(keys_ref[2 * hd + 1], q2, preferred_element_type=F32)
        thr, e1, e2 = _route_head(s1, s2)
        thr_ref[hd] = thr
        e1_ref[hd] = e1
        for c in range(e2.shape[1] // LANE):
            e2_ref[hd, c] = e2[:, c * LANE:(c + 1) * LANE]


def _peer_route(h, g, wq_t, keys, *, tt=256):
    s, d = h.shape
    nq = wq_t.shape[0]
    hk = jax.ShapeDtypeStruct((PEER_HEADS, PEER_NKEYS, s), F32)
    hck = jax.ShapeDtypeStruct((PEER_HEADS, s // LANE, PEER_NKEYS, LANE), F32)
    gate_spec = pl.BlockSpec((PEER_HEADS, PEER_NKEYS, tt), lambda i: (0, 0, i))
    chunk_spec = pl.BlockSpec((PEER_HEADS, tt // LANE, PEER_NKEYS, LANE), lambda i: (0, i, 0, 0))
    return pl.pallas_call(
        _peer_route_body,
        grid=(s // tt,),
        in_specs=[
            pl.BlockSpec((tt, d), lambda i: (i, 0)),
            pl.BlockSpec((1, d), lambda i: (0, 0)),
            pl.BlockSpec((nq, d), lambda i: (0, 0)),
            pl.BlockSpec(keys.shape, lambda i: (0, 0, 0)),
        ],
        out_specs=[pl.BlockSpec((d, tt), lambda i: (0, i)), gate_spec, gate_spec, chunk_spec],
        out_shape=[jax.ShapeDtypeStruct((d, s), BF16), hk, hk, hck],
        compiler_params=_params("parallel"),
        name="peer_route",
    )(h, g.reshape(1, d), wq_t, keys)


def _gelu(x):
    return 0.5 * x * (1.0 + lax.erf(x * (2.0 ** -0.5)))


def _peer_dense_body(xnt_ref, u0_ref, u1_ref, vt0_ref, vt1_ref, thr0_ref, thr1_ref, e10_ref, e11_ref,
                     e20_ref, e21_ref, o_ref, a0_ref, a1_ref, p0_ref, p1_ref, *, tt, eb, nb):
    i = pl.program_id(0)
    nk = PEER_NKEYS
    rb = 4 * SUBLANE
    chunks = tt // LANE

    @pl.when(i == 0)
    def _():
        a1_ref[...] = jnp.zeros(a1_ref.shape, F32)
        p0_ref[...] = jnp.zeros(p0_ref.shape, BF16)

    first_pair = jnp.maximum(2 * i - 2, 0)

    @pl.when(first_pair % nb == 0)
    def _():
        o_ref[...] = jnp.zeros(o_ref.shape, F32)

    def activations(u_ref, a_ref, cs):
        lanes = slice(cs[0] * LANE, (cs[-1] + 1) * LANE)
        a = jnp.dot(u_ref[...], xnt_ref[:, lanes], preferred_element_type=F32)
        for n, c in enumerate(cs):
            a_ref[c] = a[:, n * LANE:(n + 1) * LANE]

    def gated(thr_ref, e1_ref, e2_ref, a_ref, p_ref, cs, row0):
        ks = range(eb // nk)
        for c in cs:
            lanes = slice(c * LANE, (c + 1) * LANE)
            for r in range(0, nk, rb):
                gates = {k: jnp.zeros((rb, LANE), F32) for k in ks}
                for hd in range(PEER_HEADS):
                    e2 = e2_ref[hd, c, r:r + rb, :]
                    for k in ks:
                        thr = thr_ref[hd, 0, row0 + k:row0 + k + 1, lanes]
                        e1 = e1_ref[hd, 0, row0 + k:row0 + k + 1, lanes]
                        gates[k] = gates[k] + jnp.where(e2 >= thr, e2 * e1, 0.0)
                for k in ks:
                    blk = slice(k * nk + r, k * nk + r + rb)
                    p_ref[c, blk, :] = (gates[k] * _gelu(a_ref[c, blk, :])).astype(BF16)

    def project(vt_ref, p_ref, cs):
        lanes = slice(cs[0] * LANE, (cs[-1] + 1) * LANE)
        p = jnp.concatenate([p_ref[c] for c in cs], axis=1)
        o_ref[:, lanes] += jnp.dot(vt_ref[...], p, preferred_element_type=F32)

    def half_step(u_ref, a_new, thr_ref, e1_ref, e2_ref, a_old, p_new, vt_ref, p_old, row0):
        half = chunks // 2
        for cs in (range(half), range(half, chunks)):
            activations(u_ref, a_new, cs)
            gated(thr_ref, e1_ref, e2_ref, a_old, p_new, cs[:len(cs) // 2], row0)
            project(vt_ref, p_old, cs)
            gated(thr_ref, e1_ref, e2_ref, a_old, p_new, cs[len(cs) // 2:], row0)

    rows = eb // nk
    half_step(u0_ref, a0_ref, thr0_ref, e10_ref, e20_ref, a1_ref, p1_ref, vt0_ref, p0_ref, rows)
    half_step(u1_ref, a1_ref, thr1_ref, e11_ref, e21_ref, a0_ref, p0_ref, vt1_ref, p1_ref, 0)


def _peer_dense(xnt, u_all, vt_all, layer, thr, e1, e2, *, tt=512, eb=512):
    d, s = xnt.shape
    ne = u_all.shape[1]
    nk = PEER_NKEYS
    rows = eb // nk
    nb = ne // eb
    n = (s // tt) * nb
    assert nb % 2 == 0 and 2 * rows == SUBLANE
    split = lambda a: a.reshape(PEER_HEADS, nb // 2, SUBLANE, s)

    def pair(lag, half):
        return lambda i: divmod(jnp.clip(2 * i + half - lag, 0, n - 1), nb)

    def u_spec(half):
        return pl.BlockSpec((None, eb, d), lambda i: (layer, pair(0, half)(i)[1], 0))

    def vt_spec(half):
        return pl.BlockSpec((None, d, eb), lambda i: (layer, 0, pair(2, half)(i)[1]))

    def row_spec(half):
        return pl.BlockSpec((PEER_HEADS, 1, SUBLANE, tt),
                            lambda i: (0, pair(1, half)(i)[1] // 2, 0, pair(1, half)(i)[0]))

    def e2_spec(half):
        return pl.BlockSpec((PEER_HEADS, tt // LANE, nk, LANE), lambda i: (0, pair(1, half)(i)[0], 0, 0))

    slab = lambda dtype: pltpu.VMEM((tt // LANE, eb, LANE), dtype)
    return pl.pallas_call(
        functools.partial(_peer_dense_body, tt=tt, eb=eb, nb=nb),
        grid=(n // 2 + 1,),
        in_specs=[
            pl.BlockSpec((d, tt), lambda i: (0, pair(0, 0)(i)[0])),
            u_spec(0), u_spec(1), vt_spec(0), vt_spec(1),
            row_spec(0), row_spec(1), row_spec(0), row_spec(1), e2_spec(0), e2_spec(1),
        ],
        out_specs=pl.BlockSpec((d, tt), lambda i: (0, pair(2, 0)(i)[0])),
        out_shape=jax.ShapeDtypeStruct((d, s), F32),
        scratch_shapes=[slab(F32), slab(F32), slab(BF16), slab(BF16)],
        compiler_params=_params("arbitrary"),
        name="peer_dense",
    )(xnt, u_all, u_all, vt_all, vt_all, split(thr), split(thr), split(e1), split(e1), e2, e2)


def _add_t_body(h_ref, pt_ref, g_ref, o_ref, *, final_norm):
    y = h_ref[...] + pt_ref[...].T
    o_ref[...] = _rms(y, g_ref[...]) if final_norm else y


def _add_transposed(h, pt, g=None, *, tm=512):
    s, d = h.shape
    gain = jnp.ones((1, d), F32) if g is None else g.reshape(1, d)
    return pl.pallas_call(
        functools.partial(_add_t_body, final_norm=g is not None),
        grid=(s // tm,),
        in_specs=[
            pl.BlockSpec((tm, d), lambda i: (i, 0)),
            pl.BlockSpec((d, tm), lambda i: (0, i)),
            pl.BlockSpec((1, d), lambda i: (0, 0)),
        ],
        out_specs=pl.BlockSpec((tm, d), lambda i: (i, 0)),
        out_shape=jax.ShapeDtypeStruct((s, d), F32),
        compiler_params=_params("parallel"),
        name="add_transposed",
    )(h, pt, gain)


def _peer_ffn(h, g, w_query, sub_keys, u_all, vt_all, layer, final_g=None):
    nk = PEER_NKEYS
    keys = sub_keys.reshape(2 * PEER_HEADS, nk, -1).astype(BF16)
    xnt, thr, e1, e2 = _peer_route(h, g, w_query.T.astype(BF16), keys)
    pt = _peer_dense(xnt, u_all, vt_all, layer, thr, e1, e2)
    return _add_transposed(h, pt, final_g)


def _mlstm_prep_body(qp_ref, q_ref, qn_ref, kp_ref, k_ref, kn_ref, v_ref, cwq_ref, cwk_ref,
                     qo_ref, kto_ref, vo_ref, xc_ref, *, tm):
    i = pl.program_id(0)
    last = pl.num_programs(0) - 1
    halo = SUBLANE
    pad = CONV_W // 2

    def conv_silu(prev_ref, x_ref, next_ref, w_ref):
        xc_ref[halo:halo + tm, :] = x_ref[...]
        xc_ref[0:halo, :] = jnp.where(i > 0, prev_ref[...], 0.0)
        xc_ref[halo + tm:2 * halo + tm, :] = jnp.where(i < last, next_ref[...], 0.0)
        acc = xc_ref[halo - pad:halo - pad + tm, :] * w_ref[0:1, :]
        for j in range(1, CONV_W):
            acc = acc + xc_ref[halo - pad + j:halo - pad + j + tm, :] * w_ref[j:j + 1, :]
        return acc * jax.nn.sigmoid(acc)

    qo_ref[...] = conv_silu(qp_ref, q_ref, qn_ref, cwq_ref).astype(BF16)
    kc = conv_silu(kp_ref, k_ref, kn_ref, cwk_ref) * (ML_HEAD_DIM ** -0.5)
    kto_ref[...] = kc.T.astype(BF16)
    vo_ref[...] = v_ref[...].astype(BF16)


def _mlstm_prep(proj, conv_w, *, tm=256):
    s = proj.shape[0]
    w = ML_W
    nb = tm // SUBLANE
    prev = lambda c: pl.BlockSpec((SUBLANE, w), lambda i: (jnp.maximum(i * nb - 1, 0), c))
    main = lambda c: pl.BlockSpec((tm, w), lambda i: (i, c))
    nxt = lambda c: pl.BlockSpec((SUBLANE, w), lambda i: (jnp.minimum((i + 1) * nb, s // SUBLANE - 1), c))
    cw = pl.BlockSpec((CONV_W, w), lambda i: (0, 0))
    return pl.pallas_call(
        functools.partial(_mlstm_prep_body, tm=tm),
        grid=(s // tm,),
        in_specs=[prev(COL_MQ), main(COL_MQ), nxt(COL_MQ), prev(COL_MK), main(COL_MK), nxt(COL_MK),
                  main(COL_MV), cw, cw],
        out_specs=[
            pl.BlockSpec((tm, w), lambda i: (i, 0)),
            pl.BlockSpec((w, tm), lambda i: (0, i)),
            pl.BlockSpec((tm, w), lambda i: (i, 0)),
        ],
        out_shape=[
            jax.ShapeDtypeStruct((s, w), BF16),
            jax.ShapeDtypeStruct((w, s), BF16),
            jax.ShapeDtypeStruct((s, w), BF16),
        ],
        scratch_shapes=[pltpu.VMEM((tm + 2 * SUBLANE, w), F32)],
        compiler_params=_params("parallel"),
        name="mlstm_prep",
    )(proj, proj, proj, proj, proj, proj, proj, conv_w[:, :w], conv_w[:, w:])


def _log_sigmoid(x):
    return jnp.minimum(x, 0.0) - jnp.log1p(jnp.exp(-jnp.abs(x)))


def _mlstm_direction(q_ref, kt_ref, v_ref, g_ref, bias_ref, o_ref, s_ref, m_ref, *, backward):
    L = ML_CHUNK
    d = ML_HEAD_DIM
    row = lax.broadcasted_iota(jnp.int32, (L, L), 0)
    col = lax.broadcasted_iota(jnp.int32, (L, L), 1)
    keep = (col >= row) if backward else (col <= row)
    tri = keep.astype(BF16)
    ones = jnp.ones((L, LANE), BF16)

    gates = g_ref[...] + bias_ref[...]
    logf = _log_sigmoid(gates)
    hi = logf.astype(BF16)
    lo = (logf - hi.astype(F32)).astype(BF16)
    bcol = jnp.dot(tri, hi, preferred_element_type=F32) + jnp.dot(tri, lo, preferred_element_type=F32)
    gates_t = gates.T
    b_t = bcol.T
    base = 2 * ML_HEADS if backward else 0
    for h in range(ML_HEADS):
        idx = base // 2 + h
        ci, cf = base + h, base + ML_HEADS + h
        i_row = gates_t[ci:ci + 1, :]
        b_row = b_t[cf:cf + 1, :]
        b_col = bcol[:, cf:cf + 1]
        g_tot = b_row[:, 0:1] if backward else b_row[:, L - 1:L]
        a_row = g_tot - b_row + i_row
        a_max = jnp.max(a_row, axis=1, keepdims=True)
        w_row = jnp.exp(a_row - a_max)

        q = q_ref[:, h * d:(h + 1) * d]
        kt = kt_ref[h * d:(h + 1) * d, :]
        v_aug = jnp.concatenate([v_ref[:, h * d:(h + 1) * d], ones], axis=1)
        s_prev = s_ref[idx]
        m_prev = m_ref[idx][:, 0:1]

        dlog = jnp.where(keep, b_col + (i_row - b_row), NEG_INF)
        inter_log = b_col + m_prev
        m_j = jnp.maximum(inter_log, jnp.max(dlog, axis=1, keepdims=True))
        sqk = jnp.dot(q, kt, preferred_element_type=F32) * jnp.exp(dlog - m_j)
        inter_w = jnp.exp(inter_log - m_j)
        tot = (inter_w * jnp.dot(q, s_prev.astype(BF16), preferred_element_type=F32)
               + jnp.dot(sqk.astype(BF16), v_aug, preferred_element_type=F32))
        den = tot[:, d:]
        inv = 1.0 / jnp.maximum(jnp.abs(den), jnp.exp(-m_j))
        o_ref[:, h * d:(h + 1) * d] = tot[:, :d] * jnp.concatenate([inv] * (d // LANE), axis=1)

        ktw = (kt.astype(F32) * w_row).astype(BF16)
        upd = jnp.dot(ktw, v_aug, preferred_element_type=F32)
        m_new = jnp.maximum(g_tot + m_prev, a_max)
        s_ref[idx] = jnp.exp(g_tot + m_prev - m_new) * s_prev + jnp.exp(a_max - m_new) * upd
        m_ref[idx] = jnp.broadcast_to(m_new, (1, LANE))


def _mlstm_body(qf_ref, ktf_ref, vf_ref, gf_ref, qb_ref, ktb_ref, vb_ref, gb_ref, bias_ref,
                hf_ref, hb_ref, s_ref, m_ref):
    @pl.when(pl.program_id(0) == 0)
    def _():
        s_ref[...] = jnp.zeros(s_ref.shape, F32)
        m_ref[...] = jnp.zeros(m_ref.shape, F32)

    _mlstm_direction(qf_ref, ktf_ref, vf_ref, gf_ref, bias_ref, hf_ref, s_ref, m_ref, backward=False)
    _mlstm_direction(qb_ref, ktb_ref, vb_ref, gb_ref, bias_ref, hb_ref, s_ref, m_ref, backward=True)


def _mlstm_scan(mq, mkt, mv, gates, b_gate):
    s, w = mq.shape
    L = ML_CHUNK
    nc = s // L
    d = ML_HEAD_DIM
    bias = jnp.pad(b_gate, (0, LANE - b_gate.shape[0])).reshape(1, LANE)
    fwd = lambda c: c
    bwd = lambda c: nc - 1 - c
    rows = lambda ix, n: pl.BlockSpec((L, n), lambda c: (ix(c), 0))
    cols = lambda ix: pl.BlockSpec((w, L), lambda c: (0, ix(c)))
    return pl.pallas_call(
        _mlstm_body,
        grid=(nc,),
        in_specs=[rows(fwd, w), cols(fwd), rows(fwd, w), rows(fwd, LANE),
                  rows(bwd, w), cols(bwd), rows(bwd, w), rows(bwd, LANE),
                  pl.BlockSpec((1, LANE), lambda c: (0, 0))],
        out_specs=[rows(fwd, w), rows(bwd, w)],
        out_shape=[jax.ShapeDtypeStruct((s, w), F32)] * 2,
        scratch_shapes=[pltpu.VMEM((2 * ML_HEADS, d, d + LANE), F32), pltpu.VMEM((2 * ML_HEADS, 1, LANE), F32)],
        compiler_params=_params("arbitrary"),
        name="mlstm_scan",
    )(mq, mkt, mv, gates, mq, mkt, mv, gates, bias)


def _mix_out_body(ya_ref, hf_ref, hb_ref, mo_ref, mg_ref, wa_ref, wm_ref, r_ref, o_ref, yml_ref):
    @pl.when(pl.program_id(1) == 0)
    def _():
        d = ML_HEAD_DIM
        for h in range(ML_HEADS):
            sl = slice(h * d, (h + 1) * d)
            hm = _rms(hf_ref[:, sl] + hb_ref[:, sl], mg_ref[:, sl])
            yml_ref[:, sl] = (hm * jax.nn.sigmoid(mo_ref[:, sl])).astype(BF16)

    o_ref[...] = (r_ref[...] + jnp.dot(ya_ref[...], wa_ref[...], preferred_element_type=F32)
                  + jnp.dot(yml_ref[...], wm_ref[...], preferred_element_type=F32))


def _mix_out(y_att, hf, hb, proj, ml_gain, w_att, w_ml, res, *, tm=512, tn=1024):
    s, n = res.shape
    wa, wm = w_att.shape[0], w_ml.shape[0]
    return pl.pallas_call(
        _mix_out_body,
        grid=(s // tm, n // tn),
        in_specs=[
            pl.BlockSpec((tm, wa), lambda i, j: (i, 0)),
            pl.BlockSpec((tm, wm), lambda i, j: (i, 0)),
            pl.BlockSpec((tm, wm), lambda i, j: (i, 0)),
            pl.BlockSpec((tm, wm), lambda i, j: (i, COL_MO)),
            pl.BlockSpec((1, wm), lambda i, j: (0, 0)),
            pl.BlockSpec((wa, tn), lambda i, j: (0, j)),
            pl.BlockSpec((wm, tn), lambda i, j: (0, j)),
            pl.BlockSpec((tm, tn), lambda i, j: (i, j)),
        ],
        out_specs=pl.BlockSpec((tm, tn), lambda i, j: (i, j)),
        out_shape=jax.ShapeDtypeStruct((s, n), F32),
        scratch_shapes=[pltpu.VMEM((tm, wm), BF16)],
        compiler_params=_params("parallel", "arbitrary"),
        name="mix_out",
    )(y_att, hf, hb, proj, ml_gain.reshape(1, wm), w_att, w_ml, res)


def kernel(x, norm_mix_g, norm_ffn_g, w_in, b_gate, conv_w, q_gain, k_gain, ml_gain, w_out, pool_w,
           layer_scale, peer_wq, peer_keys, peer_u, peer_v, norm_f):
    b, s, d = x.shape
    h = x.reshape(b * s, d)

    wi = w_in[0]
    kv_end = ATT_Q + 2 * ATT_KV
    w_main = jnp.concatenate([wi[:, :ATT_Q], wi[:, kv_end:MAIN_W], wi[:, ATT_Q:kv_end]], axis=1).astype(BF16)
    w_gate = jnp.pad(wi[:, MAIN_W:], ((0, 0), (0, LANE - (wi.shape[1] - MAIN_W)))).astype(BF16)
    proj, gates = _norm_matmul(h, norm_mix_g[0], w_main, w_gate)
    qt, k, vt = _attn_prep(proj, q_gain[0], k_gain[0])
    y_att = _attention(qt, k, vt)
    mq, mkt, mv = _mlstm_prep(proj, conv_w[0])
    hf, hb = _mlstm_scan(mq, mkt, mv, gates, b_gate[0])
    wo = w_out[0].astype(BF16)
    h = _mix_out(y_att, hf, hb, proj, ml_gain[0], wo[:ATT_Q], wo[ATT_Q:], h)
    u_all = peer_u.astype(BF16)
    vt_all = jnp.swapaxes(peer_v, 1, 2).astype(BF16)
    h = _peer_ffn(h, norm_ffn_g[0], peer_wq[0], peer_keys[0], u_all, vt_all, 0)

    h = _pool_mixer(h, norm_mix_g[1], pool_w[0], layer_scale[0])
    h = _peer_ffn(h, norm_ffn_g[1], peer_wq[1], peer_keys[1], u_all, vt_all, 1, final_g=norm_f)
    return h.reshape(b, s, d)
```

```python
import functools

import jax
import jax.numpy as jnp
from jax import lax
from jax.experimental import pallas as pl
from jax.experimental.pallas import tpu as pltpu

F32 = jnp.float32
BF16 = jnp.bfloat16
EPS = 1e-6
LOG2E = 1.4426950408889634

GRID_W = 64
ROPE_THETA = 10000.0
ATT_HEADS = 8
ATT_KV_HEADS = 2
ATT_HEAD_DIM = 128
ML_HEADS = 4
ML_HEAD_DIM = 256
ML_CHUNK = 128
CONV_W = 5
POOL_WINDOWS = (2, 4, 8, 16)
PEER_HEADS = 8
PEER_NKEYS = 128
PEER_TOPK = 16

ATT_Q = ATT_HEADS * ATT_HEAD_DIM
ATT_KV = ATT_KV_HEADS * ATT_HEAD_DIM
ML_W = ML_HEADS * ML_HEAD_DIM
MAIN_W = ATT_Q + 2 * ATT_KV + 4 * ML_W
assert ATT_Q == ML_W
COL_AQ, COL_MQ, COL_MK, COL_MV, COL_MO = 0, 1, 2, 3, 4
COL_AK = (ATT_Q + 4 * ML_W) // ATT_KV
COL_AV = COL_AK + 1
LANE = 128
SUBLANE = 8
VMEM_LIMIT = 48 * 1024 * 1024


def _params(*sem):
    return pltpu.CompilerParams(dimension_semantics=sem, vmem_limit_bytes=VMEM_LIMIT)


def _rms(x, g):
    return x * lax.rsqrt(jnp.mean(x * x, axis=-1, keepdims=True) + EPS) * g


def _norm_mm_body(x_ref, g_ref, w_ref, wg_ref, o_ref, og_ref, xn_ref):
    @pl.when(pl.program_id(1) == 0)
    def _():
        xn = _rms(x_ref[...], g_ref[...]).astype(BF16)
        xn_ref[...] = xn
        og_ref[...] = jnp.dot(xn, wg_ref[...], preferred_element_type=F32)

    o_ref[...] = jnp.dot(xn_ref[...], w_ref[...], preferred_element_type=F32)


def _norm_matmul(x, g, w, wg, *, tm=512, tn=1408):
    s, d = x.shape
    n = w.shape[1]
    return pl.pallas_call(
        _norm_mm_body,
        grid=(s // tm, n // tn),
        in_specs=[
            pl.BlockSpec((tm, d), lambda i, j: (i, 0)),
            pl.BlockSpec((1, d), lambda i, j: (0, 0)),
            pl.BlockSpec((d, tn), lambda i, j: (0, j)),
            pl.BlockSpec((d, LANE), lambda i, j: (0, 0)),
        ],
        out_specs=[
            pl.BlockSpec((tm, tn), lambda i, j: (i, j)),
            pl.BlockSpec((tm, LANE), lambda i, j: (i, 0)),
        ],
        out_shape=[jax.ShapeDtypeStruct((s, n), F32), jax.ShapeDtypeStruct((s, LANE), F32)],
        scratch_shapes=[pltpu.VMEM((tm, d), BF16)],
        compiler_params=_params("parallel", "arbitrary"),
        name="norm_matmul",
    )(x, g.reshape(1, d), w, wg)


def _rope_tables(s):
    rows = s // GRID_W
    r, c = jnp.meshgrid(jnp.arange(rows), jnp.arange(GRID_W), indexing="ij")
    pos = jnp.stack([r.reshape(-1), c.reshape(-1)], axis=-1).astype(F32)
    nf = ATT_HEAD_DIM // 4
    inv = ROPE_THETA ** (-jnp.arange(nf, dtype=F32) / nf)
    ang = pos[:, :, None] * inv
    cos, sin = jnp.cos(ang), jnp.sin(ang)
    ctab = jnp.stack([cos, cos], axis=2).reshape(s, ATT_HEAD_DIM)
    stab = jnp.stack([-sin, sin], axis=2).reshape(s, ATT_HEAD_DIM)
    return ctab, stab


def _rope(y, ctab, stab, first_half):
    nf = ATT_HEAD_DIM // 4
    swapped = jnp.where(first_half, pltpu.roll(y, ATT_HEAD_DIM - nf, 1), pltpu.roll(y, nf, 1))
    return y * ctab + swapped * stab


def _attn_prep_body(aq_ref, ak_ref, av_ref, c_ref, s_ref, qg_ref, kg_ref, qt_ref, k_ref, vt_ref):
    d = ATT_HEAD_DIM
    ctab, stab = c_ref[...], s_ref[...]
    lane = lax.broadcasted_iota(jnp.int32, ctab.shape, 1)
    first_half = (lane % (d // 2)) < (d // 4)
    scale = d ** -0.5 * LOG2E
    for h in range(ATT_HEADS):
        y = _rms(aq_ref[:, h * d:(h + 1) * d], qg_ref[...])
        qt_ref[h] = (_rope(y, ctab, stab, first_half) * scale).T.astype(BF16)
    for h in range(ATT_KV_HEADS):
        y = _rms(ak_ref[:, h * d:(h + 1) * d], kg_ref[...])
        k_ref[h] = _rope(y, ctab, stab, first_half).astype(BF16)
        vt_ref[h] = av_ref[:, h * d:(h + 1) * d].T.astype(BF16)


def _attn_prep(proj, q_gain, k_gain, *, tm=512):
    s = proj.shape[0]
    d = ATT_HEAD_DIM
    ctab, stab = _rope_tables(s)
    return pl.pallas_call(
        _attn_prep_body,
        grid=(s // tm,),
        in_specs=[
            pl.BlockSpec((tm, ATT_Q), lambda i: (i, COL_AQ)),
            pl.BlockSpec((tm, ATT_KV), lambda i: (i, COL_AK)),
            pl.BlockSpec((tm, ATT_KV), lambda i: (i, COL_AV)),
            pl.BlockSpec((tm, d), lambda i: (i, 0)),
            pl.BlockSpec((tm, d), lambda i: (i, 0)),
            pl.BlockSpec((1, d), lambda i: (0, 0)),
            pl.BlockSpec((1, d), lambda i: (0, 0)),
        ],
        out_specs=[
            pl.BlockSpec((ATT_HEADS, d, tm), lambda i: (0, 0, i)),
            pl.BlockSpec((ATT_KV_HEADS, tm, d), lambda i: (0, i, 0)),
            pl.BlockSpec((ATT_KV_HEADS, d, tm), lambda i: (0, 0, i)),
        ],
        out_shape=[
            jax.ShapeDtypeStruct((ATT_HEADS, d, s), BF16),
            jax.ShapeDtypeStruct((ATT_KV_HEADS, s, d), BF16),
            jax.ShapeDtypeStruct((ATT_KV_HEADS, d, s), BF16),
        ],
        compiler_params=_params("parallel"),
        name="attn_prep",
    )(proj, proj, proj, ctab, stab, q_gain.reshape(1, d), k_gain.reshape(1, d))


def _col_reduce(x, pair, final):
    fold = 8 * SUBLANE
    if x.shape[0] > fold:
        acc = x[:fold]
        for r in range(fold, x.shape[0], fold):
            acc = pair(acc, x[r:r + fold])
        x = acc
    while x.shape[0] > SUBLANE:
        half = x.shape[0] // 2
        x = pair(x[:half], x[half:])
    return final(x, axis=0, keepdims=True)


def _attn_body(qt_ref, k_ref, vt_ref, o_ref, q_all, m_ref, l_ref, acc_ref, s_ref, p_ref, *, tq, grp, wide):
    j = pl.program_id(2)
    d = ATT_HEAD_DIM
    mq = grp * tq
    chunks = mq // LANE

    @pl.when(j == 0)
    def _():
        for h in range(grp):
            for t in range(0, tq, wide):
                q_all[(h * tq + t) // wide] = qt_ref[h, :, t:t + wide]
        m_ref[...] = jnp.full(m_ref.shape, -jnp.inf, F32)
        l_ref[...] = jnp.zeros(l_ref.shape, F32)
        acc_ref[...] = jnp.zeros(acc_ref.shape, F32)

    k = k_ref[0]
    vt = vt_ref[0]
    per = wide // LANE
    groups = mq // wide

    for g in range(groups):
        st = jnp.dot(k, q_all[g], preferred_element_type=F32)
        for c in range(per):
            s_ref[g * per + c] = st[:, c * LANE:(c + 1) * LANE]
    alphas = []
    for ci in range(chunks):
        sc = s_ref[ci]
        m_prev = m_ref[ci]
        m_new = jnp.maximum(m_prev, _col_reduce(sc, jnp.maximum, jnp.max))
        alpha = jnp.exp2(m_prev - m_new)
        p = jnp.exp2(sc - m_new)
        l_ref[ci] = alpha * l_ref[ci] + _col_reduce(p, jnp.add, jnp.sum)
        m_ref[ci] = m_new
        p_ref[ci] = p.astype(BF16)
        alphas.append(alpha)
    pt = jnp.concatenate([p_ref[ci] for ci in range(chunks)], axis=1)
    pv = jnp.dot(vt, pt, preferred_element_type=F32)
    for ci in range(chunks):
        acc_ref[ci] = alphas[ci] * acc_ref[ci] + pv[:, ci * LANE:(ci + 1) * LANE]

    @pl.when(j == pl.num_programs(2) - 1)
    def _():
        per_head = tq // LANE
        for c in range(chunks):
            h, tb = divmod(c, per_head)
            out_t = acc_ref[c] / l_ref[c]
            o_ref[tb * LANE:(tb + 1) * LANE, h * d:(h + 1) * d] = out_t.T.astype(o_ref.dtype)


def _attention(qt, k, vt, *, tq=1024, tk=1024, wide=2 * LANE):
    _, s, d = k.shape
    grp = ATT_HEADS // ATT_KV_HEADS
    mq = grp * tq
    return pl.pallas_call(
        functools.partial(_attn_body, tq=tq, grp=grp, wide=wide),
        grid=(ATT_KV_HEADS, s // tq, s // tk),
        in_specs=[
            pl.BlockSpec((grp, d, tq), lambda h, i, j: (h, 0, i)),
            pl.BlockSpec((1, tk, d), lambda h, i, j: (h, j, 0)),
            pl.BlockSpec((1, d, tk), lambda h, i, j: (h, 0, j)),
        ],
        out_specs=pl.BlockSpec((tq, grp * d), lambda h, i, j: (i, h)),
        out_shape=jax.ShapeDtypeStruct((s, ATT_HEADS * d), BF16),
        scratch_shapes=[
            pltpu.VMEM((mq // wide, d, wide), BF16),
            pltpu.VMEM((mq // LANE, 1, LANE), F32),
            pltpu.VMEM((mq // LANE, 1, LANE), F32),
            pltpu.VMEM((mq // LANE, d, LANE), F32),
            pltpu.VMEM((mq // LANE, tk, LANE), F32),
            pltpu.VMEM((mq // LANE, tk, LANE), BF16),
        ],
        compiler_params=_params("parallel", "parallel", "arbitrary"),
        name="attention",
    )(qt, k, vt)


def _pool_body(prev_ref, x_ref, next_ref, g_ref, w_ref, ls_ref, o_ref, xc_ref, *, tm, s_total):
    i = pl.program_id(0)
    halo = SUBLANE
    g = g_ref[...]
    xm = x_ref[...]
    xc_ref[halo:halo + tm, :] = _rms(xm, g)
    xc_ref[0:halo, :] = jnp.where(i > 0, _rms(prev_ref[...], g), 0.0)
    xc_ref[halo + tm:2 * halo + tm, :] = jnp.where(i < pl.num_programs(0) - 1, _rms(next_ref[...], g), 0.0)
    row = i * tm + lax.broadcasted_iota(jnp.int32, (tm, 1), 0)
    p = xm.shape[1] // len(POOL_WINDOWS)
    outs = []
    for gi, window in enumerate(POOL_WINDOWS):
        hf = window // 2
        cols = slice(gi * p, (gi + 1) * p)
        acc = xc_ref[halo - hf:halo - hf + tm, cols]
        for o in range(-hf + 1, hf):
            acc = acc + xc_ref[halo + o:halo + o + tm, cols]
        cnt = (jnp.minimum(row + hf, s_total) - jnp.maximum(row - hf, 0)).astype(F32)
        pooled = acc / cnt - xc_ref[halo:halo + tm, cols]
        outs.append(jnp.dot(pooled.astype(BF16), w_ref[gi], preferred_element_type=F32))
    o_ref[...] = xm + jnp.concatenate(outs, axis=-1) * ls_ref[...]


def _pool_mixer(h, g, pool_w, layer_scale, *, tm=256):
    s, d = h.shape
    nb = tm // SUBLANE
    ng, p, _ = pool_w.shape
    return pl.pallas_call(
        functools.partial(_pool_body, tm=tm, s_total=s),
        grid=(s // tm,),
        in_specs=[
            pl.BlockSpec((SUBLANE, d), lambda i: (jnp.maximum(i * nb - 1, 0), 0)),
            pl.BlockSpec((tm, d), lambda i: (i, 0)),
            pl.BlockSpec((SUBLANE, d), lambda i: (jnp.minimum((i + 1) * nb, s // SUBLANE - 1), 0)),
            pl.BlockSpec((1, d), lambda i: (0, 0)),
            pl.BlockSpec((ng, p, p), lambda i: (0, 0, 0)),
            pl.BlockSpec((1, d), lambda i: (0, 0)),
        ],
        out_specs=pl.BlockSpec((tm, d), lambda i: (i, 0)),
        out_shape=jax.ShapeDtypeStruct((s, d), F32),
        scratch_shapes=[pltpu.VMEM((tm + 2 * SUBLANE, d), F32)],
        compiler_params=_params("parallel"),
        name="pool_mixer",
    )(h, h, h, g.reshape(1, d), pool_w.astype(BF16), layer_scale.reshape(1, d))


NEG_INF = float("-inf")


def _sorting_network(n):
    pairs = []
    p = 1
    while p < n:
        k = p
        while k >= 1:
            for j in range(k % p, n - k, 2 * k):
                for i in range(min(k, n - j - k)):
                    if (i + j) // (2 * p) == (i + j + k) // (2 * p):
                        pairs.append((i + j, i + j + k))
            k //= 2
        p *= 2
    return pairs


def _top_values_sorted(x, count):
    tiles = [x[r:r + SUBLANE] for r in range(0, x.shape[0], SUBLANE)]
    n = len(tiles)
    for a, b in _sorting_network(n):
        hi, lo = jnp.maximum(tiles[a], tiles[b]), jnp.minimum(tiles[a], tiles[b])
        tiles[a], tiles[b] = hi, lo
    rows = []
    for r in range(count):
        m = jnp.max(tiles[0], axis=0, keepdims=True)
        rows.append(m)
        hit = tiles[0] == m
        live = min(n, count - r)
        for i in range(live - 1):
            tiles[i] = jnp.where(hit, tiles[i + 1], tiles[i])
        tiles[live - 1] = jnp.where(hit, tiles[live] if live < n else NEG_INF, tiles[live - 1])
    return rows


def _stack_rows(rows, t, n=None):
    n = len(rows) if n is None else n
    rid = lax.broadcasted_iota(jnp.int32, (n, t), 0)
    out = jnp.full((n, t), NEG_INF, F32)
    for r, row in enumerate(rows):
        out = jnp.where(rid == r, row, out)
    return out


def _route_head(s1, s2):
    t = s1.shape[1]
    kk = PEER_TOPK
    v1 = _top_values_sorted(s1, kk + 1)
    v2 = _top_values_sorted(s2, kk + 1)
    v2_all = _stack_rows(v2[:kk], t)
    pieces = []
    for a in range(kk // 2):
        nb = kk // (a + 1)
        rows = kk if nb > SUBLANE else SUBLANE
        rid = lax.broadcasted_iota(jnp.int32, (rows, t), 0)
        pieces.append(jnp.where(rid < nb, v1[a] + v2_all[:rows], NEG_INF))
    tail = [v1[a] + v2[0] for a in range(kk // 2, kk + 1)] + [v1[0] + v2[kk]]
    pieces.append(_stack_rows(tail, t, kk))
    cand = jnp.concatenate(pieces, axis=0)
    pad = (1 << (cand.shape[0] // SUBLANE - 1).bit_length()) * SUBLANE - cand.shape[0]
    cand = jnp.concatenate([cand, jnp.full((pad, t), NEG_INF, F32)], axis=0)
    best = _top_values_sorted(cand, kk + 1)
    tau = 0.5 * (best[kk - 1] + best[kk])
    z = jnp.ones_like(tau)
    for r in range(1, kk):
        z = z + jnp.exp(best[r] - best[0])
    thr = jnp.exp(tau - v2[0] - s1) / z
    e1 = jnp.exp(s1 - v1[0])
    e2 = jnp.exp(s2 - v2[0]) / z
    return thr, e1, e2


def _peer_route_body(h_ref, g_ref, wq_ref, keys_ref, xnt_ref, thr_ref, e1_ref, e2_ref):
    nk = PEER_NKEYS
    xnt = _rms(h_ref[...], g_ref[...]).T.astype(BF16)
    xnt_ref[...] = xnt
    qt = jnp.dot(wq_ref[...], xnt, preferred_element_type=F32)
    for hd in range(PEER_HEADS):
        q1 = qt[(2 * hd) * nk:(2 * hd + 1) * nk].astype(BF16)
        q2 = qt[(2 * hd + 1) * nk:(2 * hd + 2) * nk].astype(BF16)
        s1 = jnp.dot(keys_ref[2 * hd], q1, preferred_element_type=F32)
        s2 = jnp.dot(keys_ref[2 * hd + 1], q2, preferred_element_type=F32)
        thr, e1, e2 = _route_head(s1, s2)
        thr_ref[hd] = thr
        e1_ref[hd] = e1
        for c in range(e2.shape[1] // LANE):
            e2_ref[hd, c] = e2[:, c * LANE:(c + 1) * LANE]


def _peer_route(h, g, wq_t, keys, *, tt=256):
    s, d = h.shape
    nq = wq_t.shape[0]
    hk = jax.ShapeDtypeStruct((PEER_HEADS, PEER_NKEYS, s), F32)
    hck = jax.ShapeDtypeStruct((PEER_HEADS, s // LANE, PEER_NKEYS, LANE), F32)
    gate_spec = pl.BlockSpec((PEER_HEADS, PEER_NKEYS, tt), lambda i: (0, 0, i))
    chunk_spec = pl.BlockSpec((PEER_HEADS, tt // LANE, PEER_NKEYS, LANE), lambda i: (0, i, 0, 0))
    return pl.pallas_call(
        _peer_route_body,
        grid=(s // tt,),
        in_specs=[
            pl.BlockSpec((tt, d), lambda i: (i, 0)),
            pl.BlockSpec((1, d), lambda i: (0, 0)),
            pl.BlockSpec((nq, d), lambda i: (0, 0)),
            pl.BlockSpec(keys.shape, lambda i: (0, 0, 0)),
        ],
        out_specs=[pl.BlockSpec((d, tt), lambda i: (0, i)), gate_spec, gate_spec, chunk_spec],
        out_shape=[jax.ShapeDtypeStruct((d, s), BF16), hk, hk, hck],
        compiler_params=_params("parallel"),
        name="peer_route",
    )(h, g.reshape(1, d), wq_t, keys)


def _gelu(x):
    return 0.5 * x * (1.0 + lax.erf(x * (2.0 ** -0.5)))


def _peer_dense_body(xnt_ref, u0_ref, u1_ref, vt0_ref, vt1_ref, thr0_ref, thr1_ref, e10_ref, e11_ref,
                     e20_ref, e21_ref, o_ref, a0_ref, a1_ref, p0_ref, p1_ref, *, tt, eb, nb):
    i = pl.program_id(0)
    nk = PEER_NKEYS
    rb = 4 * SUBLANE
    chunks = tt // LANE

    @pl.when(i == 0)
    def _():
        a1_ref[...] = jnp.zeros(a1_ref.shape, F32)
        p0_ref[...] = jnp.zeros(p0_ref.shape, BF16)

    first_pair = jnp.maximum(2 * i - 2, 0)

    @pl.when(first_pair % nb == 0)
    def _():
        o_ref[...] = jnp.zeros(o_ref.shape, F32)

    def activations(u_ref, a_ref, cs):
        lanes = slice(cs[0] * LANE, (cs[-1] + 1) * LANE)
        a = jnp.dot(u_ref[...], xnt_ref[:, lanes], preferred_element_type=F32)
        for n, c in enumerate(cs):
            a_ref[c] = a[:, n * LANE:(n + 1) * LANE]

    def gated(thr_ref, e1_ref, e2_ref, a_ref, p_ref, cs, row0):
        ks = range(eb // nk)
        for c in cs:
            lanes = slice(c * LANE, (c + 1) * LANE)
            for r in range(0, nk, rb):
                gates = {k: jnp.zeros((rb, LANE), F32) for k in ks}
                for hd in range(PEER_HEADS):
                    e2 = e2_ref[hd, c, r:r + rb, :]
                    for k in ks:
                        thr = thr_ref[hd, 0, row0 + k:row0 + k + 1, lanes]
                        e1 = e1_ref[hd, 0, row0 + k:row0 + k + 1, lanes]
                        gates[k] = gates[k] + jnp.where(e2 >= thr, e2 * e1, 0.0)
                for k in ks:
                    blk = slice(k * nk + r, k * nk + r + rb)
                    p_ref[c, blk, :] = (gates[k] * _gelu(a_ref[c, blk, :])).astype(BF16)

    def project(cs):
        lanes = slice(cs[0] * LANE, (cs[-1] + 1) * LANE)
        p = jnp.concatenate([jnp.concatenate([p_ref[c] for c in cs], axis=1) for p_ref in (p0_ref, p1_ref)], axis=0)
        vt = jnp.concatenate([vt0_ref[...], vt1_ref[...]], axis=1)
        o_ref[:, lanes] += jnp.dot(vt, p, preferred_element_type=F32)

    def half_step(u_ref, a_new, thr_ref, e1_ref, e2_ref, a_old, p_new, row0, with_projection):
        half = chunks // 2
        for cs in (range(half), range(half, chunks)):
            activations(u_ref, a_new, cs)
            if with_projection:
                project(cs)
            gated(thr_ref, e1_ref, e2_ref, a_old, p_new, cs, row0)

    rows = eb // nk
    half_step(u0_ref, a0_ref, thr0_ref, e10_ref, e20_ref, a1_ref, p1_ref, rows, False)
    half_step(u1_ref, a1_ref, thr1_ref, e11_ref, e21_ref, a0_ref, p0_ref, 0, True)


def _peer_dense(xnt, u_all, vt_all, layer, thr, e1, e2, *, tt=512, eb=512):
    d, s = xnt.shape
    ne = u_all.shape[1]
    nk = PEER_NKEYS
    rows = eb // nk
    nb = ne // eb
    n = (s // tt) * nb
    assert nb % 2 == 0 and 2 * rows == SUBLANE
    split = lambda a: a.reshape(PEER_HEADS, nb // 2, SUBLANE, s)

    def pair(lag, half):
        return lambda i: divmod(jnp.clip(2 * i + half - lag, 0, n - 1), nb)

    def u_spec(half):
        return pl.BlockSpec((None, eb, d), lambda i: (layer, pair(0, half)(i)[1], 0))

    def vt_spec(half):
        return pl.BlockSpec((None, d, eb), lambda i: (layer, 0, pair(2, half)(i)[1]))

    def row_spec(half):
        return pl.BlockSpec((PEER_HEADS, 1, SUBLANE, tt),
                            lambda i: (0, pair(1, half)(i)[1] // 2, 0, pair(1, half)(i)[0]))

    def e2_spec(half):
        return pl.BlockSpec((PEER_HEADS, tt // LANE, nk, LANE), lambda i: (0, pair(1, half)(i)[0], 0, 0))

    slab = lambda dtype: pltpu.VMEM((tt // LANE, eb, LANE), dtype)
    return pl.pallas_call(
        functools.partial(_peer_dense_body, tt=tt, eb=eb, nb=nb),
        grid=(n // 2 + 1,),
        in_specs=[
            pl.BlockSpec((d, tt), lambda i: (0, pair(0, 0)(i)[0])),
            u_spec(0), u_spec(1), vt_spec(0), vt_spec(1),
            row_spec(0), row_spec(1), row_spec(0), row_spec(1), e2_spec(0), e2_spec(1),
        ],
        out_specs=pl.BlockSpec((d, tt), lambda i: (0, pair(2, 0)(i)[0])),
        out_shape=jax.ShapeDtypeStruct((d, s), F32),
        scratch_shapes=[slab(F32), slab(F32), slab(BF16), slab(BF16)],
        compiler_params=_params("arbitrary"),
        name="peer_dense",
    )(xnt, u_all, u_all, vt_all, vt_all, split(thr), split(thr), split(e1), split(e1), e2, e2)


def _add_t_body(h_ref, pt_ref, g_ref, o_ref, *, final_norm):
    y = h_ref[...] + pt_ref[...].T
    o_ref[...] = _rms(y, g_ref[...]) if final_norm else y


def _add_transposed(h, pt, g=None, *, tm=512):
    s, d = h.shape
    gain = jnp.ones((1, d), F32) if g is None else g.reshape(1, d)
    return pl.pallas_call(
        functools.partial(_add_t_body, final_norm=g is not None),
        grid=(s // tm,),
        in_specs=[
            pl.BlockSpec((tm, d), lambda i: (i, 0)),
            pl.BlockSpec((d, tm), lambda i: (0, i)),
            pl.BlockSpec((1, d), lambda i: (0, 0)),
        ],
        out_specs=pl.BlockSpec((tm, d), lambda i: (i, 0)),
        out_shape=jax.ShapeDtypeStruct((s, d), F32),
        compiler_params=_params("parallel"),
        name="add_transposed",
    )(h, pt, gain)


def _peer_ffn(h, g, w_query, sub_keys, u_all, vt_all, layer, final_g=None):
    nk = PEER_NKEYS
    keys = sub_keys.reshape(2 * PEER_HEADS, nk, -1).astype(BF16)
    xnt, thr, e1, e2 = _peer_route(h, g, w_query.T.astype(BF16), keys)
    pt = _peer_dense(xnt, u_all, vt_all, layer, thr, e1, e2)
    return _add_transposed(h, pt, final_g)


def _mlstm_prep_body(qp_ref, q_ref, qn_ref, kp_ref, k_ref, kn_ref, v_ref, cwq_ref, cwk_ref,
                     qo_ref, kto_ref, vo_ref, xc_ref, *, tm):
    i = pl.program_id(0)
    last = pl.num_programs(0) - 1
    halo = SUBLANE
    pad = CONV_W // 2

    def conv_silu(prev_ref, x_ref, next_ref, w_ref):
        xc_ref[halo:halo + tm, :] = x_ref[...]
        xc_ref[0:halo, :] = jnp.where(i > 0, prev_ref[...], 0.0)
        xc_ref[halo + tm:2 * halo + tm, :] = jnp.where(i < last, next_ref[...], 0.0)
        acc = xc_ref[halo - pad:halo - pad + tm, :] * w_ref[0:1, :]
        for j in range(1, CONV_W):
            acc = acc + xc_ref[halo - pad + j:halo - pad + j + tm, :] * w_ref[j:j + 1, :]
        return acc * jax.nn.sigmoid(acc)

    qo_ref[...] = conv_silu(qp_ref, q_ref, qn_ref, cwq_ref).astype(BF16)
    kc = conv_silu(kp_ref, k_ref, kn_ref, cwk_ref) * (ML_HEAD_DIM ** -0.5)
    kto_ref[...] = kc.T.astype(BF16)
    vo_ref[...] = v_ref[...].astype(BF16)


def _mlstm_prep(proj, conv_w, *, tm=256):
    s = proj.shape[0]
    w = ML_W
    nb = tm // SUBLANE
    prev = lambda c: pl.BlockSpec((SUBLANE, w), lambda i: (jnp.maximum(i * nb - 1, 0), c))
    main = lambda c: pl.BlockSpec((tm, w), lambda i: (i, c))
    nxt = lambda c: pl.BlockSpec((SUBLANE, w), lambda i: (jnp.minimum((i + 1) * nb, s // SUBLANE - 1), c))
    cw = pl.BlockSpec((CONV_W, w), lambda i: (0, 0))
    return pl.pallas_call(
        functools.partial(_mlstm_prep_body, tm=tm),
        grid=(s // tm,),
        in_specs=[prev(COL_MQ), main(COL_MQ), nxt(COL_MQ), prev(COL_MK), main(COL_MK), nxt(COL_MK),
                  main(COL_MV), cw, cw],
        out_specs=[
            pl.BlockSpec((tm, w), lambda i: (i, 0)),
            pl.BlockSpec((w, tm), lambda i: (0, i)),
            pl.BlockSpec((tm, w), lambda i: (i, 0)),
        ],
        out_shape=[
            jax.ShapeDtypeStruct((s, w), BF16),
            jax.ShapeDtypeStruct((w, s), BF16),
            jax.ShapeDtypeStruct((s, w), BF16),
        ],
        scratch_shapes=[pltpu.VMEM((tm + 2 * SUBLANE, w), F32)],
        compiler_params=_params("parallel"),
        name="mlstm_prep",
    )(proj, proj, proj, proj, proj, proj, proj, conv_w[:, :w], conv_w[:, w:])


def _log_sigmoid(x):
    return jnp.minimum(x, 0.0) - jnp.log1p(jnp.exp(-jnp.abs(x)))


def _mlstm_direction(q_ref, kt_ref, v_ref, g_ref, bias_ref, o_ref, s_ref, m_ref, *, backward):
    L = ML_CHUNK
    d = ML_HEAD_DIM
    row = lax.broadcasted_iota(jnp.int32, (L, L), 0)
    col = lax.broadcasted_iota(jnp.int32, (L, L), 1)
    keep = (col >= row) if backward else (col <= row)
    tri = keep.astype(BF16)
    ones = jnp.ones((L, LANE), BF16)

    gates = g_ref[...] + bias_ref[...]
    logf = _log_sigmoid(gates)
    hi = logf.astype(BF16)
    lo = (logf - hi.astype(F32)).astype(BF16)
    bcol = jnp.dot(tri, hi, preferred_element_type=F32) + jnp.dot(tri, lo, preferred_element_type=F32)
    gates_t = gates.T
    b_t = bcol.T
    base = 2 * ML_HEADS if backward else 0
    for h in range(ML_HEADS):
        idx = base // 2 + h
        ci, cf = base + h, base + ML_HEADS + h
        i_row = gates_t[ci:ci + 1, :]
        b_row = b_t[cf:cf + 1, :]
        b_col = bcol[:, cf:cf + 1]
        g_tot = b_row[:, 0:1] if backward else b_row[:, L - 1:L]
        a_row = g_tot - b_row + i_row
        a_max = jnp.max(a_row, axis=1, keepdims=True)
        w_row = jnp.exp(a_row - a_max)

        q = q_ref[:, h * d:(h + 1) * d]
        kt = kt_ref[h * d:(h + 1) * d, :]
        v_aug = jnp.concatenate([v_ref[:, h * d:(h + 1) * d], ones], axis=1)
        s_prev = s_ref[idx]
        m_prev = m_ref[idx][:, 0:1]

        dlog = jnp.where(keep, b_col + (i_row - b_row), NEG_INF)
        inter_log = b_col + m_prev
        m_j = jnp.maximum(inter_log, jnp.max(dlog, axis=1, keepdims=True))
        sqk = jnp.dot(q, kt, preferred_element_type=F32) * jnp.exp(dlog - m_j)
        inter_w = jnp.exp(inter_log - m_j)
        tot = (inter_w * jnp.dot(q, s_prev.astype(BF16), preferred_element_type=F32)
               + jnp.dot(sqk.astype(BF16), v_aug, preferred_element_type=F32))
        den = tot[:, d:]
        inv = 1.0 / jnp.maximum(jnp.abs(den), jnp.exp(-m_j))
        o_ref[:, h * d:(h + 1) * d] = tot[:, :d] * jnp.concatenate([inv] * (d // LANE), axis=1)

        ktw = (kt.astype(F32) * w_row).astype(BF16)
        upd = jnp.dot(ktw, v_aug, preferred_element_type=F32)
        m_new = jnp.maximum(g_tot + m_prev, a_max)
        s_ref[idx] = jnp.exp(g_tot + m_prev - m_new) * s_prev + jnp.exp(a_max - m_new) * upd
        m_ref[idx] = jnp.broadcast_to(m_new, (1, LANE))


def _mlstm_body(qf_ref, ktf_ref, vf_ref, gf_ref, qb_ref, ktb_ref, vb_ref, gb_ref, bias_ref,
                hf_ref, hb_ref, s_ref, m_ref):
    @pl.when(pl.program_id(0) == 0)
    def _():
        s_ref[...] = jnp.zeros(s_ref.shape, F32)
        m_ref[...] = jnp.zeros(m_ref.shape, F32)

    _mlstm_direction(qf_ref, ktf_ref, vf_ref, gf_ref, bias_ref, hf_ref, s_ref, m_ref, backward=False)
    _mlstm_direction(qb_ref, ktb_ref, vb_ref, gb_ref, bias_ref, hb_ref, s_ref, m_ref, backward=True)


def _mlstm_scan(mq, mkt, mv, gates, b_gate):
    s, w = mq.shape
    L = ML_CHUNK
    nc = s // L
    d = ML_HEAD_DIM
    bias = jnp.pad(b_gate, (0, LANE - b_gate.shape[0])).reshape(1, LANE)
    fwd = lambda c: c
    bwd = lambda c: nc - 1 - c
    rows = lambda ix, n: pl.BlockSpec((L, n), lambda c: (ix(c), 0))
    cols = lambda ix: pl.BlockSpec((w, L), lambda c: (0, ix(c)))
    return pl.pallas_call(
        _mlstm_body,
        grid=(nc,),
        in_specs=[rows(fwd, w), cols(fwd), rows(fwd, w), rows(fwd, LANE),
                  rows(bwd, w), cols(bwd), rows(bwd, w), rows(bwd, LANE),
                  pl.BlockSpec((1, LANE), lambda c: (0, 0))],
        out_specs=[rows(fwd, w), rows(bwd, w)],
        out_shape=[jax.ShapeDtypeStruct((s, w), F32)] * 2,
        scratch_shapes=[pltpu.VMEM((2 * ML_HEADS, d, d + LANE), F32), pltpu.VMEM((2 * ML_HEADS, 1, LANE), F32)],
        compiler_params=_params("arbitrary"),
        name="mlstm_scan",
    )(mq, mkt, mv, gates, mq, mkt, mv, gates, bias)


def _mix_out_body(ya_ref, hf_ref, hb_ref, mo_ref, mg_ref, wa_ref, wm_ref, r_ref, o_ref, yml_ref):
    @pl.when(pl.program_id(1) == 0)
    def _():
        d = ML_HEAD_DIM
        for h in range(ML_HEADS):
            sl = slice(h * d, (h + 1) * d)
            hm = _rms(hf_ref[:, sl] + hb_ref[:, sl], mg_ref[:, sl])
            yml_ref[:, sl] = (hm * jax.nn.sigmoid(mo_ref[:, sl])).astype(BF16)

    o_ref[...] = (r_ref[...] + jnp.dot(ya_ref[...], wa_ref[...], preferred_element_type=F32)
                  + jnp.dot(yml_ref[...], wm_ref[...], preferred_element_type=F32))


def _mix_out(y_att, hf, hb, proj, ml_gain, w_att, w_ml, res, *, tm=512, tn=1024):
    s, n = res.shape
    wa, wm = w_att.shape[0], w_ml.shape[0]
    return pl.pallas_call(
        _mix_out_body,
        grid=(s // tm, n // tn),
        in_specs=[
            pl.BlockSpec((tm, wa), lambda i, j: (i, 0)),
            pl.BlockSpec((tm, wm), lambda i, j: (i, 0)),
            pl.BlockSpec((tm, wm), lambda i, j: (i, 0)),
            pl.BlockSpec((tm, wm), lambda i, j: (i, COL_MO)),
            pl.BlockSpec((1, wm), lambda i, j: (0, 0)),
            pl.BlockSpec((wa, tn), lambda i, j: (0, j)),
            pl.BlockSpec((wm, tn), lambda i, j: (0, j)),
            pl.BlockSpec((tm, tn), lambda i, j: (i, j)),
        ],
        out_specs=pl.BlockSpec((tm, tn), lambda i, j: (i, j)),
        out_shape=jax.ShapeDtypeStruct((s, n), F32),
        scratch_shapes=[pltpu.VMEM((tm, wm), BF16)],
        compiler_params=_params("parallel", "arbitrary"),
        name="mix_out",
    )(y_att, hf, hb, proj, ml_gain.reshape(1, wm), w_att, w_ml, res)


def kernel(x, norm_mix_g, norm_ffn_g, w_in, b_gate, conv_w, q_gain, k_gain, ml_gain, w_out, pool_w,
           layer_scale, peer_wq, peer_keys, peer_u, peer_v, norm_f):
    b, s, d = x.shape
    h = x.reshape(b * s, d)

    wi = w_in[0]
    kv_end = ATT_Q + 2 * ATT_KV
    w_main = jnp.concatenate([wi[:, :ATT_Q], wi[:, kv_end:MAIN_W], wi[:, ATT_Q:kv_end]], axis=1).astype(BF16)
    w_gate = jnp.pad(wi[:, MAIN_W:], ((0, 0), (0, LANE - (wi.shape[1] - MAIN_W)))).astype(BF16)
    proj, gates = _norm_matmul(h, norm_mix_g[0], w_main, w_gate)
    qt, k, vt = _attn_prep(proj, q_gain[0], k_gain[0])
    y_att = _attention(qt, k, vt)
    mq, mkt, mv = _mlstm_prep(proj, conv_w[0])
    hf, hb = _mlstm_scan(mq, mkt, mv, gates, b_gate[0])
    wo = w_out[0].astype(BF16)
    h = _mix_out(y_att, hf, hb, proj, ml_gain[0], wo[:ATT_Q], wo[ATT_Q:], h)
    u_all = peer_u.astype(BF16)
    vt_all = jnp.swapaxes(peer_v, 1, 2).astype(BF16)
    h = _peer_ffn(h, norm_ffn_g[0], peer_wq[0], peer_keys[0], u_all, vt_all, 0)

    h = _pool_mixer(h, norm_mix_g[1], pool_w[0], layer_scale[0])
    h = _peer_ffn(h, norm_ffn_g[1], peer_wq[1], peer_keys[1], u_all, vt_all, 1, final_g=norm_f)
    return h.reshape(b, s, d)
```

```python
import functools

import jax
import jax.numpy as jnp
from jax import lax
from jax.experimental import pallas as pl
from jax.experimental.pallas import tpu as pltpu

F32 = jnp.float32
BF16 = jnp.bfloat16
EPS = 1e-6
LOG2E = 1.4426950408889634

GRID_W = 64
ROPE_THETA = 10000.0
ATT_HEADS = 8
ATT_KV_HEADS = 2
ATT_HEAD_DIM = 128
ML_HEADS = 4
ML_HEAD_DIM = 256
ML_CHUNK = 128
CONV_W = 5
POOL_WINDOWS = (2, 4, 8, 16)
PEER_HEADS = 8
PEER_NKEYS = 128
PEER_TOPK = 16
PEER_BLOCK = 128

ATT_Q = ATT_HEADS * ATT_HEAD_DIM
ATT_KV = ATT_KV_HEADS * ATT_HEAD_DIM
ML_W = ML_HEADS * ML_HEAD_DIM
MAIN_W = ATT_Q + 2 * ATT_KV + 4 * ML_W
assert ATT_Q == ML_W
COL_AQ, COL_MQ, COL_MK, COL_MV, COL_MO = 0, 1, 2, 3, 4
COL_AK = (ATT_Q + 4 * ML_W) // ATT_KV
COL_AV = COL_AK + 1
LANE = 128
SUBLANE = 8
VMEM_LIMIT = 48 * 1024 * 1024


def _params(*sem):
    return pltpu.CompilerParams(dimension_semantics=sem, vmem_limit_bytes=VMEM_LIMIT)


def _rms(x, g):
    return x * lax.rsqrt(jnp.mean(x * x, axis=-1, keepdims=True) + EPS) * g


def _norm_mm_body(x_ref, g_ref, w_ref, wg_ref, o_ref, og_ref, xn_ref):
    @pl.when(pl.program_id(1) == 0)
    def _():
        xn = _rms(x_ref[...], g_ref[...]).astype(BF16)
        xn_ref[...] = xn
        og_ref[...] = jnp.dot(xn, wg_ref[...], preferred_element_type=F32)

    o_ref[...] = jnp.dot(xn_ref[...], w_ref[...], preferred_element_type=F32)


def _norm_matmul(x, g, w, wg, *, tm=512, tn=1408):
    s, d = x.shape
    n = w.shape[1]
    return pl.pallas_call(
        _norm_mm_body,
        grid=(s // tm, n // tn),
        in_specs=[
            pl.BlockSpec((tm, d), lambda i, j: (i, 0)),
            pl.BlockSpec((1, d), lambda i, j: (0, 0)),
            pl.BlockSpec((d, tn), lambda i, j: (0, j)),
            pl.BlockSpec((d, LANE), lambda i, j: (0, 0)),
        ],
        out_specs=[
            pl.BlockSpec((tm, tn), lambda i, j: (i, j)),
            pl.BlockSpec((tm, LANE), lambda i, j: (i, 0)),
        ],
        out_shape=[jax.ShapeDtypeStruct((s, n), F32), jax.ShapeDtypeStruct((s, LANE), F32)],
        scratch_shapes=[pltpu.VMEM((tm, d), BF16)],
        compiler_params=_params("parallel", "arbitrary"),
        name="norm_matmul",
    )(x, g.reshape(1, d), w, wg)


def _rope_tables(s):
    rows = s // GRID_W
    r, c = jnp.meshgrid(jnp.arange(rows), jnp.arange(GRID_W), indexing="ij")
    pos = jnp.stack([r.reshape(-1), c.reshape(-1)], axis=-1).astype(F32)
    nf = ATT_HEAD_DIM // 4
    inv = ROPE_THETA ** (-jnp.arange(nf, dtype=F32) / nf)
    ang = pos[:, :, None] * inv
    cos, sin = jnp.cos(ang), jnp.sin(ang)
    ctab = jnp.stack([cos, cos], axis=2).reshape(s, ATT_HEAD_DIM)
    stab = jnp.stack([-sin, sin], axis=2).reshape(s, ATT_HEAD_DIM)
    return ctab, stab


def _rope(y, ctab, stab, first_half):
    nf = ATT_HEAD_DIM // 4
    swapped = jnp.where(first_half, pltpu.roll(y, ATT_HEAD_DIM - nf, 1), pltpu.roll(y, nf, 1))
    return y * ctab + swapped * stab


def _attn_prep_body(aq_ref, ak_ref, av_ref, c_ref, s_ref, qg_ref, kg_ref, qt_ref, k_ref, vt_ref):
    d = ATT_HEAD_DIM
    ctab, stab = c_ref[...], s_ref[...]
    lane = lax.broadcasted_iota(jnp.int32, ctab.shape, 1)
    first_half = (lane % (d // 2)) < (d // 4)
    scale = d ** -0.5 * LOG2E
    for h in range(ATT_HEADS):
        y = _rms(aq_ref[:, h * d:(h + 1) * d], qg_ref[...])
        qt_ref[h] = (_rope(y, ctab, stab, first_half) * scale).T.astype(BF16)
    for h in range(ATT_KV_HEADS):
        y = _rms(ak_ref[:, h * d:(h + 1) * d], kg_ref[...])
        k_ref[h] = _rope(y, ctab, stab, first_half).astype(BF16)
        vt_ref[h] = av_ref[:, h * d:(h + 1) * d].T.astype(BF16)


def _attn_prep(proj, q_gain, k_gain, *, tm=512):
    s = proj.shape[0]
    d = ATT_HEAD_DIM
    ctab, stab = _rope_tables(s)
    return pl.pallas_call(
        _attn_prep_body,
        grid=(s // tm,),
        in_specs=[
            pl.BlockSpec((tm, ATT_Q), lambda i: (i, COL_AQ)),
            pl.BlockSpec((tm, ATT_KV), lambda i: (i, COL_AK)),
            pl.BlockSpec((tm, ATT_KV), lambda i: (i, COL_AV)),
            pl.BlockSpec((tm, d), lambda i: (i, 0)),
            pl.BlockSpec((tm, d), lambda i: (i, 0)),
            pl.BlockSpec((1, d), lambda i: (0, 0)),
            pl.BlockSpec((1, d), lambda i: (0, 0)),
        ],
        out_specs=[
            pl.BlockSpec((ATT_HEADS, d, tm), lambda i: (0, 0, i)),
            pl.BlockSpec((ATT_KV_HEADS, tm, d), lambda i: (0, i, 0)),
            pl.BlockSpec((ATT_KV_HEADS, d, tm), lambda i: (0, 0, i)),
        ],
        out_shape=[
            jax.ShapeDtypeStruct((ATT_HEADS, d, s), BF16),
            jax.ShapeDtypeStruct((ATT_KV_HEADS, s, d), BF16),
            jax.ShapeDtypeStruct((ATT_KV_HEADS, d, s), BF16),
        ],
        compiler_params=_params("parallel"),
        name="attn_prep",
    )(proj, proj, proj, ctab, stab, q_gain.reshape(1, d), k_gain.reshape(1, d))


def _col_reduce(x, pair, final):
    fold = 2 * SUBLANE
    if x.shape[0] > fold:
        acc = x[:fold]
        for r in range(fold, x.shape[0], fold):
            acc = pair(acc, x[r:r + fold])
        x = acc
    while x.shape[0] > SUBLANE:
        half = x.shape[0] // 2
        x = pair(x[:half], x[half:])
    return final(x, axis=0, keepdims=True)


def _attn_body(qt_ref, k_ref, vt_ref, o_ref, q_all, m_ref, l_ref, acc_ref, s_ref, p_ref, *, tq, grp, wide):
    j = pl.program_id(2)
    d = ATT_HEAD_DIM
    mq = grp * tq
    chunks = mq // LANE

    @pl.when(j == 0)
    def _():
        for h in range(grp):
            for t in range(0, tq, wide):
                q_all[(h * tq + t) // wide] = qt_ref[h, :, t:t + wide]
        m_ref[...] = jnp.full(m_ref.shape, -jnp.inf, F32)
        l_ref[...] = jnp.zeros(l_ref.shape, F32)
        acc_ref[...] = jnp.zeros(acc_ref.shape, F32)

    k = k_ref[0]
    vt = vt_ref[0]
    per = wide // LANE
    groups = mq // wide

    for g in range(groups):
        st = jnp.dot(k, q_all[g], preferred_element_type=F32)
        for c in range(per):
            s_ref[g * per + c] = st[:, c * LANE:(c + 1) * LANE]
    alphas = []
    for ci in range(chunks):
        sc = s_ref[ci]
        m_prev = m_ref[ci]
        m_new = jnp.maximum(m_prev, _col_reduce(sc, jnp.maximum, jnp.max))
        alpha = jnp.exp2(m_prev - m_new)
        p = jnp.exp2(sc - m_new)
        l_ref[ci] = alpha * l_ref[ci] + _col_reduce(p, jnp.add, jnp.sum)
        m_ref[ci] = m_new
        p_ref[ci] = p.astype(BF16)
        alphas.append(alpha)
    pt = jnp.concatenate([p_ref[ci] for ci in range(chunks)], axis=1)
    pv = jnp.dot(vt, pt, preferred_element_type=F32)
    for ci in range(chunks):
        acc_ref[ci] = alphas[ci] * acc_ref[ci] + pv[:, ci * LANE:(ci + 1) * LANE]

    @pl.when(j == pl.num_programs(2) - 1)
    def _():
        per_head = tq // LANE
        for c in range(chunks):
            h, tb = divmod(c, per_head)
            out_t = acc_ref[c] / l_ref[c]
            o_ref[tb * LANE:(tb + 1) * LANE, h * d:(h + 1) * d] = out_t.T.astype(o_ref.dtype)


def _attention(qt, k, vt, *, tq=1024, tk=1024, wide=2 * LANE):
    _, s, d = k.shape
    grp = ATT_HEADS // ATT_KV_HEADS
    mq = grp * tq
    return pl.pallas_call(
        functools.partial(_attn_body, tq=tq, grp=grp, wide=wide),
        grid=(ATT_KV_HEADS, s // tq, s // tk),
        in_specs=[
            pl.BlockSpec((grp, d, tq), lambda h, i, j: (h, 0, i)),
            pl.BlockSpec((1, tk, d), lambda h, i, j: (h, j, 0)),
            pl.BlockSpec((1, d, tk), lambda h, i, j: (h, 0, j)),
        ],
        out_specs=pl.BlockSpec((tq, grp * d), lambda h, i, j: (i, h)),
        out_shape=jax.ShapeDtypeStruct((s, ATT_HEADS * d), BF16),
        scratch_shapes=[
            pltpu.VMEM((mq // wide, d, wide), BF16),
            pltpu.VMEM((mq // LANE, 1, LANE), F32),
            pltpu.VMEM((mq // LANE, 1, LANE), F32),
            pltpu.VMEM((mq // LANE, d, LANE), F32),
            pltpu.VMEM((mq // LANE, tk, LANE), F32),
            pltpu.VMEM((mq // LANE, tk, LANE), BF16),
        ],
        compiler_params=_params("parallel", "parallel", "arbitrary"),
        name="attention",
    )(qt, k, vt)


def _pool_body(prev_ref, x_ref, next_ref, g_ref, w_ref, ls_ref, o_ref, xc_ref, *, tm, s_total):
    i = pl.program_id(0)
    halo = SUBLANE
    g = g_ref[...]
    xm = x_ref[...]
    xc_ref[halo:halo + tm, :] = _rms(xm, g)
    xc_ref[0:halo, :] = jnp.where(i > 0, _rms(prev_ref[...], g), 0.0)
    xc_ref[halo + tm:2 * halo + tm, :] = jnp.where(i < pl.num_programs(0) - 1, _rms(next_ref[...], g), 0.0)
    row = i * tm + lax.broadcasted_iota(jnp.int32, (tm, 1), 0)
    p = xm.shape[1] // len(POOL_WINDOWS)
    outs = []
    for gi, window in enumerate(POOL_WINDOWS):
        hf = window // 2
        cols = slice(gi * p, (gi + 1) * p)
        acc = xc_ref[halo - hf:halo - hf + tm, cols]
        for o in range(-hf + 1, hf):
            acc = acc + xc_ref[halo + o:halo + o + tm, cols]
        cnt = (jnp.minimum(row + hf, s_total) - jnp.maximum(row - hf, 0)).astype(F32)
        pooled = acc / cnt - xc_ref[halo:halo + tm, cols]
        outs.append(jnp.dot(pooled.astype(BF16), w_ref[gi], preferred_element_type=F32))
    o_ref[...] = xm + jnp.concatenate(outs, axis=-1) * ls_ref[...]


def _pool_mixer(h, g, pool_w, layer_scale, *, tm=256):
    s, d = h.shape
    nb = tm // SUBLANE
    ng, p, _ = pool_w.shape
    return pl.pallas_call(
        functools.partial(_pool_body, tm=tm, s_total=s),
        grid=(s // tm,),
        in_specs=[
            pl.BlockSpec((SUBLANE, d), lambda i: (jnp.maximum(i * nb - 1, 0), 0)),
            pl.BlockSpec((tm, d), lambda i: (i, 0)),
            pl.BlockSpec((SUBLANE, d), lambda i: (jnp.minimum((i + 1) * nb, s // SUBLANE - 1), 0)),
            pl.BlockSpec((1, d), lambda i: (0, 0)),
            pl.BlockSpec((ng, p, p), lambda i: (0, 0, 0)),
            pl.BlockSpec((1, d), lambda i: (0, 0)),
        ],
        out_specs=pl.BlockSpec((tm, d), lambda i: (i, 0)),
        out_shape=jax.ShapeDtypeStruct((s, d), F32),
        scratch_shapes=[pltpu.VMEM((tm + 2 * SUBLANE, d), F32)],
        compiler_params=_params("parallel"),
        name="pool_mixer",
    )(h, h, h, g.reshape(1, d), pool_w.astype(BF16), layer_scale.reshape(1, d))


NEG_INF = float("-inf")


def _sorting_network(n):
    pairs = []
    p = 1
    while p < n:
        k = p
        while k >= 1:
            for j in range(k % p, n - k, 2 * k):
                for i in range(min(k, n - j - k)):
                    if (i + j) // (2 * p) == (i + j + k) // (2 * p):
                        pairs.append((i + j, i + j + k))
            k //= 2
        p *= 2
    return pairs


def _top_values_sorted(x, count):
    tiles = [x[r:r + SUBLANE] for r in range(0, x.shape[0], SUBLANE)]
    n = len(tiles)
    for a, b in _sorting_network(n):
        hi, lo = jnp.maximum(tiles[a], tiles[b]), jnp.minimum(tiles[a], tiles[b])
        tiles[a], tiles[b] = hi, lo
    rows = []
    for r in range(count):
        m = jnp.max(tiles[0], axis=0, keepdims=True)
        rows.append(m)
        hit = tiles[0] == m
        live = min(n, count - r)
        for i in range(live - 1):
            tiles[i] = jnp.where(hit, tiles[i + 1], tiles[i])
        tiles[live - 1] = jnp.where(hit, tiles[live] if live < n else NEG_INF, tiles[live - 1])
    return rows


def _stack_rows(rows, t, n=None):
    n = len(rows) if n is None else n
    rid = lax.broadcasted_iota(jnp.int32, (n, t), 0)
    out = jnp.full((n, t), NEG_INF, F32)
    for r, row in enumerate(rows):
        out = jnp.where(rid == r, row, out)
    return out


def _route_head(s1, s2):
    t = s1.shape[1]
    kk = PEER_TOPK
    v1 = _top_values_sorted(s1, kk + 1)
    v2 = _top_values_sorted(s2, kk + 1)
    v2_all = _stack_rows(v2[:kk], t)
    pieces = []
    for a in range(kk // 2):
        nb = kk // (a + 1)
        rows = kk if nb > SUBLANE else SUBLANE
        rid = lax.broadcasted_iota(jnp.int32, (rows, t), 0)
        pieces.append(jnp.where(rid < nb, v1[a] + v2_all[:rows], NEG_INF))
    tail = [v1[a] + v2[0] for a in range(kk // 2, kk + 1)] + [v1[0] + v2[kk]]
    pieces.append(_stack_rows(tail, t, kk))
    cand = jnp.concatenate(pieces, axis=0)
    pad = (1 << (cand.shape[0] // SUBLANE - 1).bit_length()) * SUBLANE - cand.shape[0]
    cand = jnp.concatenate([cand, jnp.full((pad, t), NEG_INF, F32)], axis=0)
    best = _top_values_sorted(cand, kk + 1)
    tau = 0.5 * (best[kk - 1] + best[kk])
    z = jnp.ones_like(tau)
    for r in range(1, kk):
        z = z + jnp.exp(best[r] - best[0])
    thr = jnp.exp(tau - v2[0] - s1) / z
    e1 = jnp.exp(s1 - v1[0])
    e2 = jnp.exp(s2 - v2[0]) / z
    return thr, e1, e2


def _peer_route_body(h_ref, g_ref, wq_ref, keys_ref, xnt_ref, thr_ref, e1_ref, e2_ref):
    nk = PEER_NKEYS
    xnt = _rms(h_ref[...], g_ref[...]).T.astype(BF16)
    xnt_ref[...] = xnt
    qt = jnp.dot(wq_ref[...], xnt, preferred_element_type=F32)
    for hd in range(PEER_HEADS):
        q1 = qt[(2 * hd) * nk:(2 * hd + 1) * nk].astype(BF16)
        q2 = qt[(2 * hd + 1) * nk:(2 * hd + 2) * nk].astype(BF16)
        s1 = jnp.dot(keys_ref[2 * hd], q1, preferred_element_type=F32)
        s2 = jnp.dot(keys_ref[2 * hd + 1], q2, preferred_element_type=F32)
        thr, e1, e2 = _route_head(s1, s2)
        thr_ref[hd] = thr
        e1_ref[hd] = e1
        for c in range(e2.shape[1] // LANE):
            e2_ref[hd, c] = e2[:, c * LANE:(c + 1) * LANE]


def _peer_route(h, g, wq_t, keys, *, tt=256):
    s, d = h.shape
    nq = wq_t.shape[0]
    hk = jax.ShapeDtypeStruct((PEER_HEADS, PEER_NKEYS, s), F32)
    hck = jax.ShapeDtypeStruct((PEER_HEADS, s // LANE, PEER_NKEYS, LANE), F32)
    gate_spec = pl.BlockSpec((PEER_HEADS, PEER_NKEYS, tt), lambda i: (0, 0, i))
    chunk_spec = pl.BlockSpec((PEER_HEADS, tt // LANE, PEER_NKEYS, LANE), lambda i: (0, i, 0, 0))
    return pl.pallas_call(
        _peer_route_body,
        grid=(s // tt,),
        in_specs=[
            pl.BlockSpec((tt, d), lambda i: (i, 0)),
            pl.BlockSpec((1, d), lambda i: (0, 0)),
            pl.BlockSpec((nq, d), lambda i: (0, 0)),
            pl.BlockSpec(keys.shape, lambda i: (0, 0, 0)),
        ],
        out_specs=[pl.BlockSpec((d, tt), lambda i: (0, i)), gate_spec, gate_spec, chunk_spec],
        out_shape=[jax.ShapeDtypeStruct((d, s), BF16), hk, hk, hck],
        compiler_params=_params("parallel"),
        name="peer_route",
    )(h, g.reshape(1, d), wq_t, keys)


def _gelu(x):
    return 0.5 * x * (1.0 + lax.erf(x * (2.0 ** -0.5)))


def _peer_dense_body(xnt_ref, u0_ref, u1_ref, vt0_ref, vt1_ref, thr0_ref, thr1_ref, e10_ref, e11_ref,
                     e20_ref, e21_ref, o_ref, a0_ref, a1_ref, p0_ref, p1_ref, *, tt, eb, nb):
    i = pl.program_id(0)
    nk = PEER_NKEYS
    rb = 4 * SUBLANE
    chunks = tt // LANE

    @pl.when(i == 0)
    def _():
        a1_ref[...] = jnp.zeros(a1_ref.shape, F32)
        p0_ref[...] = jnp.zeros(p0_ref.shape, BF16)

    first_pair = jnp.maximum(2 * i - 2, 0)

    @pl.when(first_pair % nb == 0)
    def _():
        o_ref[...] = jnp.zeros(o_ref.shape, F32)

    def activations(u_ref, a_ref, cs):
        lanes = slice(cs[0] * LANE, (cs[-1] + 1) * LANE)
        a = jnp.dot(u_ref[...], xnt_ref[:, lanes], preferred_element_type=F32)
        for n, c in enumerate(cs):
            a_ref[c] = a[:, n * LANE:(n + 1) * LANE]

    def gated(thr_ref, e1_ref, e2_ref, a_ref, p_ref, cs, row0):
        ks = range(eb // nk)
        for c in cs:
            lanes = slice(c * LANE, (c + 1) * LANE)
            for r in range(0, nk, rb):
                gates = {k: jnp.zeros((rb, LANE), F32) for k in ks}
                for hd in range(PEER_HEADS):
                    e2 = e2_ref[hd, c, r:r + rb, :]
                    for k in ks:
                        thr = thr_ref[hd, 0, row0 + k:row0 + k + 1, lanes]
                        e1 = e1_ref[hd, 0, row0 + k:row0 + k + 1, lanes]
                        gates[k] = gates[k] + jnp.where(e2 >= thr, e2 * e1, 0.0)
                for k in ks:
                    blk = slice(k * nk + r, k * nk + r + rb)
                    p_ref[c, blk, :] = (gates[k] * _gelu(a_ref[c, blk, :])).astype(BF16)

    def project(vt_ref, p_ref, cs):
        lanes = slice(cs[0] * LANE, (cs[-1] + 1) * LANE)
        p = jnp.concatenate([p_ref[c] for c in cs], axis=1)
        o_ref[:, lanes] += jnp.dot(vt_ref[...], p, preferred_element_type=F32)

    def half_step(u_ref, a_new, thr_ref, e1_ref, e2_ref, a_old, p_new, vt_ref, p_old, row0):
        half = chunks // 2
        for cs in (range(half), range(half, chunks)):
            activations(u_ref, a_new, cs)
            gated(thr_ref, e1_ref, e2_ref, a_old, p_new, cs[:len(cs) // 2], row0)
            project(vt_ref, p_old, cs)
            gated(thr_ref, e1_ref, e2_ref, a_old, p_new, cs[len(cs) // 2:], row0)

    rows = eb // nk
    half_step(u0_ref, a0_ref, thr0_ref, e10_ref, e20_ref, a1_ref, p1_ref, vt0_ref, p0_ref, rows)
    half_step(u1_ref, a1_ref, thr1_ref, e11_ref, e21_ref, a0_ref, p0_ref, vt1_ref, p1_ref, 0)


def _peer_dense(xnt, u_all, vt_all, layer, thr, e1, e2, *, tt=512, eb=512):
    d, s = xnt.shape
    ne = u_all.shape[1]
    nk = PEER_NKEYS
    rows = eb // nk
    nb = ne // eb
    n = (s // tt) * nb
    assert nb % 2 == 0 and 2 * rows == SUBLANE
    split = lambda a: a.reshape(PEER_HEADS, nb // 2, SUBLANE, s)

    def pair(lag, half):
        return lambda i: divmod(jnp.clip(2 * i + half - lag, 0, n - 1), nb)

    def u_spec(half):
        return pl.BlockSpec((None, eb, d), lambda i: (layer, pair(0, half)(i)[1], 0))

    def vt_spec(half):
        return pl.BlockSpec((None, d, eb), lambda i: (layer, 0, pair(2, half)(i)[1]))

    def row_spec(half):
        return pl.BlockSpec((PEER_HEADS, 1, SUBLANE, tt),
                            lambda i: (0, pair(1, half)(i)[1] // 2, 0, pair(1, half)(i)[0]))

    def e2_spec(half):
        return pl.BlockSpec((PEER_HEADS, tt // LANE, nk, LANE), lambda i: (0, pair(1, half)(i)[0], 0, 0))

    slab = lambda dtype: pltpu.VMEM((tt // LANE, eb, LANE), dtype)
    return pl.pallas_call(
        functools.partial(_peer_dense_body, tt=tt, eb=eb, nb=nb),
        grid=(n // 2 + 1,),
        in_specs=[
            pl.BlockSpec((d, tt), lambda i: (0, pair(0, 0)(i)[0])),
            u_spec(0), u_spec(1), vt_spec(0), vt_spec(1),
            row_spec(0), row_spec(1), row_spec(0), row_spec(1), e2_spec(0), e2_spec(1),
        ],
        out_specs=pl.BlockSpec((d, tt), lambda i: (0, pair(2, 0)(i)[0])),
        out_shape=jax.ShapeDtypeStruct((d, s), F32),
        scratch_shapes=[slab(F32), slab(F32), slab(BF16), slab(BF16)],
        compiler_params=_params("arbitrary"),
        name="peer_dense",
    )(xnt, u_all, u_all, vt_all, vt_all, split(thr), split(thr), split(e1), split(e1), e2, e2)


def _add_t_body(h_ref, pt_ref, g_ref, o_ref, *, final_norm):
    y = h_ref[...] + pt_ref[...].T
    o_ref[...] = _rms(y, g_ref[...]) if final_norm else y


def _add_transposed(h, pt, g=None, *, tm=512):
    s, d = h.shape
    gain = jnp.ones((1, d), F32) if g is None else g.reshape(1, d)
    return pl.pallas_call(
        functools.partial(_add_t_body, final_norm=g is not None),
        grid=(s // tm,),
        in_specs=[
            pl.BlockSpec((tm, d), lambda i: (i, 0)),
            pl.BlockSpec((d, tm), lambda i: (0, i)),
            pl.BlockSpec((1, d), lambda i: (0, 0)),
        ],
        out_specs=pl.BlockSpec((tm, d), lambda i: (i, 0)),
        out_shape=jax.ShapeDtypeStruct((s, d), F32),
        compiler_params=_params("parallel"),
        name="add_transposed",
    )(h, pt, gain)


def _peer_ffn(h, g, w_query, sub_keys, u_all, vt_all, layer, final_g=None):
    nk = PEER_NKEYS
    keys = sub_keys.reshape(2 * PEER_HEADS, nk, -1).astype(BF16)
    xnt, thr, e1, e2 = _peer_route(h, g, w_query.T.astype(BF16), keys)
    pt = _peer_dense(xnt, u_all, vt_all, layer, thr, e1, e2)
    return _add_transposed(h, pt, final_g)


def _mlstm_prep_body(qp_ref, q_ref, qn_ref, kp_ref, k_ref, kn_ref, v_ref, cwq_ref, cwk_ref,
                     qo_ref, kto_ref, vo_ref, xc_ref, *, tm):
    i = pl.program_id(0)
    last = pl.num_programs(0) - 1
    halo = SUBLANE
    pad = CONV_W // 2

    def conv_silu(prev_ref, x_ref, next_ref, w_ref):
        xc_ref[halo:halo + tm, :] = x_ref[...]
        xc_ref[0:halo, :] = jnp.where(i > 0, prev_ref[...], 0.0)
        xc_ref[halo + tm:2 * halo + tm, :] = jnp.where(i < last, next_ref[...], 0.0)
        acc = xc_ref[halo - pad:halo - pad + tm, :] * w_ref[0:1, :]
        for j in range(1, CONV_W):
            acc = acc + xc_ref[halo - pad + j:halo - pad + j + tm, :] * w_ref[j:j + 1, :]
        return acc * jax.nn.sigmoid(acc)

    qo_ref[...] = conv_silu(qp_ref, q_ref, qn_ref, cwq_ref).astype(BF16)
    kc = conv_silu(kp_ref, k_ref, kn_ref, cwk_ref) * (ML_HEAD_DIM ** -0.5)
    kto_ref[...] = kc.T.astype(BF16)
    vo_ref[...] = v_ref[...].astype(BF16)


def _mlstm_prep(proj, conv_w, *, tm=256):
    s = proj.shape[0]
    w = ML_W
    nb = tm // SUBLANE
    prev = lambda c: pl.BlockSpec((SUBLANE, w), lambda i: (jnp.maximum(i * nb - 1, 0), c))
    main = lambda c: pl.BlockSpec((tm, w), lambda i: (i, c))
    nxt = lambda c: pl.BlockSpec((SUBLANE, w), lambda i: (jnp.minimum((i + 1) * nb, s // SUBLANE - 1), c))
    cw = pl.BlockSpec((CONV_W, w), lambda i: (0, 0))
    return pl.pallas_call(
        functools.partial(_mlstm_prep_body, tm=tm),
        grid=(s // tm,),
        in_specs=[prev(COL_MQ), main(COL_MQ), nxt(COL_MQ), prev(COL_MK), main(COL_MK), nxt(COL_MK),
                  main(COL_MV), cw, cw],
        out_specs=[
            pl.BlockSpec((tm, w), lambda i: (i, 0)),
            pl.BlockSpec((w, tm), lambda i: (0, i)),
            pl.BlockSpec((tm, w), lambda i: (i, 0)),
        ],
        out_shape=[
            jax.ShapeDtypeStruct((s, w), BF16),
            jax.ShapeDtypeStruct((w, s), BF16),
            jax.ShapeDtypeStruct((s, w), BF16),
        ],
        scratch_shapes=[pltpu.VMEM((tm + 2 * SUBLANE, w), F32)],
        compiler_params=_params("parallel"),
        name="mlstm_prep",
    )(proj, proj, proj, proj, proj, proj, proj, conv_w[:, :w], conv_w[:, w:])


def _log_sigmoid(x):
    return jnp.minimum(x, 0.0) - jnp.log1p(jnp.exp(-jnp.abs(x)))


def _mlstm_direction(q_ref, kt_ref, v_ref, g_ref, bias_ref, o_ref, s_ref, m_ref, *, backward):
    L = ML_CHUNK
    d = ML_HEAD_DIM
    row = lax.broadcasted_iota(jnp.int32, (L, L), 0)
    col = lax.broadcasted_iota(jnp.int32, (L, L), 1)
    keep = (col >= row) if backward else (col <= row)
    tri = keep.astype(BF16)
    ones = jnp.ones((L, LANE), BF16)

    gates = g_ref[...] + bias_ref[...]
    logf = _log_sigmoid(gates)
    hi = logf.astype(BF16)
    lo = (logf - hi.astype(F32)).astype(BF16)
    bcol = jnp.dot(tri, hi, preferred_element_type=F32) + jnp.dot(tri, lo, preferred_element_type=F32)
    gates_t = gates.T
    b_t = bcol.T
    base = 2 * ML_HEADS if backward else 0
    for h in range(ML_HEADS):
        idx = base // 2 + h
        ci, cf = base + h, base + ML_HEADS + h
        i_row = gates_t[ci:ci + 1, :]
        b_row = b_t[cf:cf + 1, :]
        b_col = bcol[:, cf:cf + 1]
        g_tot = b_row[:, 0:1] if backward else b_row[:, L - 1:L]
        a_row = g_tot - b_row + i_row
        a_max = jnp.max(a_row, axis=1, keepdims=True)
        w_row = jnp.exp(a_row - a_max)

        q = q_ref[:, h * d:(h + 1) * d]
        kt = kt_ref[h * d:(h + 1) * d, :]
        v_aug = jnp.concatenate([v_ref[:, h * d:(h + 1) * d], ones], axis=1)
        s_prev = s_ref[idx]
        m_prev = m_ref[idx][:, 0:1]

        dlog = jnp.where(keep, b_col + (i_row - b_row), NEG_INF)
        inter_log = b_col + m_prev
        m_j = jnp.maximum(inter_log, jnp.max(dlog, axis=1, keepdims=True))
        sqk = jnp.dot(q, kt, preferred_element_type=F32) * jnp.exp(dlog - m_j)
        inter_w = jnp.exp(inter_log - m_j)
        tot = (inter_w * jnp.dot(q, s_prev.astype(BF16), preferred_element_type=F32)
               + jnp.dot(sqk.astype(BF16), v_aug, preferred_element_type=F32))
        den = tot[:, d:]
        inv = 1.0 / jnp.maximum(jnp.abs(den), jnp.exp(-m_j))
        o_ref[:, h * d:(h + 1) * d] = tot[:, :d] * jnp.concatenate([inv] * (d // LANE), axis=1)

        ktw = (kt.astype(F32) * w_row).astype(BF16)
        upd = jnp.dot(ktw, v_aug, preferred_element_type=F32)
        m_new = jnp.maximum(g_tot + m_prev, a_max)
        s_ref[idx] = jnp.exp(g_tot + m_prev - m_new) * s_prev + jnp.exp(a_max - m_new) * upd
        m_ref[idx] = jnp.broadcast_to(m_new, (1, LANE))


def _mlstm_body(qf_ref, ktf_ref, vf_ref, gf_ref, qb_ref, ktb_ref, vb_ref, gb_ref, bias_ref,
                hf_ref, hb_ref, s_ref, m_ref):
    @pl.when(pl.program_id(0) == 0)
    def _():
        s_ref[...] = jnp.zeros(s_ref.shape, F32)
        m_ref[...] = jnp.zeros(m_ref.shape, F32)

    _mlstm_direction(qf_ref, ktf_ref, vf_ref, gf_ref, bias_ref, hf_ref, s_ref, m_ref, backward=False)
    _mlstm_direction(qb_ref, ktb_ref, vb_ref, gb_ref, bias_ref, hb_ref, s_ref, m_ref, backward=True)


def _mlstm_scan(mq, mkt, mv, gates, b_gate):
    s, w = mq.shape
    L = ML_CHUNK
    nc = s // L
    d = ML_HEAD_DIM
    bias = jnp.pad(b_gate, (0, LANE - b_gate.shape[0])).reshape(1, LANE)
    fwd = lambda c: c
    bwd = lambda c: nc - 1 - c
    rows = lambda ix, n: pl.BlockSpec((L, n), lambda c: (ix(c), 0))
    cols = lambda ix: pl.BlockSpec((w, L), lambda c: (0, ix(c)))
    return pl.pallas_call(
        _mlstm_body,
        grid=(nc,),
        in_specs=[rows(fwd, w), cols(fwd), rows(fwd, w), rows(fwd, LANE),
                  rows(bwd, w), cols(bwd), rows(bwd, w), rows(bwd, LANE),
                  pl.BlockSpec((1, LANE), lambda c: (0, 0))],
        out_specs=[rows(fwd, w), rows(bwd, w)],
        out_shape=[jax.ShapeDtypeStruct((s, w), F32)] * 2,
        scratch_shapes=[pltpu.VMEM((2 * ML_HEADS, d, d + LANE), F32), pltpu.VMEM((2 * ML_HEADS, 1, LANE), F32)],
        compiler_params=_params("arbitrary"),
        name="mlstm_scan",
    )(mq, mkt, mv, gates, mq, mkt, mv, gates, bias)


def _mix_out_body(ya_ref, hf_ref, hb_ref, mo_ref, mg_ref, wa_ref, wm_ref, r_ref, o_ref, yml_ref):
    @pl.when(pl.program_id(1) == 0)
    def _():
        d = ML_HEAD_DIM
        for h in range(ML_HEADS):
            sl = slice(h * d, (h + 1) * d)
            hm = _rms(hf_ref[:, sl] + hb_ref[:, sl], mg_ref[:, sl])
            yml_ref[:, sl] = (hm * jax.nn.sigmoid(mo_ref[:, sl])).astype(BF16)

    o_ref[...] = (r_ref[...] + jnp.dot(ya_ref[...], wa_ref[...], preferred_element_type=F32)
                  + jnp.dot(yml_ref[...], wm_ref[...], preferred_element_type=F32))


def _mix_out(y_att, hf, hb, proj, ml_gain, w_att, w_ml, res, *, tm=512, tn=1024):
    s, n = res.shape
    wa, wm = w_att.shape[0], w_ml.shape[0]
    return pl.pallas_call(
        _mix_out_body,
        grid=(s // tm, n // tn),
        in_specs=[
            pl.BlockSpec((tm, wa), lambda i, j: (i, 0)),
            pl.BlockSpec((tm, wm), lambda i, j: (i, 0)),
            pl.BlockSpec((tm, wm), lambda i, j: (i, 0)),
            pl.BlockSpec((tm, wm), lambda i, j: (i, COL_MO)),
            pl.BlockSpec((1, wm), lambda i, j: (0, 0)),
            pl.BlockSpec((wa, tn), lambda i, j: (0, j)),
            pl.BlockSpec((wm, tn), lambda i, j: (0, j)),
            pl.BlockSpec((tm, tn), lambda i, j: (i, j)),
        ],
        out_specs=pl.BlockSpec((tm, tn), lambda i, j: (i, j)),
        out_shape=jax.ShapeDtypeStruct((s, n), F32),
        scratch_shapes=[pltpu.VMEM((tm, wm), BF16)],
        compiler_params=_params("parallel", "arbitrary"),
        name="mix_out",
    )(y_att, hf, hb, proj, ml_gain.reshape(1, wm), w_att, w_ml, res)


def kernel(x, norm_mix_g, norm_ffn_g, w_in, b_gate, conv_w, q_gain, k_gain, ml_gain, w_out, pool_w,
           layer_scale, peer_wq, peer_keys, peer_u, peer_v, norm_f):
    b, s, d = x.shape
    h = x.reshape(b * s, d)

    wi = w_in[0]
    kv_end = ATT_Q + 2 * ATT_KV
    w_main = jnp.concatenate([wi[:, :ATT_Q], wi[:, kv_end:MAIN_W], wi[:, ATT_Q:kv_end]], axis=1).astype(BF16)
    w_gate = jnp.pad(wi[:, MAIN_W:], ((0, 0), (0, LANE - (wi.shape[1] - MAIN_W)))).astype(BF16)
    proj, gates = _norm_matmul(h, norm_mix_g[0], w_main, w_gate)
    qt, k, vt = _attn_prep(proj, q_gain[0], k_gain[0])
    y_att = _attention(qt, k, vt)
    mq, mkt, mv = _mlstm_prep(proj, conv_w[0])
    hf, hb = _mlstm_scan(mq, mkt, mv, gates, b_gate[0])
    wo = w_out[0].astype(BF16)
    h = _mix_out(y_att, hf, hb, proj, ml_gain[0], wo[:ATT_Q], wo[ATT_Q:], h)
    u_all = peer_u.astype(BF16)
    vt_all = jnp.swapaxes(peer_v, 1, 2).astype(BF16)
    h = _peer_ffn(h, norm_ffn_g[0], peer_wq[0], peer_keys[0], u_all, vt_all, 0)

    h = _pool_mixer(h, norm_mix_g[1], pool_w[0], layer_scale[0])
    h = _peer_ffn(h, norm_ffn_g[1], peer_wq[1], peer_keys[1], u_all, vt_all, 1, final_g=norm_f)
    return h.reshape(b, s, d)
```
